```python
import jax, jax.numpy as jnp
from jax import lax
import numpy as np


D_MODEL = 2048
BATCH = 4
SEQ = 4096
DEPTH = 2

GRID_W = 64
CTX_LEN = 256
EPS = 1e-6
F32 = jnp.float32

MLSTM_HEADS = 4
MLSTM_DK = 256
MLSTM_DV = 256
MLSTM_CHUNK = 64
MLSTM_QK = MLSTM_HEADS * MLSTM_DK
MLSTM_V = MLSTM_HEADS * MLSTM_DV
MLSTM_GATES = 2 * 2 * MLSTM_HEADS
NA_HEADS = 8
NA_DIM = 128
NA_WIDTH = NA_HEADS * NA_DIM
NA_KH = 8
NA_KW = 16
GQA_HEADS = 16
GQA_KV_HEADS = 4
GQA_DIM = 64
GQA_Q = GQA_HEADS * GQA_DIM
GQA_KV = GQA_KV_HEADS * GQA_DIM
GQA_WINDOW = 128
GQA_BLOCK = 128
ROPE_THETA = 10000.0
LRU_WIDTH = 1024
LRU_BLOCKS = 8
LRU_BS = LRU_WIDTH // LRU_BLOCKS
LRU_CONV = 4
LRU_C = 8.0
N_EXPERTS = 16
EC_FACTOR = 2
EXPERT_FF = 2048

AB_SIZES = (MLSTM_QK, MLSTM_QK, MLSTM_V, MLSTM_V, MLSTM_GATES, NA_WIDTH, NA_WIDTH, NA_WIDTH)
AB_IN = 2 * MLSTM_QK + 2 * MLSTM_V + MLSTM_GATES + 3 * NA_WIDTH
AB_OUT = MLSTM_V + NA_WIDTH
CD_SIZES = (GQA_Q, GQA_KV, GQA_KV, LRU_WIDTH, LRU_WIDTH)
CD_IN = GQA_Q + 2 * GQA_KV + 2 * LRU_WIDTH
CD_OUT = GQA_Q + LRU_WIDTH

kernel_name = 'hybrid_mlstm_na_swa_rglru_ecmoe_dit'


def _split(t, sizes):
    offs = []
    acc = 0
    for s in sizes[:-1]:
        acc += s
        offs.append(acc)
    return jnp.split(t, offs, axis=-1)


def _rmsnorm(x, w):
    x32 = x.astype(F32)
    y = x32 * lax.rsqrt(jnp.mean(x32 * x32, axis=-1, keepdims=True) + EPS)
    return (y * w.astype(F32)).astype(x.dtype)


def _modulate(h, shift, scale):
    return h * (1.0 + scale) + shift


def _rope_1d(x, pos):
    half = x.shape[-1] // 2
    inv = ROPE_THETA ** (-(jnp.arange(half, dtype=F32) / half))
    ang = pos.astype(F32)[:, None] * inv[None, :]
    cos = jnp.cos(ang)[None, :, None, :]
    sin = jnp.sin(ang)[None, :, None, :]
    x32 = x.astype(F32)
    x1, x2 = x32[..., :half], x32[..., half:]
    return jnp.concatenate([x1 * cos - x2 * sin, x2 * cos + x1 * sin], axis=-1).astype(x.dtype)


def _rope_2d(x, row, col):
    half = x.shape[-1] // 2
    return jnp.concatenate([_rope_1d(x[..., :half], row), _rope_1d(x[..., half:], col)], axis=-1)


def _dense_attn(q, k, v, sink):
    B, L, Hq, d = q.shape
    Hkv = k.shape[2]
    G = Hq // Hkv
    qg = q.reshape(B, L, Hkv, G, d)
    s = jnp.einsum('bqkgd,bskd->bkgqs', qg, k).astype(F32) * (d ** -0.5)
    if sink is not None:
        sk = jnp.broadcast_to(sink.reshape(Hkv, G).astype(F32)[None, :, :, None, None], (B, Hkv, G, L, 1))
        s = jnp.concatenate([s, sk], axis=-1)
    p = jax.nn.softmax(s, axis=-1)[..., :L].astype(v.dtype)
    return jnp.einsum('bkgqs,bskd->bqkgd', p, v).reshape(B, L, Hq * d)


def _mlstm_prep(q, k, v, o, g, gate_b):
    B, N, _ = q.shape
    H = MLSTM_HEADS

    def heads(t, d):
        return jnp.transpose(t.reshape(B, N, H, d), (0, 2, 1, 3)).astype(F32)

    q = heads(q, MLSTM_DK)
    k = heads(k, MLSTM_DK) * (MLSTM_DK ** -0.5)
    v = heads(v, MLSTM_DV)
    g = g.reshape(B, N, 2, 2, H).astype(F32) + gate_b.astype(F32)
    g = jnp.transpose(g, (2, 3, 0, 4, 1))
    ig = g[:, 0]
    lf = jax.nn.log_sigmoid(g[:, 1])
    return q, k, v, o, ig, lf


def _mlstm_scan(q, k, v, ig, lf, state, emit):
    B, H, N, _ = q.shape
    L = MLSTM_CHUNK
    nc = N // L

    def chunks(t):
        return jnp.moveaxis(t.reshape((B, H, nc, L) + t.shape[3:]), 2, 0)

    causal = jnp.tril(jnp.ones((L, L), dtype=bool))

    def step(carry, xs):
        C, n, m = carry
        qc, kc, vc, ic, fc = xs
        b = jnp.cumsum(fc, axis=-1)
        dmat = jnp.where(causal, b[..., :, None] - b[..., None, :] + ic[..., None, :], -jnp.inf)
        inter = b + m[..., None]
        mj = jnp.maximum(inter, jnp.max(dmat, axis=-1))
        s = jnp.einsum('bhjd,bhsd->bhjs', qc, kc) * jnp.exp(dmat - mj[..., None])
        w_inter = jnp.exp(inter - mj)
        num = w_inter[..., None] * jnp.einsum('bhjd,bhde->bhje', qc, C) + jnp.einsum('bhjs,bhse->bhje', s, vc)
        den = w_inter * jnp.einsum('bhjd,bhd->bhj', qc, n) + jnp.sum(s, axis=-1)
        h = num / jnp.maximum(jnp.abs(den), jnp.exp(-mj))[..., None]
        g = b[..., -1:] - b + ic
        m_new = jnp.maximum(b[..., -1] + m, jnp.max(g, axis=-1))
        decay = jnp.exp(b[..., -1] + m - m_new)
        wg = jnp.exp(g - m_new[..., None])
        C = decay[..., None, None] * C + jnp.einsum('bhs,bhsd,bhse->bhde', wg, kc, vc)
        n = decay[..., None] * n + jnp.einsum('bhs,bhsd->bhd', wg, kc)
        return (C, n, m_new), (h if emit else None)

    state, h = lax.scan(step, state, (chunks(q), chunks(k), chunks(v), chunks(ig), chunks(lf)))
    if emit:
        h = jnp.moveaxis(h, 0, 2).reshape(B, H, N, v.shape[-1])
    return h, state


def _mlstm_out(h, o, gain):
    B, H, N, dv = h.shape
    hn = _rmsnorm(jnp.transpose(h, (0, 2, 1, 3)), gain.reshape(H, dv)).reshape(B, N, H * dv)
    return (hn * jax.nn.sigmoid(o.astype(F32))).astype(o.dtype)


def _mlstm_bidir(pl, pc, gate_b, head_norm, need_ctx):
    ql, kl, vl, ol, igl, lfl = _mlstm_prep(*pl, gate_b)
    qc, kc, vc, oc, igc, lfc = _mlstm_prep(*pc, gate_b)
    B = ql.shape[0]
    zero = (jnp.zeros((B, MLSTM_HEADS, MLSTM_DK, MLSTM_DV), F32),
            jnp.zeros((B, MLSTM_HEADS, MLSTM_DK), F32),
            jnp.zeros((B, MLSTM_HEADS), F32))
    hs_l, hs_c = [], []
    for d, rev in ((0, False), (1, True)):
        fl = (lambda t: jnp.flip(t, axis=2)) if rev else (lambda t: t)
        h_c, st = _mlstm_scan(fl(qc), fl(kc), fl(vc), fl(igc[d]), fl(lfc[d]), zero, need_ctx)
        h_l, _ = _mlstm_scan(fl(ql), fl(kl), fl(vl), fl(igl[d]), fl(lfl[d]), st, True)
        hs_l.append(fl(h_l))
        if need_ctx:
            hs_c.append(fl(h_c))
    out_l = _mlstm_out(hs_l[0] + hs_l[1], ol, head_norm)
    out_c = _mlstm_out(hs_c[0] + hs_c[1], oc, head_norm) if need_ctx else None
    return out_l, out_c


def _neighbourhood_attn(pl, pc, rpb, rows_n, need_ctx):
    B, N, _ = pl[0].shape
    H, d = NA_HEADS, NA_DIM
    ql, kl, vl = [t.reshape(B, N, H, d) for t in pl]
    qc, kc, vc = [t.reshape(B, t.shape[1], H, d) for t in pc]
    kh = min(NA_KH, rows_n)
    kw = NA_KW
    nk = kh * kw
    scale = d ** -0.5
    kg = kl.reshape(B, rows_n, GRID_W, H, d)
    vg = vl.reshape(B, rows_n, GRID_W, H, d)
    qg = jnp.moveaxis(ql.reshape(B, rows_n, GRID_W, H, d), 1, 0)
    col_start = np.clip(np.arange(GRID_W) - kw // 2, 0, GRID_W - kw)
    col_idx = col_start[:, None] + np.arange(kw)[None, :]
    dcol = col_idx - np.arange(GRID_W)[:, None] + (NA_KW - 1)

    def row_step(args):
        r, qr = args
        r0 = jnp.clip(r - kh // 2, 0, rows_n - kh)
        kr = lax.dynamic_slice_in_dim(kg, r0, kh, axis=1)[:, :, col_idx]
        vr = lax.dynamic_slice_in_dim(vg, r0, kh, axis=1)[:, :, col_idx]
        kr = jnp.transpose(kr, (0, 2, 1, 3, 4, 5)).reshape(B, GRID_W, nk, H, d)
        vr = jnp.transpose(vr, (0, 2, 1, 3, 4, 5)).reshape(B, GRID_W, nk, H, d)
        drow = r0 + jnp.arange(kh) - r + (NA_KH - 1)
        bias = rpb[:, drow][:, :, dcol]
        bias = jnp.transpose(bias, (2, 0, 1, 3)).reshape(GRID_W, H, nk).astype(F32)
        s_win = jnp.einsum('bwhd,bwkhd->bwhk', qr, kr).astype(F32) * scale + bias
        s_ctx = jnp.einsum('bwhd,bchd->bwhc', qr, kc).astype(F32) * scale
        p = jax.nn.softmax(jnp.concatenate([s_win, s_ctx], axis=-1), axis=-1).astype(vr.dtype)
        return (jnp.einsum('bwhk,bwkhd->bwhd', p[..., :nk], vr)
                + jnp.einsum('bwhc,bchd->bwhd', p[..., nk:], vc))

    o = lax.map(row_step, (jnp.arange(rows_n), qg))
    out_l = jnp.moveaxis(o, 0, 1).reshape(B, N, H * d)
    out_c = _dense_attn(qc, kc, vc, None) if need_ctx else None
    return out_l, out_c


def _mixer_ab(hl, hc, w_in, w_out, gate_b, head_norm, rpb, rows_n, need_ctx):
    pl = _split(hl @ w_in, AB_SIZES)
    pc = _split(hc @ w_in, AB_SIZES)
    ml, mc = _mlstm_bidir(pl[:5], pc[:5], gate_b, head_norm, need_ctx)
    nl, nc = _neighbourhood_attn(pl[5:], pc[5:], rpb, rows_n, need_ctx)
    yl = jnp.concatenate([ml, nl], axis=-1) @ w_out
    yc = (jnp.concatenate([mc, nc], axis=-1) @ w_out) if need_ctx else None
    return yl, yc


def _window_gqa(q, k, v, kc, vc, sink):
    B, N, Hq, d = q.shape
    Hkv = k.shape[2]
    G = Hq // Hkv
    Lc = kc.shape[1]
    nb = N // GQA_BLOCK
    span = GQA_BLOCK + 2 * GQA_WINDOW
    pad = ((0, 0), (GQA_WINDOW, GQA_WINDOW), (0, 0), (0, 0))
    kp = jnp.pad(k, pad)
    vp = jnp.pad(v, pad)
    qb = jnp.moveaxis(q.reshape(B, nb, GQA_BLOCK, Hkv, G, d), 1, 0)
    s_sink = jnp.broadcast_to(sink.reshape(Hkv, G).astype(F32)[None, :, :, None, None], (B, Hkv, G, GQA_BLOCK, 1))
    scale = d ** -0.5

    def block(args):
        i, qi = args
        start = i * GQA_BLOCK
        ki = lax.dynamic_slice_in_dim(kp, start, span, axis=1)
        vi = lax.dynamic_slice_in_dim(vp, start, span, axis=1)
        kpos = start - GQA_WINDOW + jnp.arange(span)
        qpos = start + jnp.arange(GQA_BLOCK)
        valid = ((kpos >= 0) & (kpos < N))[None, :] & (jnp.abs(qpos[:, None] - kpos[None, :]) <= GQA_WINDOW)
        s_win = jnp.einsum('bqkgd,bskd->bkgqs', qi, ki).astype(F32) * scale
        s_win = jnp.where(valid, s_win, -jnp.inf)
        s_ctx = jnp.einsum('bqkgd,bckd->bkgqc', qi, kc).astype(F32) * scale
        p = jax.nn.softmax(jnp.concatenate([s_win, s_ctx, s_sink], axis=-1), axis=-1).astype(v.dtype)
        return (jnp.einsum('bkgqs,bskd->bqkgd', p[..., :span], vi)
                + jnp.einsum('bkgqc,bckd->bqkgd', p[..., span:span + Lc], vc))

    o = lax.map(block, (jnp.arange(nb), qb))
    return jnp.moveaxis(o, 0, 1).reshape(B, N, Hq * d)


def _conv_centred(x, w, b):
    N = x.shape[1]
    lo = (LRU_CONV - 1) // 2
    hi = LRU_CONV - 1 - lo
    xp = jnp.pad(x, ((0, 0), (lo, hi), (0, 0)))
    out = xp[:, 0:N] * w[0] + b
    for j in range(1, LRU_CONV):
        out = out + xp[:, j:j + N] * w[j]
    return out


def _lin_combine(e1, e2):
    a1, b1 = e1
    a2, b2 = e2
    return a1 * a2, a2 * b1 + b2


def _rglru_dir(u, gw, gb, lam, h0, reverse):
    B, N, W = u.shape
    u32 = u.astype(F32)
    ub = u32.reshape(B, N, LRU_BLOCKS, LRU_BS)
    r = jax.nn.sigmoid(jnp.einsum('bnhi,hij->bnhj', ub, gw[0].astype(F32)).reshape(B, N, W) + gb[0].astype(F32))
    i = jax.nn.sigmoid(jnp.einsum('bnhi,hij->bnhj', ub, gw[1].astype(F32)).reshape(B, N, W) + gb[1].astype(F32))
    log_a = -LRU_C * r * jax.nn.softplus(-lam.astype(F32))
    a = jnp.exp(log_a)
    bterm = jnp.sqrt(-jnp.expm1(2.0 * log_a)) * (i * u32)
    if reverse:
        a = jnp.flip(a, axis=1)
        bterm = jnp.flip(bterm, axis=1)
    A, Hs = lax.associative_scan(_lin_combine, (a, bterm), axis=1)
    h = A * h0[:, None, :] + Hs
    final = h[:, -1]
    if reverse:
        h = jnp.flip(h, axis=1)
    return h, final


def _mixer_cd(hl, hc, w_in, w_out, sink, conv_w, conv_b, gate_w, gate_b, lam, row, col, need_ctx):
    B, N, _ = hl.shape
    Lc = hc.shape[1]
    ql, kl, vl, xl_r, gl_r = _split(hl @ w_in, CD_SIZES)
    qc, kc, vc, xc_r, gc_r = _split(hc @ w_in, CD_SIZES)
    ql = _rope_2d(ql.reshape(B, N, GQA_HEADS, GQA_DIM), row, col)
    kl = _rope_2d(kl.reshape(B, N, GQA_KV_HEADS, GQA_DIM), row, col)
    vl = vl.reshape(B, N, GQA_KV_HEADS, GQA_DIM)
    qc = qc.reshape(B, Lc, GQA_HEADS, GQA_DIM)
    kc = kc.reshape(B, Lc, GQA_KV_HEADS, GQA_DIM)
    vc = vc.reshape(B, Lc, GQA_KV_HEADS, GQA_DIM)
    al = _window_gqa(ql, kl, vl, kc, vc, sink)
    ul = _conv_centred(xl_r, conv_w, conv_b)
    uc = _conv_centred(xc_r, conv_w, conv_b)
    zero = jnp.zeros((B, LRU_WIDTH), F32)
    hs_l, hs_c = [], []
    for d, rev in ((0, False), (1, True)):
        h_c, st = _rglru_dir(uc, gate_w[d], gate_b[d], lam[d], zero, rev)
        h_l, _ = _rglru_dir(ul, gate_w[d], gate_b[d], lam[d], st, rev)
        hs_l.append(h_l)
        hs_c.append(h_c)
    rl = ((hs_l[0] + hs_l[1]) * jax.nn.gelu(gl_r.astype(F32))).astype(hl.dtype)
    yl = jnp.concatenate([al, rl], axis=-1) @ w_out
    yc = None
    if need_ctx:
        ac = _dense_attn(qc, kc, vc, sink)
        rc = ((hs_c[0] + hs_c[1]) * jax.nn.gelu(gc_r.astype(F32))).astype(hc.dtype)
        yc = jnp.concatenate([ac, rc], axis=-1) @ w_out
    return yl, yc


def _ec_moe(h, router, w1, w3, w2):
    B, T, D = h.shape
    cap = EC_FACTOR * T // N_EXPERTS
    aff = jax.nn.softmax((h @ router).astype(F32), axis=-1)
    g, idx = lax.top_k(jnp.swapaxes(aff, 1, 2), cap)
    xs = jax.vmap(lambda hb, ib: hb[ib])(h, idx)
    u = jnp.einsum('becd,edf->becf', xs, w1)
    v = jnp.einsum('becd,edf->becf', xs, w3)
    y = jnp.einsum('becf,efd->becd', jax.nn.silu(u) * v, w2) * g[..., None].astype(h.dtype)
    return jax.vmap(lambda ib, yb: jnp.zeros((T, D), yb.dtype).at[ib.reshape(-1)].add(yb.reshape(-1, D)))(idx, y)


def setup_inputs(seed: int = 0) -> dict:
    key = jax.random.key(seed)
    ks = iter(jax.random.split(key, 40))

    def nrm(shape, std):
        return jax.random.normal(next(ks), shape, F32) * std

    D = D_MODEL
    n_even = (DEPTH + 1) // 2
    n_odd = DEPTH // 2
    gate_base = jnp.stack([jnp.zeros((MLSTM_HEADS,), F32), jnp.linspace(3.0, 6.0, MLSTM_HEADS)])
    u = jax.random.uniform(next(ks), (n_odd, 2, LRU_WIDTH), F32, 0.9, 0.999)
    a = u ** (1.0 / LRU_C)
    return {
        'x': nrm((BATCH, SEQ, D), 1.0),
        'c': nrm((BATCH, D), 1.0),
        'ctx': nrm((BATCH, CTX_LEN, D), 1.0),
        'c_ctx': nrm((D,), 1.0),
        'ada_w': nrm((DEPTH, D, 6 * D), 0.5 * D ** -0.5),
        'ada_b': nrm((DEPTH, 6 * D), 0.02),
        'norm_mix_pre': 1.0 + nrm((DEPTH, D), 0.1),
        'norm_mix_post': 1.0 + nrm((DEPTH, D), 0.1),
        'norm_ffn_pre': 1.0 + nrm((DEPTH, D), 0.1),
        'norm_ffn_post': 1.0 + nrm((DEPTH, D), 0.1),
        'ab_w_in': nrm((n_even, D, AB_IN), D ** -0.5),
        'ab_w_out': nrm((n_even, AB_OUT, D), AB_OUT ** -0.5),
        'mlstm_gate_b': gate_base[None, None] + nrm((n_even, 2, 2, MLSTM_HEADS), 0.1),
        'mlstm_norm': 1.0 + nrm((n_even, MLSTM_V), 0.1),
        'na_rpb': nrm((n_even, NA_HEADS, 2 * NA_KH - 1, 2 * NA_KW - 1), 0.1),
        'cd_w_in': nrm((n_odd, D, CD_IN), D ** -0.5),
        'cd_w_out': nrm((n_odd, CD_OUT, D), CD_OUT ** -0.5),
        'gqa_sink': nrm((n_odd, GQA_HEADS), 0.5),
        'lru_conv_w': nrm((n_odd, LRU_CONV, LRU_WIDTH), LRU_CONV ** -0.5),
        'lru_conv_b': nrm((n_odd, LRU_WIDTH), 0.02),
        'lru_gate_w': nrm((n_odd, 2, 2, LRU_BLOCKS, LRU_BS, LRU_BS), LRU_BS ** -0.5),
        'lru_gate_b': nrm((n_odd, 2, 2, LRU_WIDTH), 0.1),
        'lru_lambda': jnp.log(a) - jnp.log1p(-a),
        'moe_router': nrm((DEPTH, D, N_EXPERTS), D ** -0.5),
        'moe_w1': nrm((DEPTH, N_EXPERTS, D, EXPERT_FF), D ** -0.5),
        'moe_w3': nrm((DEPTH, N_EXPERTS, D, EXPERT_FF), D ** -0.5),
        'moe_w2': nrm((DEPTH, N_EXPERTS, EXPERT_FF, D), EXPERT_FF ** -0.5),
    }


def reference(x, c, ctx, c_ctx, ada_w, ada_b, norm_mix_pre, norm_mix_post, norm_ffn_pre, norm_ffn_post,
              ab_w_in, ab_w_out, mlstm_gate_b, mlstm_norm, na_rpb,
              cd_w_in, cd_w_out, gqa_sink, lru_conv_w, lru_conv_b, lru_gate_w, lru_gate_b, lru_lambda,
              moe_router, moe_w1, moe_w3, moe_w2):
    B, N, D = x.shape
    rows_n = N // GRID_W
    t = jnp.arange(N)
    row = t // GRID_W
    col = t % GRID_W
    cond_l = jax.nn.silu(c.astype(F32))
    cond_c = jax.nn.silu(c_ctx.astype(F32))[None]
    xl, xc = x, ctx
    for l in range(DEPTH):
        last = l == DEPTH - 1
        j = l // 2
        mod_l = (cond_l @ ada_w[l].astype(F32) + ada_b[l].astype(F32)).astype(x.dtype)[:, None]
        mod_c = (cond_c @ ada_w[l].astype(F32) + ada_b[l].astype(F32)).astype(x.dtype)[:, None]
        sh1, sc1, g1, sh2, sc2, g2 = jnp.split(mod_l, 6, axis=-1)
        sh1c, sc1c, g1c, sh2c, sc2c, g2c = jnp.split(mod_c, 6, axis=-1)
        hl = _modulate(_rmsnorm(xl, norm_mix_pre[l]), sh1, sc1)
        hc = _modulate(_rmsnorm(xc, norm_mix_pre[l]), sh1c, sc1c)
        if l % 2 == 0:
            yl, yc = _mixer_ab(hl, hc, ab_w_in[j], ab_w_out[j], mlstm_gate_b[j], mlstm_norm[j], na_rpb[j],
                               rows_n, not last)
        else:
            yl, yc = _mixer_cd(hl, hc, cd_w_in[j], cd_w_out[j], gqa_sink[j], lru_conv_w[j], lru_conv_b[j],
                               lru_gate_w[j], lru_gate_b[j], lru_lambda[j], row, col, not last)
        xl = xl + g1 * _rmsnorm(yl, norm_mix_post[l])
        hl = _modulate(_rmsnorm(xl, norm_ffn_pre[l]), sh2, sc2)
        xl = xl + g2 * _rmsnorm(_ec_moe(hl, moe_router[l], moe_w1[l], moe_w3[l], moe_w2[l]), norm_ffn_post[l])
        if not last:
            xc = xc + g1c * _rmsnorm(yc, norm_mix_post[l])
            hc = _modulate(_rmsnorm(xc, norm_ffn_pre[l]), sh2c, sc2c)
            xc = xc + g2c * _rmsnorm(_ec_moe(hc, moe_router[l], moe_w1[l], moe_w3[l], moe_w2[l]), norm_ffn_post[l])
    return xl
```

```python
import functools

import numpy as np
import jax
import jax.numpy as jnp
from jax import lax
from jax.experimental import pallas as pl
from jax.experimental.pallas import tpu as pltpu

F32 = jnp.float32
BF16 = jnp.bfloat16
EPS = 1e-6
NEG = -1e30

GRID_W = 64
MLSTM_HEADS = 4
MLSTM_DK = 256
MLSTM_DV = 256
MLSTM_QK = MLSTM_HEADS * MLSTM_DK
MLSTM_V = MLSTM_HEADS * MLSTM_DV
MLSTM_GATES = 2 * 2 * MLSTM_HEADS
NA_HEADS = 8
NA_DIM = 128
NA_WIDTH = NA_HEADS * NA_DIM
NA_KH = 8
NA_KW = 16
GQA_HEADS = 16
GQA_KV_HEADS = 4
GQA_DIM = 64
GQA_Q = GQA_HEADS * GQA_DIM
GQA_KV = GQA_KV_HEADS * GQA_DIM
GQA_WINDOW = 128
ROPE_THETA = 10000.0
LRU_WIDTH = 1024
LRU_BLOCKS = 8
LRU_BS = LRU_WIDTH // LRU_BLOCKS
LRU_C = 8.0
N_EXPERTS = 16
EC_FACTOR = 2

TS = 256
LANE = 128
SUB = 8
VMEM_LIMIT = 56 * 1024 * 1024


def _cp(*sem):
    return pltpu.CompilerParams(dimension_semantics=sem, vmem_limit_bytes=VMEM_LIMIT)


def _pick(n, prefs):
    for p in prefs:
        if n % p == 0:
            return p
    return n


def _sigmoid(x):
    return 1.0 / (1.0 + jnp.exp(-x))


def _dot(a, b):
    return jnp.dot(a, b, preferred_element_type=F32)


def _dot_nt(a, b):
    return lax.dot_general(a, b, (((1,), (1,)), ((), ())), preferred_element_type=F32)


def _dot_tn(a, b):
    return lax.dot_general(a, b, (((0,), (0,)), ((), ())), preferred_element_type=F32)


def _rms(x, w):
    return x * lax.rsqrt(jnp.mean(x * x, axis=-1, keepdims=True) + EPS) * w


def _ada_kernel(c_ref, w_ref, b_ref, o_ref):
    c = c_ref[...]
    a = (c * _sigmoid(c)).astype(BF16)
    o_ref[0] = _dot(a, w_ref[0].astype(BF16)) + b_ref[0]


def _ada(cond, ada_w, ada_b):
    L, D, D6 = ada_w.shape
    tn = _pick(D6, (1024, 512, 256, 128))
    return pl.pallas_call(
        _ada_kernel, grid=(L, D6 // tn),
        in_specs=[pl.BlockSpec((SUB, D), lambda l, j: (0, 0)),
                  pl.BlockSpec((1, D, tn), lambda l, j: (l, 0, j)),
                  pl.BlockSpec((1, 1, tn), lambda l, j: (l, 0, j))],
        out_specs=pl.BlockSpec((1, SUB, tn), lambda l, j: (l, 0, j)),
        out_shape=jax.ShapeDtypeStruct((L, SUB, D6), F32),
        compiler_params=_cp("parallel", "parallel"), name="ada",
    )(cond, ada_w, ada_b.reshape(L, 1, D6))


def _mod_map(l, k, ctx_row, has_ctx):
    if has_ctx:
        return lambda b, s: ((l * SUB + jnp.where(s == 0, ctx_row, b)) * 6 + k, 0, 0)
    return lambda b, s: ((l * SUB + b) * 6 + k, 0, 0)


def _prenorm_kernel(x_ref, w_ref, sh_ref, sc_ref, o_ref):
    y = _rms(x_ref[0], w_ref[...])
    o_ref[0] = (y * (1.0 + sc_ref[0]) + sh_ref[0]).astype(o_ref.dtype)


def _prenorm(x, w, mod, l, ctx_row, has_ctx):
    B, S, D = x.shape
    vec = lambda k: pl.BlockSpec((1, 1, D), _mod_map(l, k, ctx_row, has_ctx))
    return pl.pallas_call(
        _prenorm_kernel, grid=(B, S // TS),
        in_specs=[pl.BlockSpec((1, TS, D), lambda b, s: (b, s, 0)),
                  pl.BlockSpec((1, D), lambda b, s: (0, 0)), vec(0), vec(1)],
        out_specs=pl.BlockSpec((1, TS, D), lambda b, s: (b, s, 0)),
        out_shape=jax.ShapeDtypeStruct((B, S, D), BF16),
        compiler_params=_cp("parallel", "parallel"), name="prenorm",
    )(x, w.reshape(1, D), mod, mod)


def _resid_kernel(has_next, x_ref, y_ref, wpost_ref, g_ref, *rest):
    if has_next:
        wpre_ref, sh_ref, sc_ref, xo_ref, ho_ref = rest
    else:
        (xo_ref,) = rest
    xn = x_ref[0] + g_ref[0] * _rms(y_ref[0], wpost_ref[...])
    xo_ref[0] = xn
    if has_next:
        h = _rms(xn, wpre_ref[...])
        ho_ref[0] = (h * (1.0 + sc_ref[0]) + sh_ref[0]).astype(ho_ref.dtype)


def _resid(x, y, wpost, mod, l, kg, ctx_row, x_off, nxt):
    B, Sy, D = y.shape
    has_ctx = x_off == 0 and x.shape[1] == Sy and ctx_row is not None
    vec = lambda ll, k: pl.BlockSpec((1, 1, D), _mod_map(ll, k, ctx_row, has_ctx))
    row = pl.BlockSpec((1, D), lambda b, s: (0, 0))
    tile = pl.BlockSpec((1, TS, D), lambda b, s: (b, s, 0))
    in_specs = [pl.BlockSpec((1, TS, D), lambda b, s: (b, s + x_off, 0)), tile, row, vec(l, kg)]
    args = [x, y, wpost.reshape(1, D), mod]
    out_specs = [tile]
    out_shape = [jax.ShapeDtypeStruct((B, Sy, D), F32)]
    if nxt is not None:
        wpre, ln, ksh, ksc = nxt
        in_specs += [row, vec(ln, ksh), vec(ln, ksc)]
        args += [wpre.reshape(1, D), mod, mod]
        out_specs.append(tile)
        out_shape.append(jax.ShapeDtypeStruct((B, Sy, D), BF16))
    out = pl.pallas_call(
        functools.partial(_resid_kernel, nxt is not None), grid=(B, Sy // TS),
        in_specs=in_specs, out_specs=out_specs, out_shape=out_shape,
        compiler_params=_cp("parallel", "parallel"), name="resid",
    )(*args)
    return out if nxt is not None else (out[0], None)


def _mm_kernel(x_ref, w_ref, o_ref):
    o_ref[...] = _dot(x_ref[...], w_ref[...]).astype(o_ref.dtype)


def _matmul(x, w, out_dtype):
    M, K = x.shape
    N = w.shape[1]
    tm = _pick(M, (1024, 512, 256))
    tn = _pick(N, (512, 384, 256, 128))
    return pl.pallas_call(
        _mm_kernel, grid=(M // tm, N // tn),
        in_specs=[pl.BlockSpec((tm, K), lambda i, j: (i, 0)),
                  pl.BlockSpec((K, tn), lambda i, j: (0, j))],
        out_specs=pl.BlockSpec((tm, tn), lambda i, j: (i, j)),
        out_shape=jax.ShapeDtypeStruct((M, N), out_dtype),
        compiler_params=_cp("parallel", "parallel"), name="matmul",
    )(x, w)


def _mm2_kernel(x1_ref, x2_ref, w1_ref, w2_ref, o_ref):
    o_ref[...] = _dot(x1_ref[...], w1_ref[...]) + _dot(x2_ref[...], w2_ref[...])


def _matmul2(x1, x2, w):
    M, K1 = x1.shape
    K2 = x2.shape[1]
    assert K1 == K2 and w.shape[0] == K1 + K2
    N = w.shape[1]
    tm = _pick(M, (1024, 512, 256))
    tn = _pick(N, (512, 256, 128))
    return pl.pallas_call(
        _mm2_kernel, grid=(M // tm, N // tn),
        in_specs=[pl.BlockSpec((tm, K1), lambda i, j: (i, 0)),
                  pl.BlockSpec((tm, K2), lambda i, j: (i, 0)),
                  pl.BlockSpec((K1, tn), lambda i, j: (0, j)),
                  pl.BlockSpec((K2, tn), lambda i, j: (1, j))],
        out_specs=pl.BlockSpec((tm, tn), lambda i, j: (i, j)),
        out_shape=jax.ShapeDtypeStruct((M, N), F32),
        compiler_params=_cp("parallel", "parallel"), name="matmul2",
    )(x1, x2, w, w)


def _log_sigmoid(x):
    return jnp.minimum(x, 0.0) - jnp.log1p(jnp.exp(-jnp.abs(x)))


def _mlstm_kernel(q_ref, k_ref, v_ref, gc_ref, gr_ref, bc_ref, br_ref, o_ref, C_ref, n_ref, m_ref):
    H, dk, dv, L = MLSTM_HEADS, MLSTM_DK, MLSTM_DV, TS
    d = pl.program_id(1)
    t = pl.program_id(2)

    @pl.when(t == 0)
    def _():
        C_ref[...] = jnp.zeros_like(C_ref)
        n_ref[...] = jnp.zeros_like(n_ref)
        m_ref[...] = jnp.zeros_like(m_ref)

    fwd = d == 0
    ri = lax.broadcasted_iota(jnp.int32, (L, L), 0)
    ci = lax.broadcasted_iota(jnp.int32, (L, L), 1)
    diff = (ci - ri) * (1 - 2 * d)
    causal = diff <= 0
    causal_f = jnp.where(causal, 1.0, 0.0)
    causal_t = jnp.where(diff >= 0, 1.0, 0.0)
    gc = gc_ref[0] + bc_ref[...]
    gr = gr_ref[0] + br_ref[...]
    lfc = _log_sigmoid(gc)
    lfr = _log_sigmoid(gr)
    hi = lax.Precision.HIGHEST
    bcol_all = jnp.dot(causal_f, lfc, precision=hi, preferred_element_type=F32)
    brow_all = jnp.dot(lfr, causal_t, precision=hi, preferred_element_type=F32)
    tot_all = jnp.sum(lfr, axis=-1, keepdims=True)
    sel = lambda a, b: jnp.where(fwd, a, b)
    scale = dk ** -0.5
    for h in range(H):
        ic = sel(gc[:, h:h + 1], gc[:, 2 * H + h:2 * H + h + 1])
        ir = sel(gr[h:h + 1], gr[2 * H + h:2 * H + h + 1])
        bc = sel(bcol_all[:, H + h:H + h + 1], bcol_all[:, 3 * H + h:3 * H + h + 1])
        br = sel(brow_all[H + h:H + h + 1], brow_all[3 * H + h:3 * H + h + 1])
        tot = sel(tot_all[H + h:H + h + 1], tot_all[3 * H + h:3 * H + h + 1])
        m = m_ref[h]
        dmat = jnp.where(causal, bc - br + ir, NEG)
        inter = bc + m
        mj = jnp.maximum(inter, jnp.max(dmat, axis=-1, keepdims=True))
        p = jnp.exp(dmat - mj)
        q = q_ref[0, :, h * dk:(h + 1) * dk]
        k = k_ref[0, :, h * dk:(h + 1) * dk]
        v = v_ref[0, :, h * dv:(h + 1) * dv]
        s = _dot_nt(q, k) * scale * p
        w_inter = jnp.exp(inter - mj)
        num = w_inter * _dot(q, C_ref[h].astype(BF16)) + _dot(s.astype(BF16), v)
        qn = jnp.sum(q.astype(F32) * n_ref[h], axis=-1, keepdims=True)
        den = w_inter * qn + jnp.sum(s, axis=-1, keepdims=True)
        o_ref[0, 0, :, h * dv:(h + 1) * dv] = num / jnp.maximum(jnp.abs(den), jnp.exp(-mj))
        g_c = tot - bc + ic
        g_r = tot - br + ir
        m_new = jnp.maximum(tot + m, jnp.max(g_r, axis=-1, keepdims=True))
        decay = jnp.exp(tot + m - m_new)
        wk = jnp.exp(g_c - m_new) * (k.astype(F32) * scale)
        C_ref[h] = decay * C_ref[h] + _dot_tn(wk.astype(BF16), v)
        n_ref[h] = decay * n_ref[h] + jnp.sum(wk, axis=0, keepdims=True)
        m_ref[h] = m_new


def _mlstm(P, gcol, grow, gate_b):
    B, S, _ = P.shape
    nc = S // TS
    chunk = lambda d, t: jnp.where(d == 0, t, jnp.where(t == 0, 0, nc - t))
    blk = lambda c: pl.BlockSpec((1, TS, MLSTM_QK), lambda b, d, t: (b, chunk(d, t), c))
    G = MLSTM_GATES
    return pl.pallas_call(
        _mlstm_kernel, grid=(B, 2, nc),
        in_specs=[blk(0), blk(1), blk(2),
                  pl.BlockSpec((1, TS, G), lambda b, d, t: (b, chunk(d, t), 0)),
                  pl.BlockSpec((1, G, TS), lambda b, d, t: (b, 0, chunk(d, t))),
                  pl.BlockSpec((1, G), lambda b, d, t: (0, 0)),
                  pl.BlockSpec((G, 1), lambda b, d, t: (0, 0))],
        out_specs=pl.BlockSpec((1, 1, TS, MLSTM_V), lambda b, d, t: (d, b, chunk(d, t), 0)),
        out_shape=jax.ShapeDtypeStruct((2, B, S, MLSTM_V), F32),
        scratch_shapes=[pltpu.VMEM((MLSTM_HEADS, MLSTM_DK, MLSTM_DV), F32),
                        pltpu.VMEM((MLSTM_HEADS, 1, MLSTM_DK), F32),
                        pltpu.VMEM((MLSTM_HEADS, 1, 1), F32)],
        compiler_params=_cp("parallel", "arbitrary", "arbitrary"), name="mlstm",
    )(P, P, P, gcol, grow, gate_b.reshape(1, G), gate_b.reshape(G, 1))


def _mlstm_out_kernel(hf_ref, hr_ref, o_ref, gain_ref, out_ref):
    dv = MLSTM_DV
    for h in range(MLSTM_HEADS):
        sl = slice(h * dv, (h + 1) * dv)
        x = hf_ref[0, 0, :, sl] + hr_ref[0, 0, :, sl]
        out_ref[0, :, sl] = (_rms(x, gain_ref[:, sl]) * _sigmoid(o_ref[0, :, sl])).astype(out_ref.dtype)


def _mlstm_out(Hd, OG, gain):
    _, B, S, V = Hd.shape
    return pl.pallas_call(
        _mlstm_out_kernel, grid=(B, S // TS),
        in_specs=[pl.BlockSpec((1, 1, TS, V), lambda b, s: (0, b, s, 0)),
                  pl.BlockSpec((1, 1, TS, V), lambda b, s: (1, b, s, 0)),
                  pl.BlockSpec((1, TS, V), lambda b, s: (b, s, 0)),
                  pl.BlockSpec((1, V), lambda b, s: (0, 0))],
        out_specs=pl.BlockSpec((1, TS, V), lambda b, s: (b, s, 0)),
        out_shape=jax.ShapeDtypeStruct((B, S, V), BF16),
        compiler_params=_cp("parallel", "parallel"), name="mlstm_out",
    )(Hd, Hd, OG, gain.reshape(1, V))


def _na_bias(rpb):
    cls = np.arange(NA_KH)[:, None]
    kr = np.arange(NA_KH)[None, :]
    drow = kr - cls + (NA_KH - 1)
    w = np.arange(GRID_W)[:, None]
    cc = np.arange(GRID_W)[None, :]
    col_start = np.clip(w - NA_KW // 2, 0, GRID_W - NA_KW)
    valid = (cc >= col_start) & (cc < col_start + NA_KW)
    dcol = np.clip(cc - w + (NA_KW - 1), 0, 2 * NA_KW - 2)
    t = rpb[:, drow]
    t = t[:, :, :, dcol]
    t = jnp.where(valid[None, None, None], t.astype(F32), NEG)
    t = jnp.transpose(t, (1, 0, 3, 2, 4))
    return t.reshape(NA_KH, NA_HEADS, GRID_W, NA_KH * GRID_W)


def _na_kernel(q_ref, k_ref, v_ref, bias_ref, o_ref, *, Lc, rows_n):
    W, d = GRID_W, NA_DIM
    scale = d ** -0.5
    kc = k_ref[0, 0:Lc, :]
    vc = v_ref[0, 0:Lc, :]
    sc = _dot_nt(q_ref[0, 0:Lc, :], kc) * scale
    pc = jnp.exp(sc - jnp.max(sc, axis=-1, keepdims=True))
    oc = _dot(pc.astype(BF16), vc) / jnp.sum(pc, axis=-1, keepdims=True)
    o_ref[0, 0:Lc, :] = oc.astype(o_ref.dtype)

    def row(r, carry):
        r0 = jnp.clip(r - NA_KH // 2, 0, rows_n - NA_KH)
        qoff = pl.multiple_of(Lc + r * W, W)
        koff = pl.multiple_of(Lc + r0 * W, W)
        qr = q_ref[0, pl.ds(qoff, W), :]
        kw = k_ref[0, pl.ds(koff, NA_KH * W), :]
        vw = v_ref[0, pl.ds(koff, NA_KH * W), :]
        sw = _dot_nt(qr, kw) * scale + bias_ref[r - r0, 0]
        sx = _dot_nt(qr, kc) * scale
        m = jnp.maximum(jnp.max(sw, axis=-1, keepdims=True), jnp.max(sx, axis=-1, keepdims=True))
        pw = jnp.exp(sw - m)
        px = jnp.exp(sx - m)
        l = jnp.sum(pw, axis=-1, keepdims=True) + jnp.sum(px, axis=-1, keepdims=True)
        o = (_dot(pw.astype(BF16), vw) + _dot(px.astype(BF16), vc)) / l
        o_ref[0, pl.ds(qoff, W), :] = o.astype(o_ref.dtype)
        return carry

    lax.fori_loop(0, rows_n, row, 0)


def _na(P, bias, Lc, col0):
    B, S, _ = P.shape
    rows_n = (S - Lc) // GRID_W
    c0 = col0 // NA_DIM
    blk = lambda c: pl.BlockSpec((1, S, NA_DIM), lambda b, h: (b, 0, c0 + c * NA_HEADS + h))
    return pl.pallas_call(
        functools.partial(_na_kernel, Lc=Lc, rows_n=rows_n), grid=(B, NA_HEADS),
        in_specs=[blk(0), blk(1), blk(2),
                  pl.BlockSpec((NA_KH, 1, GRID_W, NA_KH * GRID_W), lambda b, h: (0, h, 0, 0))],
        out_specs=pl.BlockSpec((1, S, NA_DIM), lambda b, h: (b, 0, h)),
        out_shape=jax.ShapeDtypeStruct((B, S, NA_WIDTH), BF16),
        compiler_params=_cp("parallel", "parallel"), name="na",
    )(P, P, P, bias)


def _rope_tables(Lc, N):
    quarter = GQA_DIM // 4
    t = jnp.arange(N)
    inv = ROPE_THETA ** (-(jnp.arange(quarter, dtype=F32) / quarter))
    ang_r = (t // GRID_W).astype(F32)[:, None] * inv[None, :]
    ang_c = (t % GRID_W).astype(F32)[:, None] * inv[None, :]
    cos = jnp.concatenate([jnp.cos(ang_r)] * 2 + [jnp.cos(ang_c)] * 2, axis=-1)
    sin = jnp.concatenate([-jnp.sin(ang_r), jnp.sin(ang_r), -jnp.sin(ang_c), jnp.sin(ang_c)], axis=-1)
    cos = jnp.concatenate([jnp.ones((Lc, GQA_DIM), F32), cos], axis=0)
    sin = jnp.concatenate([jnp.zeros((Lc, GQA_DIM), F32), sin], axis=0)
    return jnp.tile(cos, (1, 4)), jnp.tile(sin, (1, 4))


def _rope_kernel(x_ref, cos_ref, sin_ref, o_ref):
    x = x_ref[0]
    wd = x.shape[-1]
    quarter = GQA_DIM // 4
    lane = lax.broadcasted_iota(jnp.int32, x.shape, 1)
    first = (lane % (2 * quarter)) < quarter
    rot = jnp.where(first, pltpu.roll(x, wd - quarter, 1), pltpu.roll(x, quarter, 1))
    o_ref[0] = (x * cos_ref[...] + rot * sin_ref[...]).astype(o_ref.dtype)


def _rope(P1, cos, sin):
    B, S, _ = P1.shape
    wd = GQA_KV
    nblk = (GQA_Q + GQA_KV) // wd
    return pl.pallas_call(
        _rope_kernel, grid=(B, S // TS, nblk),
        in_specs=[pl.BlockSpec((1, TS, wd), lambda b, s, c: (b, s, c)),
                  pl.BlockSpec((TS, wd), lambda b, s, c: (s, 0)),
                  pl.BlockSpec((TS, wd), lambda b, s, c: (s, 0))],
        out_specs=pl.BlockSpec((1, TS, wd), lambda b, s, c: (b, s, c)),
        out_shape=jax.ShapeDtypeStruct((B, S, GQA_Q + GQA_KV), BF16),
        compiler_params=_cp("parallel", "parallel", "parallel"), name="rope",
    )(P1, cos, sin)


def _gqa_kernel(sink_ref, q_ref, k_ref, v_ref, o_ref, kd_ref, vd_ref, *, Lc, N):
    W, dh, KV = GQA_WINDOW, GQA_DIM, GQA_KV_HEADS
    G = GQA_HEADS // KV
    i = pl.program_id(1)
    scale = dh ** -0.5

    @pl.when(i == 0)
    def _():
        zero = jnp.zeros((W, 2 * dh), BF16)
        for kh in range(KV):
            kk = k_ref[0, :, kh * dh:(kh + 1) * dh]
            vv = v_ref[0, :, kh * dh:(kh + 1) * dh].astype(BF16)
            k2 = jnp.concatenate([kk, kk], axis=-1)
            v2 = jnp.concatenate([vv, vv], axis=-1)
            for ref, a in ((kd_ref, k2), (vd_ref, v2)):
                ref[kh, 0:Lc, :] = a[0:Lc]
                ref[kh, Lc:Lc + W, :] = zero
                ref[kh, Lc + W:Lc + W + N, :] = a[Lc:]
                ref[kh, Lc + W + N:Lc + 2 * W + N, :] = zero

    span = 3 * W
    off = pl.multiple_of(Lc + i * W, W)
    ii = lax.broadcasted_iota(jnp.int32, (W, span), 0)
    jj = lax.broadcasted_iota(jnp.int32, (W, span), 1)
    kpos = i * W - W + jj
    valid = (kpos >= 0) & (kpos < N) & (jnp.abs(ii - (jj - W)) <= W)
    lane = lax.broadcasted_iota(jnp.int32, (W, 2 * dh), 1)
    low = lane < dh
    for kh in range(KV):
        kw = kd_ref[kh, pl.ds(off, span), :]
        vw = vd_ref[kh, pl.ds(off, span), :]
        kc = kd_ref[kh, 0:Lc, :]
        vc = vd_ref[kh, 0:Lc, :]
        for slab in range(G // 2):
            j = kh * (G // 2) + slab
            q2 = q_ref[0, :, j * 2 * dh:(j + 1) * 2 * dh]
            outs = []
            for e in range(2):
                qm = jnp.where(low if e == 0 else jnp.logical_not(low), q2, jnp.zeros_like(q2))
                sw = jnp.where(valid, _dot_nt(qm, kw) * scale, NEG)
                sx = _dot_nt(qm, kc) * scale
                sk = sink_ref[2 * j + e]
                m = jnp.maximum(jnp.maximum(jnp.max(sw, axis=-1, keepdims=True),
                                            jnp.max(sx, axis=-1, keepdims=True)), sk)
                pw = jnp.exp(sw - m)
                px = jnp.exp(sx - m)
                l = (jnp.sum(pw, axis=-1, keepdims=True) + jnp.sum(px, axis=-1, keepdims=True)
                     + jnp.exp(sk - m))
                outs.append((_dot(pw.astype(BF16), vw) + _dot(px.astype(BF16), vc)) / l)
            o_ref[0, :, j * 2 * dh:(j + 1) * 2 * dh] = jnp.where(low, outs[0], outs[1]).astype(o_ref.dtype)


def _gqa(QK, P1, sink, Lc):
    B, S, _ = QK.shape
    N = S - Lc
    W = GQA_WINDOW
    rows = Lc + 2 * W + N
    return pl.pallas_call(
        functools.partial(_gqa_kernel, Lc=Lc, N=N), grid=(B, N // W),
        in_specs=[pl.BlockSpec(memory_space=pltpu.SMEM),
                  pl.BlockSpec((1, W, GQA_Q), lambda b, i: (b, i + Lc // W, 0)),
                  pl.BlockSpec((1, S, GQA_KV), lambda b, i: (b, 0, GQA_Q // GQA_KV)),
                  pl.BlockSpec((1, S, GQA_KV), lambda b, i: (b, 0, (GQA_Q + GQA_KV) // GQA_KV))],
        out_specs=pl.BlockSpec((1, W, GQA_Q), lambda b, i: (b, i, 0)),
        out_shape=jax.ShapeDtypeStruct((B, N, GQA_Q), BF16),
        scratch_shapes=[pltpu.VMEM((GQA_KV_HEADS, rows, 2 * GQA_DIM), BF16),
                        pltpu.VMEM((GQA_KV_HEADS, rows, 2 * GQA_DIM), BF16)],
        compiler_params=_cp("parallel", "arbitrary"), name="gqa",
    )(sink, QK, QK, P1)


def _gelu_tanh(x):
    return 0.5 * x * (1.0 + jnp.tanh(np.sqrt(2.0 / np.pi) * (x + 0.044715 * (x * x * x))))


def _lru_kernel(x_ref, g_ref, cw_ref, cb_ref, gw_ref, gb_ref, lam_ref, o_ref,
                u_ref, a_ref, b_ref, hs_ref, *, Lc, N):
    S = Lc + N
    CH = 128
    nt = S // SUB
    ntc = Lc // SUB
    x = x_ref[0]
    row = lax.broadcasted_iota(jnp.int32, x.shape, 0)
    seg_first = (row == 0) | (row == Lc)
    seg_last = (row == Lc - 1) | (row == S - 1)
    seg_last2 = seg_last | (row == Lc - 2) | (row == S - 2)
    xm1 = jnp.where(seg_first, 0.0, pltpu.roll(x, 1, 0))
    xp1 = jnp.where(seg_last, 0.0, pltpu.roll(x, S - 1, 0))
    xp2 = jnp.where(seg_last2, 0.0, pltpu.roll(x, S - 2, 0))
    u_ref[...] = xm1 * cw_ref[0:1, :] + cb_ref[...] + x * cw_ref[1:2, :] + xp1 * cw_ref[2:3, :] + xp2 * cw_ref[3:4, :]

    crow = lax.broadcasted_iota(jnp.int32, (CH, LANE), 0) % SUB
    for d in range(2):
        lam = lam_ref[d:d + 1, :]
        sp = jnp.maximum(-lam, 0.0) + jnp.log1p(jnp.exp(-jnp.abs(lam)))
        wr = gw_ref[d, 0, 0].astype(BF16)
        wi = gw_ref[d, 1, 0].astype(BF16)
        br = gb_ref[d, 0:1, :]
        bi = gb_ref[d, 1:2, :]

        def gates(c, carry):
            off = pl.multiple_of(c * CH, CH)
            u = u_ref[pl.ds(off, CH), :]
            ub = u.astype(BF16)
            r = _sigmoid(_dot(ub, wr) + br)
            ig = _sigmoid(_dot(ub, wi) + bi)
            a = jnp.exp(-LRU_C * r * sp)
            b = jnp.sqrt(1.0 - a * a) * (ig * u)
            for s in (1, 2, 4):
                if d == 0:
                    keep = crow >= s
                    a_sh = pltpu.roll(a, s, 0)
                    b_sh = pltpu.roll(b, s, 0)
                else:
                    keep = crow < SUB - s
                    a_sh = pltpu.roll(a, CH - s, 0)
                    b_sh = pltpu.roll(b, CH - s, 0)
                b = jnp.where(keep, a * b_sh + b, b)
                a = jnp.where(keep, a * a_sh, a)
            a_ref[pl.ds(off, CH), :] = a
            b_ref[pl.ds(off, CH), :] = b
            return carry

        lax.fori_loop(0, S // CH, gates, 0)

        if d == 0:
            def fstep(i, h):
                off = pl.multiple_of(i * SUB, SUB)
                ht = a_ref[pl.ds(off, SUB), :] * h + b_ref[pl.ds(off, SUB), :]
                hs_ref[pl.ds(off, SUB), :] = ht
                return jnp.broadcast_to(ht[SUB - 1:SUB, :], (SUB, LANE))
            lax.fori_loop(0, nt, fstep, jnp.zeros((SUB, LANE), F32), unroll=8)
        else:
            def rstep(lo, hi_):
                def step(i, h):
                    off = pl.multiple_of((hi_ - 1 - i) * SUB, SUB)
                    ht = a_ref[pl.ds(off, SUB), :] * h + b_ref[pl.ds(off, SUB), :]
                    hs_ref[pl.ds(off, SUB), :] += ht
                    return jnp.broadcast_to(ht[0:1, :], (SUB, LANE))
                return step
            h = lax.fori_loop(0, ntc, rstep(0, ntc), jnp.zeros((SUB, LANE), F32), unroll=8)
            lax.fori_loop(0, nt - ntc, rstep(ntc, nt), h, unroll=8)

    def outp(c, carry):
        off = pl.multiple_of(c * CH, CH)
        g = g_ref[0, pl.ds(Lc + off, CH), :]
        o_ref[0, pl.ds(off, CH), :] = (hs_ref[pl.ds(Lc + off, CH), :] * _gelu_tanh(g)).astype(o_ref.dtype)
        return carry

    lax.fori_loop(0, N // CH, outp, 0)


def _lru(P1, col_x, col_g, conv_w, conv_b, gate_w, gate_b, lam, Lc):
    B, S, _ = P1.shape
    N = S - Lc
    bs = LRU_BS
    return pl.pallas_call(
        functools.partial(_lru_kernel, Lc=Lc, N=N), grid=(B, LRU_BLOCKS),
        in_specs=[pl.BlockSpec((1, S, bs), lambda b, j: (b, 0, col_x // bs + j)),
                  pl.BlockSpec((1, S, bs), lambda b, j: (b, 0, col_g // bs + j)),
                  pl.BlockSpec((4, bs), lambda b, j: (0, j)),
                  pl.BlockSpec((1, bs), lambda b, j: (0, j)),
                  pl.BlockSpec((2, 2, 1, bs, bs), lambda b, j: (0, 0, j, 0, 0)),
                  pl.BlockSpec((2, 2, bs), lambda b, j: (0, 0, j)),
                  pl.BlockSpec((2, bs), lambda b, j: (0, j))],
        out_specs=pl.BlockSpec((1, N, bs), lambda b, j: (b, 0, j)),
        out_shape=jax.ShapeDtypeStruct((B, N, LRU_WIDTH), BF16),
        scratch_shapes=[pltpu.VMEM((S, bs), F32) for _ in range(4)],
        compiler_params=_cp("parallel", "parallel"), name="lru",
    )(P1, P1, conv_w, conv_b.reshape(1, LRU_WIDTH), gate_w, gate_b, lam)


def _ffn_kernel(x_ref, w1_ref, w3_ref, w2_ref, g_ref, o_ref):
    f = pl.program_id(2)
    x = x_ref[0]
    u = _dot(x, w1_ref[0].astype(BF16))
    v = _dot(x, w3_ref[0].astype(BF16))
    a = (u * _sigmoid(u) * v).astype(BF16)
    y = _dot(a, w2_ref[0].astype(BF16))

    @pl.when(f == 0)
    def _():
        o_ref[0] = y

    @pl.when(f > 0)
    def _():
        o_ref[0] += y

    @pl.when(f == pl.num_programs(2) - 1)
    def _():
        o_ref[0] = o_ref[0] * g_ref[0]


def _ffn(xs, w1, w3, w2, gs):
    E, R, D = xs.shape
    FF = w1.shape[-1]
    rm = R // 2
    tf = _pick(FF, (256, 128))
    return pl.pallas_call(
        _ffn_kernel, grid=(E, 2, FF // tf),
        in_specs=[pl.BlockSpec((1, rm, D), lambda e, m, f: (e, m, 0)),
                  pl.BlockSpec((1, D, tf), lambda e, m, f: (e, 0, f)),
                  pl.BlockSpec((1, D, tf), lambda e, m, f: (e, 0, f)),
                  pl.BlockSpec((1, tf, D), lambda e, m, f: (e, f, 0)),
                  pl.BlockSpec((1, rm, 1), lambda e, m, f: (e, m, 0))],
        out_specs=pl.BlockSpec((1, rm, D), lambda e, m, f: (e, m, 0)),
        out_shape=jax.ShapeDtypeStruct((E, R, D), F32),
        compiler_params=_cp("parallel", "parallel", "arbitrary"), name="ffn",
    )(xs, w1, w3, w2, gs)


def _moe(h, router_bf, w1, w3, w2, Lc):
    B, S, D = h.shape
    E = N_EXPERTS
    hf = h.reshape(B * S, D)
    logits = _matmul(hf, router_bf, F32)[:, :E].reshape(B, S, E)
    aff = jax.nn.softmax(logits, axis=-1)
    ids, gs = [], []
    for lo, n in ((Lc, S - Lc), (0, Lc)):
        if n == 0:
            continue
        cap = EC_FACTOR * n // E
        g, idx = lax.top_k(jnp.swapaxes(aff[:, lo:lo + n], 1, 2), cap)
        tok = idx + lo + (jnp.arange(B, dtype=idx.dtype) * S)[:, None, None]
        ids.append(jnp.swapaxes(tok, 0, 1).reshape(E, B * cap))
        gs.append(jnp.swapaxes(g, 0, 1).reshape(E, B * cap))
    ids = jnp.concatenate(ids, axis=1)
    gs = jnp.concatenate(gs, axis=1)
    xs = jnp.take(hf, ids.reshape(-1), axis=0).reshape(E, -1, D)
    y = _ffn(xs, w1, w3, w2, gs[..., None])
    out = jnp.zeros((B * S, D), F32).at[ids.reshape(-1)].add(y.reshape(-1, D))
    return out.reshape(B, S, D)


def kernel(x, c, ctx, c_ctx, ada_w, ada_b, norm_mix_pre, norm_mix_post, norm_ffn_pre, norm_ffn_post,
           ab_w_in, ab_w_out, mlstm_gate_b, mlstm_norm, na_rpb,
           cd_w_in, cd_w_out, gqa_sink, lru_conv_w, lru_conv_b, lru_gate_w, lru_gate_b, lru_lambda,
           moe_router, moe_w1, moe_w3, moe_w2):
    B, N, D = x.shape
    Lc = ctx.shape[1]
    S = Lc + N
    assert Lc == TS and N % TS == 0 and B < SUB
    ctx_row = B
    xs = jnp.concatenate([ctx, x], axis=1)
    cond = jnp.concatenate([c, c_ctx[None], jnp.zeros((SUB - B - 1, D), F32)], axis=0)
    mod = _ada(cond, ada_w, ada_b).reshape(-1, 1, D)
    router = jnp.pad(moe_router, ((0, 0), (0, 0), (0, LANE - N_EXPERTS))).astype(BF16)

    h = _prenorm(xs, norm_mix_pre[0], mod, 0, ctx_row, True)
    hf = h.reshape(B * S, D)
    w_in = ab_w_in[0]
    o0 = 3 * MLSTM_QK
    n0 = o0 + MLSTM_V + MLSTM_GATES
    w_qkv = jnp.concatenate([w_in[:, :o0], w_in[:, n0:]], axis=1).astype(BF16)
    w_og = jnp.pad(w_in[:, o0:n0], ((0, 0), (0, LANE - MLSTM_GATES))).astype(BF16)
    P = _matmul(hf, w_qkv, BF16).reshape(B, S, -1)
    OG = _matmul(hf, w_og, F32).reshape(B, S, -1)
    gcol = OG[:, :, MLSTM_V:MLSTM_V + MLSTM_GATES]
    Hd = _mlstm(P, gcol, jnp.swapaxes(gcol, 1, 2), mlstm_gate_b[0])
    ML = _mlstm_out(Hd, OG, mlstm_norm[0])
    NL = _na(P, _na_bias(na_rpb[0]), Lc, o0)
    Y = _matmul2(ML.reshape(B * S, -1), NL.reshape(B * S, -1), ab_w_out[0].astype(BF16)).reshape(B, S, D)
    xs, h = _resid(xs, Y, norm_mix_post[0], mod, 0, 2, ctx_row, 0, (norm_ffn_pre[0], 0, 3, 4))
    Y = _moe(h, router[0], moe_w1[0], moe_w3[0], moe_w2[0], Lc)
    xs, h = _resid(xs, Y, norm_ffn_post[0], mod, 0, 5, ctx_row, 0, (norm_mix_pre[1], 1, 0, 1))

    P1 = _matmul(h.reshape(B * S, D), cd_w_in[0].astype(BF16), F32).reshape(B, S, -1)
    cos, sin = _rope_tables(Lc, N)
    QK = _rope(P1, cos, sin)
    AL = _gqa(QK, P1, gqa_sink[0], Lc)
    RL = _lru(P1, GQA_Q + 2 * GQA_KV, GQA_Q + 2 * GQA_KV + LRU_WIDTH,
              lru_conv_w[0], lru_conv_b[0], lru_gate_w[0], lru_gate_b[0], lru_lambda[0], Lc)
    Y = _matmul2(AL.reshape(B * N, -1), RL.reshape(B * N, -1), cd_w_out[0].astype(BF16)).reshape(B, N, D)
    xl, h = _resid(xs, Y, norm_mix_post[1], mod, 1, 2, None, Lc // TS, (norm_ffn_pre[1], 1, 3, 4))
    Y = _moe(h, router[1], moe_w1[1], moe_w3[1], moe_w2[1], 0)
    xl, _ = _resid(xl, Y, norm_ffn_post[1], mod, 1, 5, None, 0, None)
    return xl
```

```python
import functools

import numpy as np
import jax
import jax.numpy as jnp
from jax import lax
from jax.experimental import pallas as pl
from jax.experimental.pallas import tpu as pltpu

F32 = jnp.float32
BF16 = jnp.bfloat16
EPS = 1e-6
NEG = -1e30

GRID_W = 64
MLSTM_HEADS = 4
MLSTM_DK = 256
MLSTM_DV = 256
MLSTM_QK = MLSTM_HEADS * MLSTM_DK
MLSTM_V = MLSTM_HEADS * MLSTM_DV
MLSTM_GATES = 2 * 2 * MLSTM_HEADS
NA_HEADS = 8
NA_DIM = 128
NA_WIDTH = NA_HEADS * NA_DIM
NA_KH = 8
NA_KW = 16
GQA_HEADS = 16
GQA_KV_HEADS = 4
GQA_DIM = 64
GQA_Q = GQA_HEADS * GQA_DIM
GQA_KV = GQA_KV_HEADS * GQA_DIM
GQA_WINDOW = 128
ROPE_THETA = 10000.0
LRU_WIDTH = 1024
LRU_BLOCKS = 8
LRU_BS = LRU_WIDTH // LRU_BLOCKS
LRU_C = 8.0
N_EXPERTS = 16
EC_FACTOR = 2

TS = 256
LANE = 128
SUB = 8
VMEM_LIMIT = 56 * 1024 * 1024


def _cp(*sem):
    return pltpu.CompilerParams(dimension_semantics=sem, vmem_limit_bytes=VMEM_LIMIT)


def _pick(n, prefs):
    for p in prefs:
        if n % p == 0:
            return p
    return n


def _sigmoid(x):
    return 1.0 / (1.0 + jnp.exp(-x))


def _dot(a, b):
    return jnp.dot(a, b, preferred_element_type=F32)


def _dot_nt(a, b):
    return lax.dot_general(a, b, (((1,), (1,)), ((), ())), preferred_element_type=F32)


def _dot_tn(a, b):
    return lax.dot_general(a, b, (((0,), (0,)), ((), ())), preferred_element_type=F32)


def _rms(x, w):
    return x * lax.rsqrt(jnp.mean(x * x, axis=-1, keepdims=True) + EPS) * w


def _ada_kernel(c_ref, w_ref, b_ref, o_ref):
    c = c_ref[...]
    a = (c * _sigmoid(c)).astype(BF16)
    o_ref[0] = _dot(a, w_ref[0].astype(BF16)) + b_ref[0]


def _ada(cond, ada_w, ada_b):
    L, D, D6 = ada_w.shape
    tn = _pick(D6, (1024, 512, 256, 128))
    return pl.pallas_call(
        _ada_kernel, grid=(L, D6 // tn),
        in_specs=[pl.BlockSpec((SUB, D), lambda l, j: (0, 0)),
                  pl.BlockSpec((1, D, tn), lambda l, j: (l, 0, j)),
                  pl.BlockSpec((1, 1, tn), lambda l, j: (l, 0, j))],
        out_specs=pl.BlockSpec((1, SUB, tn), lambda l, j: (l, 0, j)),
        out_shape=jax.ShapeDtypeStruct((L, SUB, D6), F32),
        compiler_params=_cp("parallel", "parallel"), name="ada",
    )(cond, ada_w, ada_b.reshape(L, 1, D6))


def _mod_map(l, k, ctx_row, has_ctx):
    if has_ctx:
        return lambda b, s: ((l * SUB + jnp.where(s == 0, ctx_row, b)) * 6 + k, 0, 0)
    return lambda b, s: ((l * SUB + b) * 6 + k, 0, 0)


def _prenorm_kernel(x_ref, w_ref, sh_ref, sc_ref, o_ref):
    y = _rms(x_ref[0], w_ref[...])
    o_ref[0] = (y * (1.0 + sc_ref[0]) + sh_ref[0]).astype(o_ref.dtype)


def _prenorm(x, w, mod, l, ctx_row, has_ctx):
    B, S, D = x.shape
    vec = lambda k: pl.BlockSpec((1, 1, D), _mod_map(l, k, ctx_row, has_ctx))
    return pl.pallas_call(
        _prenorm_kernel, grid=(B, S // TS),
        in_specs=[pl.BlockSpec((1, TS, D), lambda b, s: (b, s, 0)),
                  pl.BlockSpec((1, D), lambda b, s: (0, 0)), vec(0), vec(1)],
        out_specs=pl.BlockSpec((1, TS, D), lambda b, s: (b, s, 0)),
        out_shape=jax.ShapeDtypeStruct((B, S, D), BF16),
        compiler_params=_cp("parallel", "parallel"), name="prenorm",
    )(x, w.reshape(1, D), mod, mod)


def _resid_kernel(has_next, x_ref, y_ref, wpost_ref, g_ref, *rest):
    if has_next:
        wpre_ref, sh_ref, sc_ref, xo_ref, ho_ref = rest
    else:
        (xo_ref,) = rest
    xn = x_ref[0] + g_ref[0] * _rms(y_ref[0], wpost_ref[...])
    xo_ref[0] = xn
    if has_next:
        h = _rms(xn, wpre_ref[...])
        ho_ref[0] = (h * (1.0 + sc_ref[0]) + sh_ref[0]).astype(ho_ref.dtype)


def _resid(x, y, wpost, mod, l, kg, ctx_row, x_off, nxt):
    B, Sy, D = y.shape
    has_ctx = x_off == 0 and x.shape[1] == Sy and ctx_row is not None
    vec = lambda ll, k: pl.BlockSpec((1, 1, D), _mod_map(ll, k, ctx_row, has_ctx))
    row = pl.BlockSpec((1, D), lambda b, s: (0, 0))
    tile = pl.BlockSpec((1, TS, D), lambda b, s: (b, s, 0))
    in_specs = [pl.BlockSpec((1, TS, D), lambda b, s: (b, s + x_off, 0)), tile, row, vec(l, kg)]
    args = [x, y, wpost.reshape(1, D), mod]
    out_specs = [tile]
    out_shape = [jax.ShapeDtypeStruct((B, Sy, D), F32)]
    if nxt is not None:
        wpre, ln, ksh, ksc = nxt
        in_specs += [row, vec(ln, ksh), vec(ln, ksc)]
        args += [wpre.reshape(1, D), mod, mod]
        out_specs.append(tile)
        out_shape.append(jax.ShapeDtypeStruct((B, Sy, D), BF16))
    out = pl.pallas_call(
        functools.partial(_resid_kernel, nxt is not None), grid=(B, Sy // TS),
        in_specs=in_specs, out_specs=out_specs, out_shape=out_shape,
        compiler_params=_cp("parallel", "parallel"), name="resid",
    )(*args)
    return out if nxt is not None else (out[0], None)


def _mm_kernel(x_ref, w_ref, o_ref):
    o_ref[...] = _dot(x_ref[...], w_ref[...]).astype(o_ref.dtype)


def _matmul(x, w, out_dtype):
    M, K = x.shape
    N = w.shape[1]
    tm = _pick(M, (1024, 512, 256))
    tn = _pick(N, (512, 384, 256, 128))
    return pl.pallas_call(
        _mm_kernel, grid=(M // tm, N // tn),
        in_specs=[pl.BlockSpec((tm, K), lambda i, j: (i, 0)),
                  pl.BlockSpec((K, tn), lambda i, j: (0, j))],
        out_specs=pl.BlockSpec((tm, tn), lambda i, j: (i, j)),
        out_shape=jax.ShapeDtypeStruct((M, N), out_dtype),
        compiler_params=_cp("parallel", "parallel"), name="matmul",
    )(x, w)


def _mm2_kernel(x1_ref, x2_ref, w1_ref, w2_ref, o_ref):
    o_ref[...] = _dot(x1_ref[...], w1_ref[...]) + _dot(x2_ref[...], w2_ref[...])


def _matmul2(x1, x2, w):
    M, K1 = x1.shape
    K2 = x2.shape[1]
    assert K1 == K2 and w.shape[0] == K1 + K2
    N = w.shape[1]
    tm = _pick(M, (1024, 512, 256))
    tn = _pick(N, (512, 256, 128))
    return pl.pallas_call(
        _mm2_kernel, grid=(M // tm, N // tn),
        in_specs=[pl.BlockSpec((tm, K1), lambda i, j: (i, 0)),
                  pl.BlockSpec((tm, K2), lambda i, j: (i, 0)),
                  pl.BlockSpec((K1, tn), lambda i, j: (0, j)),
                  pl.BlockSpec((K2, tn), lambda i, j: (1, j))],
        out_specs=pl.BlockSpec((tm, tn), lambda i, j: (i, j)),
        out_shape=jax.ShapeDtypeStruct((M, N), F32),
        compiler_params=_cp("parallel", "parallel"), name="matmul2",
    )(x1, x2, w, w)


def _log_sigmoid(x):
    return jnp.minimum(x, 0.0) - jnp.log1p(jnp.exp(-jnp.abs(x)))


def _mlstm_kernel(q_ref, k_ref, v_ref, gc_ref, gr_ref, bc_ref, br_ref, o_ref, C_ref, n_ref, m_ref):
    H, dk, dv, L = MLSTM_HEADS, MLSTM_DK, MLSTM_DV, TS
    d = pl.program_id(1)
    t = pl.program_id(2)

    @pl.when(t == 0)
    def _():
        C_ref[...] = jnp.zeros_like(C_ref)
        n_ref[...] = jnp.zeros_like(n_ref)
        m_ref[...] = jnp.zeros_like(m_ref)

    fwd = d == 0
    ri = lax.broadcasted_iota(jnp.int32, (L, L), 0)
    ci = lax.broadcasted_iota(jnp.int32, (L, L), 1)
    diff = (ci - ri) * (1 - 2 * d)
    causal = diff <= 0
    causal_f = jnp.where(causal, 1.0, 0.0)
    causal_t = jnp.where(diff >= 0, 1.0, 0.0)
    gc = gc_ref[0] + bc_ref[...]
    gr = gr_ref[0] + br_ref[...]
    lfc = _log_sigmoid(gc)
    lfr = _log_sigmoid(gr)
    hi = lax.Precision.HIGHEST
    bcol_all = jnp.dot(causal_f, lfc, precision=hi, preferred_element_type=F32)
    brow_all = jnp.dot(lfr, causal_t, precision=hi, preferred_element_type=F32)
    tot_all = jnp.sum(lfr, axis=-1, keepdims=True)
    sel = lambda a, b: jnp.where(fwd, a, b)
    scale = dk ** -0.5
    for h in range(H):
        ic = sel(gc[:, h:h + 1], gc[:, 2 * H + h:2 * H + h + 1])
        ir = sel(gr[h:h + 1], gr[2 * H + h:2 * H + h + 1])
        bc = sel(bcol_all[:, H + h:H + h + 1], bcol_all[:, 3 * H + h:3 * H + h + 1])
        br = sel(brow_all[H + h:H + h + 1], brow_all[3 * H + h:3 * H + h + 1])
        tot = sel(tot_all[H + h:H + h + 1], tot_all[3 * H + h:3 * H + h + 1])
        m = m_ref[h]
        dmat = jnp.where(causal, bc - br + ir, NEG)
        inter = bc + m
        mj = jnp.maximum(inter, jnp.max(dmat, axis=-1, keepdims=True))
        p = jnp.exp(dmat - mj)
        q = q_ref[0, :, h * dk:(h + 1) * dk]
        k = k_ref[0, :, h * dk:(h + 1) * dk]
        v = v_ref[0, :, h * dv:(h + 1) * dv]
        s = _dot_nt(q, k) * scale * p
        w_inter = jnp.exp(inter - mj)
        num = w_inter * _dot(q, C_ref[h].astype(BF16)) + _dot(s.astype(BF16), v)
        qn = jnp.sum(q.astype(F32) * n_ref[h], axis=-1, keepdims=True)
        den = w_inter * qn + jnp.sum(s, axis=-1, keepdims=True)
        o_ref[0, 0, :, h * dv:(h + 1) * dv] = num / jnp.maximum(jnp.abs(den), jnp.exp(-mj))
        g_c = tot - bc + ic
        g_r = tot - br + ir
        m_new = jnp.maximum(tot + m, jnp.max(g_r, axis=-1, keepdims=True))
        decay = jnp.exp(tot + m - m_new)
        wk = jnp.exp(g_c - m_new) * (k.astype(F32) * scale)
        C_ref[h] = decay * C_ref[h] + _dot_tn(wk.astype(BF16), v)
        n_ref[h] = decay * n_ref[h] + jnp.sum(wk, axis=0, keepdims=True)
        m_ref[h] = m_new


def _mlstm(P, gcol, grow, gate_b):
    B, S, _ = P.shape
    nc = S // TS
    chunk = lambda d, t: jnp.where(d == 0, t, jnp.where(t == 0, 0, nc - t))
    blk = lambda c: pl.BlockSpec((1, TS, MLSTM_QK), lambda b, d, t: (b, chunk(d, t), c))
    G = MLSTM_GATES
    return pl.pallas_call(
        _mlstm_kernel, grid=(B, 2, nc),
        in_specs=[blk(0), blk(1), blk(2),
                  pl.BlockSpec((1, TS, G), lambda b, d, t: (b, chunk(d, t), 0)),
                  pl.BlockSpec((1, G, TS), lambda b, d, t: (b, 0, chunk(d, t))),
                  pl.BlockSpec((1, G), lambda b, d, t: (0, 0)),
                  pl.BlockSpec((G, 1), lambda b, d, t: (0, 0))],
        out_specs=pl.BlockSpec((1, 1, TS, MLSTM_V), lambda b, d, t: (d, b, chunk(d, t), 0)),
        out_shape=jax.ShapeDtypeStruct((2, B, S, MLSTM_V), F32),
        scratch_shapes=[pltpu.VMEM((MLSTM_HEADS, MLSTM_DK, MLSTM_DV), F32),
                        pltpu.VMEM((MLSTM_HEADS, 1, MLSTM_DK), F32),
                        pltpu.VMEM((MLSTM_HEADS, 1, 1), F32)],
        compiler_params=_cp("parallel", "arbitrary", "arbitrary"), name="mlstm",
    )(P, P, P, gcol, grow, gate_b.reshape(1, G), gate_b.reshape(G, 1))


def _mlstm_out_kernel(hf_ref, hr_ref, o_ref, gain_ref, out_ref):
    dv = MLSTM_DV
    for h in range(MLSTM_HEADS):
        sl = slice(h * dv, (h + 1) * dv)
        x = hf_ref[0, 0, :, sl] + hr_ref[0, 0, :, sl]
        out_ref[0, :, sl] = (_rms(x, gain_ref[:, sl]) * _sigmoid(o_ref[0, :, sl])).astype(out_ref.dtype)


def _mlstm_out(Hd, OG, gain):
    _, B, S, V = Hd.shape
    return pl.pallas_call(
        _mlstm_out_kernel, grid=(B, S // TS),
        in_specs=[pl.BlockSpec((1, 1, TS, V), lambda b, s: (0, b, s, 0)),
                  pl.BlockSpec((1, 1, TS, V), lambda b, s: (1, b, s, 0)),
                  pl.BlockSpec((1, TS, V), lambda b, s: (b, s, 0)),
                  pl.BlockSpec((1, V), lambda b, s: (0, 0))],
        out_specs=pl.BlockSpec((1, TS, V), lambda b, s: (b, s, 0)),
        out_shape=jax.ShapeDtypeStruct((B, S, V), BF16),
        compiler_params=_cp("parallel", "parallel"), name="mlstm_out",
    )(Hd, Hd, OG, gain.reshape(1, V))


NA_QR = 4
NA_KR = NA_KH + 2 * NA_QR - 4


def _na_bias(rpb):
    a = np.arange(NA_QR)
    kr = np.arange(NA_KR)
    qrel = NA_QR * np.arange(3)[:, None] + a[None, :]
    r0rel = np.stack([0 * a, a, NA_KR - NA_KH + 0 * a])
    valid_r = (kr[None, None] >= r0rel[..., None]) & (kr[None, None] < r0rel[..., None] + NA_KH)
    drow = np.clip(kr[None, None] - qrel[..., None] + (NA_KH - 1), 0, 2 * NA_KH - 2)
    w = np.arange(GRID_W)[:, None]
    cc = np.arange(GRID_W)[None, :]
    col_start = np.clip(w - NA_KW // 2, 0, GRID_W - NA_KW)
    valid_c = (cc >= col_start) & (cc < col_start + NA_KW)
    dcol = np.clip(cc - w + (NA_KW - 1), 0, 2 * NA_KW - 2)
    t = rpb[:, drow]
    t = t[..., dcol]
    valid = valid_r[None, :, :, :, None, None] & valid_c[None, None, None, None]
    t = jnp.where(valid, t.astype(F32), NEG)
    t = jnp.transpose(t, (1, 0, 2, 4, 3, 5))
    return t.reshape(3, NA_HEADS, NA_QR * GRID_W, NA_KR * GRID_W)


def _na_kernel(q_ref, k_ref, v_ref, bias_ref, o_ref, *, Lc, rows_n):
    W, d = GRID_W, NA_DIM
    scale = d ** -0.5
    kc = k_ref[0, 0:Lc, :]
    vc = v_ref[0, 0:Lc, :]
    sc = _dot_nt(q_ref[0, 0:Lc, :], kc) * scale
    pc = jnp.exp(sc - jnp.max(sc, axis=-1, keepdims=True))
    oc = _dot(pc.astype(BF16), vc) / jnp.sum(pc, axis=-1, keepdims=True)
    o_ref[0, 0:Lc, :] = oc.astype(o_ref.dtype)

    def group(gi, carry):
        r = gi * NA_QR
        kr0 = jnp.clip(r - NA_KH // 2, 0, rows_n - NA_KR)
        qoff = pl.multiple_of(Lc + r * W, NA_QR * W)
        koff = pl.multiple_of(Lc + kr0 * W, W)
        qg = q_ref[0, pl.ds(qoff, NA_QR * W), :]
        kw = k_ref[0, pl.ds(koff, NA_KR * W), :]
        vw = v_ref[0, pl.ds(koff, NA_KR * W), :]
        sw = _dot_nt(qg, kw) * scale + bias_ref[(r - kr0) // NA_QR, 0]
        sx = _dot_nt(qg, kc) * scale
        m = jnp.maximum(jnp.max(sw, axis=-1, keepdims=True), jnp.max(sx, axis=-1, keepdims=True))
        pw = jnp.exp(sw - m)
        px = jnp.exp(sx - m)
        l = jnp.sum(pw, axis=-1, keepdims=True) + jnp.sum(px, axis=-1, keepdims=True)
        o = (_dot(pw.astype(BF16), vw) + _dot(px.astype(BF16), vc)) / l
        o_ref[0, pl.ds(qoff, NA_QR * W), :] = o.astype(o_ref.dtype)
        return carry

    lax.fori_loop(0, rows_n // NA_QR, group, 0, unroll=2)


def _na(P, bias, Lc, col0):
    B, S, _ = P.shape
    rows_n = (S - Lc) // GRID_W
    assert rows_n % NA_QR == 0 and rows_n >= NA_KR
    c0 = col0 // NA_DIM
    blk = lambda c: pl.BlockSpec((1, S, NA_DIM), lambda b, h: (b, 0, c0 + c * NA_HEADS + h))
    return pl.pallas_call(
        functools.partial(_na_kernel, Lc=Lc, rows_n=rows_n), grid=(B, NA_HEADS),
        in_specs=[blk(0), blk(1), blk(2),
                  pl.BlockSpec((3, 1, NA_QR * GRID_W, NA_KR * GRID_W), lambda b, h: (0, h, 0, 0))],
        out_specs=pl.BlockSpec((1, S, NA_DIM), lambda b, h: (b, 0, h)),
        out_shape=jax.ShapeDtypeStruct((B, S, NA_WIDTH), BF16),
        compiler_params=_cp("parallel", "parallel"), name="na",
    )(P, P, P, bias)


def _rope_tables(Lc, N):
    quarter = GQA_DIM // 4
    t = jnp.arange(N)
    inv = ROPE_THETA ** (-(jnp.arange(quarter, dtype=F32) / quarter))
    ang_r = (t // GRID_W).astype(F32)[:, None] * inv[None, :]
    ang_c = (t % GRID_W).astype(F32)[:, None] * inv[None, :]
    cos = jnp.concatenate([jnp.cos(ang_r)] * 2 + [jnp.cos(ang_c)] * 2, axis=-1)
    sin = jnp.concatenate([-jnp.sin(ang_r), jnp.sin(ang_r), -jnp.sin(ang_c), jnp.sin(ang_c)], axis=-1)
    cos = jnp.concatenate([jnp.ones((Lc, GQA_DIM), F32), cos], axis=0)
    sin = jnp.concatenate([jnp.zeros((Lc, GQA_DIM), F32), sin], axis=0)
    return jnp.tile(cos, (1, LANE // GQA_DIM)), jnp.tile(sin, (1, LANE // GQA_DIM))


def _rope_rot(x, cos, sin):
    quarter = GQA_DIM // 4
    lane = lax.broadcasted_iota(jnp.int32, x.shape, 1)
    first = (lane % (2 * quarter)) < quarter
    rot = jnp.where(first, pltpu.roll(x, LANE - quarter, 1), pltpu.roll(x, quarter, 1))
    return x * cos + rot * sin


def _gqa_kernel(sink_ref, q_ref, k_ref, v_ref, cos_ref, sin_ref, o_ref, kd_ref, vd_ref, *, Lc, N):
    W, dh, KV = GQA_WINDOW, GQA_DIM, GQA_KV_HEADS
    G = GQA_HEADS // KV
    S = Lc + N
    i = pl.program_id(1)
    scale = dh ** -0.5

    @pl.when(i == 0)
    def _():
        low_t = lax.broadcasted_iota(jnp.int32, (TS, LANE), 1) < dh
        zero = jnp.zeros((W, LANE), BF16)
        for kh in range(KV):
            for ref in (kd_ref, vd_ref):
                ref[kh, Lc:Lc + W, :] = zero
                ref[kh, Lc + W + N:Lc + 2 * W + N, :] = zero

        def build(c, carry):
            r0 = pl.multiple_of(c * TS, TS)
            dst = pl.multiple_of(r0 + jnp.where(c >= Lc // TS, W, 0), W)
            cs = cos_ref[pl.ds(r0, TS), :]
            sn = sin_ref[pl.ds(r0, TS), :]
            for hh in range(KV // 2):
                kr = _rope_rot(k_ref[0, pl.ds(r0, TS), hh * LANE:(hh + 1) * LANE], cs, sn)
                vx = v_ref[0, pl.ds(r0, TS), hh * LANE:(hh + 1) * LANE]
                for ref, a in ((kd_ref, kr), (vd_ref, vx)):
                    sw = pltpu.roll(a, dh, 1)
                    ref[2 * hh, pl.ds(dst, TS), :] = jnp.where(low_t, a, sw).astype(BF16)
                    ref[2 * hh + 1, pl.ds(dst, TS), :] = jnp.where(low_t, sw, a).astype(BF16)
            return carry

        lax.fori_loop(0, S // TS, build, 0)

    span = 3 * W
    off = pl.multiple_of(Lc + i * W, W)
    cq = cos_ref[pl.ds(off, W), :]
    sq = sin_ref[pl.ds(off, W), :]
    ii = lax.broadcasted_iota(jnp.int32, (G * W, span), 0) % W
    jj = lax.broadcasted_iota(jnp.int32, (G * W, span), 1)
    kpos = i * W - W + jj
    valid = (kpos >= 0) & (kpos < N) & (jnp.abs(ii - (jj - W)) <= W)
    rg = lax.broadcasted_iota(jnp.int32, (G * W, 1), 0) // W
    low = lax.broadcasted_iota(jnp.int32, (W, LANE), 1) < dh
    for kh in range(KV):
        parts = []
        for slab in range(G // 2):
            j = kh * (G // 2) + slab
            qr = _rope_rot(q_ref[0, :, j * LANE:(j + 1) * LANE] * scale, cq, sq)
            parts += [jnp.where(low, qr, 0.0).astype(BF16), jnp.where(low, 0.0, qr).astype(BF16)]
        q4 = jnp.concatenate(parts, axis=0)
        kw = kd_ref[kh, pl.ds(off, span), :]
        vw = vd_ref[kh, pl.ds(off, span), :]
        kc = kd_ref[kh, 0:Lc, :]
        vc = vd_ref[kh, 0:Lc, :]
        sw = jnp.where(valid, _dot_nt(q4, kw), NEG)
        sx = _dot_nt(q4, kc)
        sk = sink_ref[G * kh + G - 1]
        for g in range(G - 2, -1, -1):
            sk = jnp.where(rg == g, sink_ref[G * kh + g], sk)
        m = jnp.maximum(jnp.maximum(jnp.max(sw, axis=-1, keepdims=True),
                                    jnp.max(sx, axis=-1, keepdims=True)), sk)
        pw = jnp.exp(sw - m)
        px = jnp.exp(sx - m)
        l = jnp.sum(pw, axis=-1, keepdims=True) + jnp.sum(px, axis=-1, keepdims=True) + jnp.exp(sk - m)
        o = (_dot(pw.astype(BF16), vw) + _dot(px.astype(BF16), vc)) / l
        for slab in range(G // 2):
            j = kh * (G // 2) + slab
            o_ref[0, :, j * LANE:(j + 1) * LANE] = jnp.where(
                low, o[2 * slab * W:(2 * slab + 1) * W], o[(2 * slab + 1) * W:(2 * slab + 2) * W]
            ).astype(o_ref.dtype)


def _gqa(P1, cos, sin, sink, Lc):
    B, S, _ = P1.shape
    N = S - Lc
    W = GQA_WINDOW
    rows = Lc + 2 * W + N
    table = pl.BlockSpec((S, LANE), lambda b, i: (0, 0))
    return pl.pallas_call(
        functools.partial(_gqa_kernel, Lc=Lc, N=N), grid=(B, N // W),
        in_specs=[pl.BlockSpec(memory_space=pltpu.SMEM),
                  pl.BlockSpec((1, W, GQA_Q), lambda b, i: (b, i + Lc // W, 0)),
                  pl.BlockSpec((1, S, GQA_KV), lambda b, i: (b, 0, GQA_Q // GQA_KV)),
                  pl.BlockSpec((1, S, GQA_KV), lambda b, i: (b, 0, GQA_Q // GQA_KV + 1)),
                  table, table],
        out_specs=pl.BlockSpec((1, W, GQA_Q), lambda b, i: (b, i, 0)),
        out_shape=jax.ShapeDtypeStruct((B, N, GQA_Q), BF16),
        scratch_shapes=[pltpu.VMEM((GQA_KV_HEADS, rows, LANE), BF16),
                        pltpu.VMEM((GQA_KV_HEADS, rows, LANE), BF16)],
        compiler_params=_cp("parallel", "arbitrary"), name="gqa",
    )(sink, P1, P1, P1, cos, sin)


def _gelu_tanh(x):
    return 0.5 * x * (1.0 + jnp.tanh(np.sqrt(2.0 / np.pi) * (x + 0.044715 * (x * x * x))))


def _lru_kernel(x_ref, g_ref, cw_ref, cb_ref, gw_ref, gb_ref, lam_ref, o_ref,
                af_ref, bf_ref, ar_ref, br_ref, *, Lc, N):
    S = Lc + N
    CH = 128
    LW = x_ref.shape[-1]
    nb = LW // LRU_BS
    nt = S // SUB
    ntc = Lc // SUB
    row = lax.broadcasted_iota(jnp.int32, (CH, LW), 0)
    crow = row % SUB
    first_row = row == 0
    last_row = row == CH - 1
    last_row2 = row == CH - 2
    sps = []
    for d in range(2):
        lam = lam_ref[d:d + 1, :]
        sps.append(jnp.maximum(-lam, 0.0) + jnp.log1p(jnp.exp(-jnp.abs(lam))))

    def gates(c, carry):
        off = pl.multiple_of(c * CH, CH)
        xc = x_ref[0, pl.ds(off, CH), :]
        prev = x_ref[0, pl.ds(pl.multiple_of(jnp.maximum(off - SUB, 0), SUB), SUB), :]
        nxt = x_ref[0, pl.ds(pl.multiple_of(jnp.minimum(off + CH, S - SUB), SUB), SUB), :]
        pz = jnp.where((c == 0) | (c == Lc // CH), 0.0, 1.0)
        nz = jnp.where((c == Lc // CH - 1) | (c == S // CH - 1), 0.0, 1.0)
        p7 = prev[SUB - 1:SUB, :] * pz
        n0 = nxt[0:1, :] * nz
        n1 = nxt[1:2, :] * nz
        xm1 = jnp.where(first_row, p7, pltpu.roll(xc, 1, 0))
        xp1 = jnp.where(last_row, n0, pltpu.roll(xc, CH - 1, 0))
        xp2 = jnp.where(last_row2, n0, jnp.where(last_row, n1, pltpu.roll(xc, CH - 2, 0)))
        u = (xm1 * cw_ref[0:1, :] + cb_ref[...] + xc * cw_ref[1:2, :] + xp1 * cw_ref[2:3, :]
             + xp2 * cw_ref[3:4, :])
        ub = u.astype(BF16)
        for d, (a_ref, b_ref) in enumerate(((af_ref, bf_ref), (ar_ref, br_ref))):
            pre = []
            for gi in range(2):
                pre.append(jnp.concatenate(
                    [_dot(ub[:, k * LRU_BS:(k + 1) * LRU_BS], gw_ref[d, gi, k].astype(BF16))
                     for k in range(nb)], axis=-1) + gb_ref[d, gi:gi + 1, :])
            r = _sigmoid(pre[0])
            ig = _sigmoid(pre[1])
            a = jnp.exp(-LRU_C * r * sps[d])
            b = jnp.sqrt(1.0 - a * a) * (ig * u)
            for s in (1, 2, 4):
                if d == 0:
                    keep = crow >= s
                    a_sh = pltpu.roll(a, s, 0)
                    b_sh = pltpu.roll(b, s, 0)
                else:
                    keep = crow < SUB - s
                    a_sh = pltpu.roll(a, CH - s, 0)
                    b_sh = pltpu.roll(b, CH - s, 0)
                b = jnp.where(keep, a * b_sh + b, b)
                a = jnp.where(keep, a * a_sh, a)
            a_ref[pl.ds(off, CH), :] = a
            b_ref[pl.ds(off, CH), :] = b
        return carry

    lax.fori_loop(0, S // CH, gates, 0)

    def scan(ft, rt, carry):
        hf, hr = carry
        of = pl.multiple_of(ft * SUB, SUB)
        orv = pl.multiple_of(rt * SUB, SUB)
        tf = af_ref[pl.ds(of, SUB), :] * hf + bf_ref[pl.ds(of, SUB), :]
        tr = ar_ref[pl.ds(orv, SUB), :] * hr + br_ref[pl.ds(orv, SUB), :]
        bf_ref[pl.ds(of, SUB), :] = tf
        br_ref[pl.ds(orv, SUB), :] = tr
        return (jnp.broadcast_to(tf[SUB - 1:SUB, :], (SUB, LW)), jnp.broadcast_to(tr[0:1, :], (SUB, LW)))

    zero = jnp.zeros((SUB, LW), F32)
    carry = lax.fori_loop(0, ntc, lambda i, c: scan(i, ntc - 1 - i, c), (zero, zero), unroll=8)
    lax.fori_loop(ntc, nt, lambda i, c: scan(i, nt - 1 - (i - ntc), c), carry, unroll=8)

    def outp(c, carry):
        off = pl.multiple_of(c * CH, CH)
        src = pl.multiple_of(Lc + off, CH)
        hs = bf_ref[pl.ds(src, CH), :] + br_ref[pl.ds(src, CH), :]
        o_ref[0, pl.ds(off, CH), :] = (hs * _gelu_tanh(g_ref[0, pl.ds(src, CH), :])).astype(o_ref.dtype)
        return carry

    lax.fori_loop(0, N // CH, outp, 0)


def _lru(P1, col_x, col_g, conv_w, conv_b, gate_w, gate_b, lam, Lc):
    B, S, _ = P1.shape
    N = S - Lc
    nb = 2
    lw = nb * LRU_BS
    return pl.pallas_call(
        functools.partial(_lru_kernel, Lc=Lc, N=N), grid=(B, LRU_BLOCKS // nb),
        in_specs=[pl.BlockSpec((1, S, lw), lambda b, j: (b, 0, col_x // lw + j)),
                  pl.BlockSpec((1, S, lw), lambda b, j: (b, 0, col_g // lw + j)),
                  pl.BlockSpec((4, lw), lambda b, j: (0, j)),
                  pl.BlockSpec((1, lw), lambda b, j: (0, j)),
                  pl.BlockSpec((2, 2, nb, LRU_BS, LRU_BS), lambda b, j: (0, 0, j, 0, 0)),
                  pl.BlockSpec((2, 2, lw), lambda b, j: (0, 0, j)),
                  pl.BlockSpec((2, lw), lambda b, j: (0, j))],
        out_specs=pl.BlockSpec((1, N, lw), lambda b, j: (b, 0, j)),
        out_shape=jax.ShapeDtypeStruct((B, N, LRU_WIDTH), BF16),
        scratch_shapes=[pltpu.VMEM((S, lw), F32) for _ in range(4)],
        compiler_params=_cp("parallel", "parallel"), name="lru",
    )(P1, P1, conv_w, conv_b.reshape(1, LRU_WIDTH), gate_w, gate_b, lam)


def _ffn_kernel(x_ref, w1_ref, w3_ref, w2_ref, g_ref, o_ref):
    f = pl.program_id(2)
    x = x_ref[0]
    u = _dot(x, w1_ref[0, 0].astype(BF16))
    v = _dot(x, w3_ref[0, 0].astype(BF16))
    a = (u * _sigmoid(u) * v).astype(BF16)
    y = _dot(a, w2_ref[0, 0].astype(BF16))

    @pl.when(f == 0)
    def _():
        o_ref[0] = y

    @pl.when(f > 0)
    def _():
        o_ref[0] += y

    @pl.when(f == pl.num_programs(2) - 1)
    def _():
        o_ref[0] = o_ref[0] * g_ref[0]


def _ffn(xs, w1, w3, w2, l, gs):
    E, R, D = xs.shape
    FF = w1.shape[-1]
    rm = R // 2
    tf = _pick(FF, (256, 128))
    return pl.pallas_call(
        _ffn_kernel, grid=(E, 2, FF // tf),
        in_specs=[pl.BlockSpec((1, rm, D), lambda e, m, f: (e, m, 0)),
                  pl.BlockSpec((1, 1, D, tf), lambda e, m, f: (l, e, 0, f)),
                  pl.BlockSpec((1, 1, D, tf), lambda e, m, f: (l, e, 0, f)),
                  pl.BlockSpec((1, 1, tf, D), lambda e, m, f: (l, e, f, 0)),
                  pl.BlockSpec((1, rm, 1), lambda e, m, f: (e, m, 0))],
        out_specs=pl.BlockSpec((1, rm, D), lambda e, m, f: (e, m, 0)),
        out_shape=jax.ShapeDtypeStruct((E, R, D), F32),
        compiler_params=_cp("parallel", "parallel", "arbitrary"), name="ffn",
    )(xs, w1, w3, w2, gs)


def _moe(h, router_bf, w1, w3, w2, l, Lc):
    B, S, D = h.shape
    E = N_EXPERTS
    hf = h.reshape(B * S, D)
    logits = _matmul(hf, router_bf, F32)[:, :E].reshape(B, S, E)
    aff = jax.nn.softmax(logits, axis=-1)
    ids, gs = [], []
    for lo, n in ((Lc, S - Lc), (0, Lc)):
        if n == 0:
            continue
        cap = EC_FACTOR * n // E
        g, idx = lax.top_k(jnp.swapaxes(aff[:, lo:lo + n], 1, 2), cap)
        tok = idx + lo + (jnp.arange(B, dtype=idx.dtype) * S)[:, None, None]
        ids.append(jnp.swapaxes(tok, 0, 1).reshape(E, B * cap))
        gs.append(jnp.swapaxes(g, 0, 1).reshape(E, B * cap))
    ids = jnp.concatenate(ids, axis=1)
    gs = jnp.concatenate(gs, axis=1)
    xs = jnp.take(hf, ids.reshape(-1), axis=0, mode="clip").reshape(E, -1, D)
    y = _ffn(xs, w1, w3, w2, l, gs[..., None])
    out = jnp.zeros((B * S, D), F32).at[ids.reshape(-1)].add(y.reshape(-1, D))
    return out.reshape(B, S, D)


def kernel(x, c, ctx, c_ctx, ada_w, ada_b, norm_mix_pre, norm_mix_post, norm_ffn_pre, norm_ffn_post,
           ab_w_in, ab_w_out, mlstm_gate_b, mlstm_norm, na_rpb,
           cd_w_in, cd_w_out, gqa_sink, lru_conv_w, lru_conv_b, lru_gate_w, lru_gate_b, lru_lambda,
           moe_router, moe_w1, moe_w3, moe_w2):
    B, N, D = x.shape
    Lc = ctx.shape[1]
    S = Lc + N
    assert Lc == TS and N % TS == 0 and B < SUB
    ctx_row = B
    xs = jnp.concatenate([ctx, x], axis=1)
    cond = jnp.concatenate([c, c_ctx[None], jnp.zeros((SUB - B - 1, D), F32)], axis=0)
    mod = _ada(cond, ada_w, ada_b).reshape(-1, 1, D)
    router = jnp.pad(moe_router, ((0, 0), (0, 0), (0, LANE - N_EXPERTS))).astype(BF16)

    h = _prenorm(xs, norm_mix_pre[0], mod, 0, ctx_row, True)
    hf = h.reshape(B * S, D)
    w_in = ab_w_in[0]
    o0 = 3 * MLSTM_QK
    n0 = o0 + MLSTM_V + MLSTM_GATES
    w_qkv = jnp.concatenate([w_in[:, :o0], w_in[:, n0:]], axis=1).astype(BF16)
    w_og = jnp.pad(w_in[:, o0:n0], ((0, 0), (0, LANE - MLSTM_GATES))).astype(BF16)
    P = _matmul(hf, w_qkv, BF16).reshape(B, S, -1)
    OG = _matmul(hf, w_og, F32).reshape(B, S, -1)
    gcol = OG[:, :, MLSTM_V:MLSTM_V + MLSTM_GATES]
    Hd = _mlstm(P, gcol, jnp.swapaxes(gcol, 1, 2), mlstm_gate_b[0])
    ML = _mlstm_out(Hd, OG, mlstm_norm[0])
    NL = _na(P, _na_bias(na_rpb[0]), Lc, o0)
    Y = _matmul2(ML.reshape(B * S, -1), NL.reshape(B * S, -1), ab_w_out[0].astype(BF16)).reshape(B, S, D)
    xs, h = _resid(xs, Y, norm_mix_post[0], mod, 0, 2, ctx_row, 0, (norm_ffn_pre[0], 0, 3, 4))
    Y = _moe(h, router[0], moe_w1, moe_w3, moe_w2, 0, Lc)
    xs, h = _resid(xs, Y, norm_ffn_post[0], mod, 0, 5, ctx_row, 0, (norm_mix_pre[1], 1, 0, 1))

    P1 = _matmul(h.reshape(B * S, D), cd_w_in[0].astype(BF16), F32).reshape(B, S, -1)
    cos, sin = _rope_tables(Lc, N)
    AL = _gqa(P1, cos, sin, gqa_sink[0], Lc)
    RL = _lru(P1, GQA_Q + 2 * GQA_KV, GQA_Q + 2 * GQA_KV + LRU_WIDTH,
              lru_conv_w[0], lru_conv_b[0], lru_gate_w[0], lru_gate_b[0], lru_lambda[0], Lc)
    Y = _matmul2(AL.reshape(B * N, -1), RL.reshape(B * N, -1), cd_w_out[0].astype(BF16)).reshape(B, N, D)
    xl, h = _resid(xs, Y, norm_mix_post[1], mod, 1, 2, None, Lc // TS, (norm_ffn_pre[1], 1, 3, 4))
    Y = _moe(h, router[1], moe_w1, moe_w3, moe_w2, 1, 0)
    xl, _ = _resid(xl, Y, norm_ffn_post[1], mod, 1, 5, None, 0, None)
    return xl
```

```python
import functools

import numpy as np
import jax
import jax.numpy as jnp
from jax import lax
from jax.experimental import pallas as pl
from jax.experimental.pallas import tpu as pltpu

F32 = jnp.float32
BF16 = jnp.bfloat16
EPS = 1e-6
NEG = -1e30

GRID_W = 64
MLSTM_HEADS = 4
MLSTM_DK = 256
MLSTM_DV = 256
MLSTM_QK = MLSTM_HEADS * MLSTM_DK
MLSTM_V = MLSTM_HEADS * MLSTM_DV
MLSTM_GATES = 2 * 2 * MLSTM_HEADS
NA_HEADS = 8
NA_DIM = 128
NA_WIDTH = NA_HEADS * NA_DIM
NA_KH = 8
NA_KW = 16
GQA_HEADS = 16
GQA_KV_HEADS = 4
GQA_DIM = 64
GQA_Q = GQA_HEADS * GQA_DIM
GQA_KV = GQA_KV_HEADS * GQA_DIM
GQA_WINDOW = 128
ROPE_THETA = 10000.0
LRU_WIDTH = 1024
LRU_BLOCKS = 8
LRU_BS = LRU_WIDTH // LRU_BLOCKS
LRU_C = 8.0
N_EXPERTS = 16
EC_FACTOR = 2

TS = 256
LANE = 128
SUB = 8
VMEM_LIMIT = 56 * 1024 * 1024


def _cp(*sem):
    return pltpu.CompilerParams(dimension_semantics=sem, vmem_limit_bytes=VMEM_LIMIT)


def _pick(n, prefs):
    for p in prefs:
        if n % p == 0:
            return p
    return n


def _sigmoid(x):
    return 1.0 / (1.0 + jnp.exp(-x))


def _dot(a, b):
    return jnp.dot(a, b, preferred_element_type=F32)


def _dot_nt(a, b):
    return lax.dot_general(a, b, (((1,), (1,)), ((), ())), preferred_element_type=F32)


def _dot_tn(a, b):
    return lax.dot_general(a, b, (((0,), (0,)), ((), ())), preferred_element_type=F32)


def _rms(x, w):
    return x * lax.rsqrt(jnp.mean(x * x, axis=-1, keepdims=True) + EPS) * w


def _ada_kernel(c_ref, w_ref, b_ref, o_ref):
    c = c_ref[...]
    a = (c * _sigmoid(c)).astype(BF16)
    o_ref[0] = _dot(a, w_ref[0].astype(BF16)) + b_ref[0]


def _ada(cond, ada_w, ada_b):
    L, D, D6 = ada_w.shape
    tn = _pick(D6, (1024, 512, 256, 128))
    return pl.pallas_call(
        _ada_kernel, grid=(L, D6 // tn),
        in_specs=[pl.BlockSpec((SUB, D), lambda l, j: (0, 0)),
                  pl.BlockSpec((1, D, tn), lambda l, j: (l, 0, j)),
                  pl.BlockSpec((1, 1, tn), lambda l, j: (l, 0, j))],
        out_specs=pl.BlockSpec((1, SUB, tn), lambda l, j: (l, 0, j)),
        out_shape=jax.ShapeDtypeStruct((L, SUB, D6), F32),
        compiler_params=_cp("parallel", "parallel"), name="ada",
    )(cond, ada_w, ada_b.reshape(L, 1, D6))


def _mod_map(l, k, ctx_row, has_ctx):
    if has_ctx:
        return lambda b, s: ((l * SUB + jnp.where(s == 0, ctx_row, b)) * 6 + k, 0, 0)
    return lambda b, s: ((l * SUB + b) * 6 + k, 0, 0)


def _prenorm_kernel(x_ref, w_ref, sh_ref, sc_ref, o_ref):
    y = _rms(x_ref[0], w_ref[...])
    o_ref[0] = (y * (1.0 + sc_ref[0]) + sh_ref[0]).astype(o_ref.dtype)


def _prenorm(x, w, mod, l, ctx_row, has_ctx):
    B, S, D = x.shape
    vec = lambda k: pl.BlockSpec((1, 1, D), _mod_map(l, k, ctx_row, has_ctx))
    return pl.pallas_call(
        _prenorm_kernel, grid=(B, S // TS),
        in_specs=[pl.BlockSpec((1, TS, D), lambda b, s: (b, s, 0)),
                  pl.BlockSpec((1, D), lambda b, s: (0, 0)), vec(0), vec(1)],
        out_specs=pl.BlockSpec((1, TS, D), lambda b, s: (b, s, 0)),
        out_shape=jax.ShapeDtypeStruct((B, S, D), BF16),
        compiler_params=_cp("parallel", "parallel"), name="prenorm",
    )(x, w.reshape(1, D), mod, mod)


def _resid_kernel(has_next, x_ref, y_ref, wpost_ref, g_ref, *rest):
    if has_next:
        wpre_ref, sh_ref, sc_ref, xo_ref, ho_ref = rest
    else:
        (xo_ref,) = rest
    xn = x_ref[0] + g_ref[0] * _rms(y_ref[0], wpost_ref[...])
    xo_ref[0] = xn
    if has_next:
        h = _rms(xn, wpre_ref[...])
        ho_ref[0] = (h * (1.0 + sc_ref[0]) + sh_ref[0]).astype(ho_ref.dtype)


def _resid(x, y, wpost, mod, l, kg, ctx_row, x_off, nxt):
    B, Sy, D = y.shape
    has_ctx = x_off == 0 and x.shape[1] == Sy and ctx_row is not None
    vec = lambda ll, k: pl.BlockSpec((1, 1, D), _mod_map(ll, k, ctx_row, has_ctx))
    row = pl.BlockSpec((1, D), lambda b, s: (0, 0))
    tile = pl.BlockSpec((1, TS, D), lambda b, s: (b, s, 0))
    in_specs = [pl.BlockSpec((1, TS, D), lambda b, s: (b, s + x_off, 0)), tile, row, vec(l, kg)]
    args = [x, y, wpost.reshape(1, D), mod]
    out_specs = [tile]
    out_shape = [jax.ShapeDtypeStruct((B, Sy, D), F32)]
    if nxt is not None:
        wpre, ln, ksh, ksc, hdtype = nxt
        in_specs += [row, vec(ln, ksh), vec(ln, ksc)]
        args += [wpre.reshape(1, D), mod, mod]
        out_specs.append(tile)
        out_shape.append(jax.ShapeDtypeStruct((B, Sy, D), hdtype))
    out = pl.pallas_call(
        functools.partial(_resid_kernel, nxt is not None), grid=(B, Sy // TS),
        in_specs=in_specs, out_specs=out_specs, out_shape=out_shape,
        compiler_params=_cp("parallel", "parallel"), name="resid",
    )(*args)
    return out if nxt is not None else (out[0], None)


def _mm_kernel(x_ref, w_ref, o_ref):
    o_ref[...] = _dot(x_ref[...].astype(BF16), w_ref[...]).astype(o_ref.dtype)


def _matmul(x, w, out_dtype):
    M, K = x.shape
    N = w.shape[1]
    tm = _pick(M, (1024, 512, 256))
    tn = _pick(N, (512, 384, 256, 128))
    return pl.pallas_call(
        _mm_kernel, grid=(M // tm, N // tn),
        in_specs=[pl.BlockSpec((tm, K), lambda i, j: (i, 0)),
                  pl.BlockSpec((K, tn), lambda i, j: (0, j))],
        out_specs=pl.BlockSpec((tm, tn), lambda i, j: (i, j)),
        out_shape=jax.ShapeDtypeStruct((M, N), out_dtype),
        compiler_params=_cp("parallel", "parallel"), name="matmul",
    )(x, w)


def _mm2_kernel(x1_ref, x2_ref, w1_ref, w2_ref, o_ref):
    o_ref[...] = _dot(x1_ref[...], w1_ref[...]) + _dot(x2_ref[...], w2_ref[...])


def _matmul2(x1, x2, w):
    M, K1 = x1.shape
    K2 = x2.shape[1]
    assert K1 == K2 and w.shape[0] == K1 + K2
    N = w.shape[1]
    tm = _pick(M, (1024, 512, 256))
    tn = _pick(N, (512, 256, 128))
    return pl.pallas_call(
        _mm2_kernel, grid=(M // tm, N // tn),
        in_specs=[pl.BlockSpec((tm, K1), lambda i, j: (i, 0)),
                  pl.BlockSpec((tm, K2), lambda i, j: (i, 0)),
                  pl.BlockSpec((K1, tn), lambda i, j: (0, j)),
                  pl.BlockSpec((K2, tn), lambda i, j: (1, j))],
        out_specs=pl.BlockSpec((tm, tn), lambda i, j: (i, j)),
        out_shape=jax.ShapeDtypeStruct((M, N), F32),
        compiler_params=_cp("parallel", "parallel"), name="matmul2",
    )(x1, x2, w, w)


def _log_sigmoid(x):
    return jnp.minimum(x, 0.0) - jnp.log1p(jnp.exp(-jnp.abs(x)))


def _mlstm_kernel(q_ref, k_ref, v_ref, gc_ref, gr_ref, bc_ref, br_ref, o_ref, C_ref, n_ref, m_ref):
    H, dk, dv, L = MLSTM_HEADS, MLSTM_DK, MLSTM_DV, TS
    d = pl.program_id(1)
    t = pl.program_id(2)

    @pl.when(t == 0)
    def _():
        C_ref[...] = jnp.zeros_like(C_ref)
        n_ref[...] = jnp.zeros_like(n_ref)
        m_ref[...] = jnp.zeros_like(m_ref)

    fwd = d == 0
    ri = lax.broadcasted_iota(jnp.int32, (L, L), 0)
    ci = lax.broadcasted_iota(jnp.int32, (L, L), 1)
    diff = (ci - ri) * (1 - 2 * d)
    causal = diff <= 0
    causal_f = jnp.where(causal, 1.0, 0.0)
    causal_t = jnp.where(diff >= 0, 1.0, 0.0)
    gc = gc_ref[0] + bc_ref[...]
    gr = gr_ref[0] + br_ref[...]
    lfc = _log_sigmoid(gc)
    lfr = _log_sigmoid(gr)
    hi = lax.Precision.HIGHEST
    bcol_all = jnp.dot(causal_f, lfc, precision=hi, preferred_element_type=F32)
    brow_all = jnp.dot(lfr, causal_t, precision=hi, preferred_element_type=F32)
    tot_all = jnp.sum(lfr, axis=-1, keepdims=True)
    sel = lambda a, b: jnp.where(fwd, a, b)
    scale = dk ** -0.5
    for h in range(H):
        ic = sel(gc[:, h:h + 1], gc[:, 2 * H + h:2 * H + h + 1])
        ir = sel(gr[h:h + 1], gr[2 * H + h:2 * H + h + 1])
        bc = sel(bcol_all[:, H + h:H + h + 1], bcol_all[:, 3 * H + h:3 * H + h + 1])
        br = sel(brow_all[H + h:H + h + 1], brow_all[3 * H + h:3 * H + h + 1])
        tot = sel(tot_all[H + h:H + h + 1], tot_all[3 * H + h:3 * H + h + 1])
        m = m_ref[h]
        dmat = jnp.where(causal, bc - br + ir, NEG)
        inter = bc + m
        mj = jnp.maximum(inter, jnp.max(dmat, axis=-1, keepdims=True))
        p = jnp.exp(dmat - mj)
        q = q_ref[0, :, h * dk:(h + 1) * dk]
        k = k_ref[0, :, h * dk:(h + 1) * dk]
        v = v_ref[0, :, h * dv:(h + 1) * dv]
        s = _dot_nt(q, k) * scale * p
        w_inter = jnp.exp(inter - mj)
        num = w_inter * _dot(q, C_ref[h].astype(BF16)) + _dot(s.astype(BF16), v)
        qn = jnp.sum(q.astype(F32) * n_ref[h], axis=-1, keepdims=True)
        den = w_inter * qn + jnp.sum(s, axis=-1, keepdims=True)
        o_ref[0, 0, :, h * dv:(h + 1) * dv] = num / jnp.maximum(jnp.abs(den), jnp.exp(-mj))
        g_c = tot - bc + ic
        g_r = tot - br + ir
        m_new = jnp.maximum(tot + m, jnp.max(g_r, axis=-1, keepdims=True))
        decay = jnp.exp(tot + m - m_new)
        wk = jnp.exp(g_c - m_new) * (k.astype(F32) * scale)
        C_ref[h] = decay * C_ref[h] + _dot_tn(wk.astype(BF16), v)
        n_ref[h] = decay * n_ref[h] + jnp.sum(wk, axis=0, keepdims=True)
        m_ref[h] = m_new


def _mlstm(P, gcol, grow, gate_b):
    B, S, _ = P.shape
    nc = S // TS
    chunk = lambda d, t: jnp.where(d == 0, t, jnp.where(t == 0, 0, nc - t))
    blk = lambda c: pl.BlockSpec((1, TS, MLSTM_QK), lambda b, d, t: (b, chunk(d, t), c))
    G = MLSTM_GATES
    return pl.pallas_call(
        _mlstm_kernel, grid=(B, 2, nc),
        in_specs=[blk(0), blk(1), blk(2),
                  pl.BlockSpec((1, TS, G), lambda b, d, t: (b, chunk(d, t), 0)),
                  pl.BlockSpec((1, G, TS), lambda b, d, t: (b, 0, chunk(d, t))),
                  pl.BlockSpec((1, G), lambda b, d, t: (0, 0)),
                  pl.BlockSpec((G, 1), lambda b, d, t: (0, 0))],
        out_specs=pl.BlockSpec((1, 1, TS, MLSTM_V), lambda b, d, t: (d, b, chunk(d, t), 0)),
        out_shape=jax.ShapeDtypeStruct((2, B, S, MLSTM_V), F32),
        scratch_shapes=[pltpu.VMEM((MLSTM_HEADS, MLSTM_DK, MLSTM_DV), F32),
                        pltpu.VMEM((MLSTM_HEADS, 1, MLSTM_DK), F32),
                        pltpu.VMEM((MLSTM_HEADS, 1, 1), F32)],
        compiler_params=_cp("parallel", "arbitrary", "arbitrary"), name="mlstm",
    )(P, P, P, gcol, grow, gate_b.reshape(1, G), gate_b.reshape(G, 1))


def _mlstm_out_kernel(hf_ref, hr_ref, o_ref, gain_ref, out_ref):
    dv = MLSTM_DV
    for h in range(MLSTM_HEADS):
        sl = slice(h * dv, (h + 1) * dv)
        x = hf_ref[0, 0, :, sl] + hr_ref[0, 0, :, sl]
        out_ref[0, :, sl] = (_rms(x, gain_ref[:, sl]) * _sigmoid(o_ref[0, :, sl])).astype(out_ref.dtype)


def _mlstm_out(Hd, OG, gain):
    _, B, S, V = Hd.shape
    return pl.pallas_call(
        _mlstm_out_kernel, grid=(B, S // TS),
        in_specs=[pl.BlockSpec((1, 1, TS, V), lambda b, s: (0, b, s, 0)),
                  pl.BlockSpec((1, 1, TS, V), lambda b, s: (1, b, s, 0)),
                  pl.BlockSpec((1, TS, V), lambda b, s: (b, s, 0)),
                  pl.BlockSpec((1, V), lambda b, s: (0, 0))],
        out_specs=pl.BlockSpec((1, TS, V), lambda b, s: (b, s, 0)),
        out_shape=jax.ShapeDtypeStruct((B, S, V), BF16),
        compiler_params=_cp("parallel", "parallel"), name="mlstm_out",
    )(Hd, Hd, OG, gain.reshape(1, V))


NA_QR = 4
NA_KR = NA_KH + 2 * NA_QR - 4


def _na_bias(rpb):
    a = np.arange(NA_QR)
    kr = np.arange(NA_KR)
    qrel = NA_QR * np.arange(3)[:, None] + a[None, :]
    r0rel = np.stack([0 * a, a, NA_KR - NA_KH + 0 * a])
    valid_r = (kr[None, None] >= r0rel[..., None]) & (kr[None, None] < r0rel[..., None] + NA_KH)
    drow = np.clip(kr[None, None] - qrel[..., None] + (NA_KH - 1), 0, 2 * NA_KH - 2)
    w = np.arange(GRID_W)[:, None]
    cc = np.arange(GRID_W)[None, :]
    col_start = np.clip(w - NA_KW // 2, 0, GRID_W - NA_KW)
    valid_c = (cc >= col_start) & (cc < col_start + NA_KW)
    dcol = np.clip(cc - w + (NA_KW - 1), 0, 2 * NA_KW - 2)
    t = rpb[:, drow]
    t = t[..., dcol]
    valid = valid_r[None, :, :, :, None, None] & valid_c[None, None, None, None]
    t = jnp.where(valid, t.astype(F32), NEG)
    t = jnp.transpose(t, (1, 0, 2, 4, 3, 5))
    return t.reshape(3, NA_HEADS, NA_QR * GRID_W, NA_KR * GRID_W)


def _na_kernel(q_ref, k_ref, v_ref, bias_ref, o_ref, *, Lc, rows_n):
    W, d = GRID_W, NA_DIM
    scale = d ** -0.5
    kc = k_ref[0, 0:Lc, :]
    vc = v_ref[0, 0:Lc, :]
    sc = _dot_nt(q_ref[0, 0:Lc, :], kc) * scale
    pc = jnp.exp(sc - jnp.max(sc, axis=-1, keepdims=True))
    oc = _dot(pc.astype(BF16), vc) / jnp.sum(pc, axis=-1, keepdims=True)
    o_ref[0, 0:Lc, :] = oc.astype(o_ref.dtype)

    def group(gi, carry):
        r = gi * NA_QR
        kr0 = jnp.clip(r - NA_KH // 2, 0, rows_n - NA_KR)
        qoff = pl.multiple_of(Lc + r * W, NA_QR * W)
        koff = pl.multiple_of(Lc + kr0 * W, W)
        qg = q_ref[0, pl.ds(qoff, NA_QR * W), :]
        kw = k_ref[0, pl.ds(koff, NA_KR * W), :]
        vw = v_ref[0, pl.ds(koff, NA_KR * W), :]
        sw = _dot_nt(qg, kw) * scale + bias_ref[(r - kr0) // NA_QR, 0]
        sx = _dot_nt(qg, kc) * scale
        m = jnp.maximum(jnp.max(sw, axis=-1, keepdims=True), jnp.max(sx, axis=-1, keepdims=True))
        pw = jnp.exp(sw - m)
        px = jnp.exp(sx - m)
        l = jnp.sum(pw, axis=-1, keepdims=True) + jnp.sum(px, axis=-1, keepdims=True)
        o = (_dot(pw.astype(BF16), vw) + _dot(px.astype(BF16), vc)) / l
        o_ref[0, pl.ds(qoff, NA_QR * W), :] = o.astype(o_ref.dtype)
        return carry

    lax.fori_loop(0, rows_n // NA_QR, group, 0, unroll=2)


def _na(P, bias, Lc, col0):
    B, S, _ = P.shape
    rows_n = (S - Lc) // GRID_W
    assert rows_n % NA_QR == 0 and rows_n >= NA_KR
    c0 = col0 // NA_DIM
    blk = lambda c: pl.BlockSpec((1, S, NA_DIM), lambda b, h: (b, 0, c0 + c * NA_HEADS + h))
    return pl.pallas_call(
        functools.partial(_na_kernel, Lc=Lc, rows_n=rows_n), grid=(B, NA_HEADS),
        in_specs=[blk(0), blk(1), blk(2),
                  pl.BlockSpec((3, 1, NA_QR * GRID_W, NA_KR * GRID_W), lambda b, h: (0, h, 0, 0))],
        out_specs=pl.BlockSpec((1, S, NA_DIM), lambda b, h: (b, 0, h)),
        out_shape=jax.ShapeDtypeStruct((B, S, NA_WIDTH), BF16),
        compiler_params=_cp("parallel", "parallel"), name="na",
    )(P, P, P, bias)


def _rope_tables(Lc, N):
    quarter = GQA_DIM // 4
    t = jnp.arange(N)
    inv = ROPE_THETA ** (-(jnp.arange(quarter, dtype=F32) / quarter))
    ang_r = (t // GRID_W).astype(F32)[:, None] * inv[None, :]
    ang_c = (t % GRID_W).astype(F32)[:, None] * inv[None, :]
    cos = jnp.concatenate([jnp.cos(ang_r)] * 2 + [jnp.cos(ang_c)] * 2, axis=-1)
    sin = jnp.concatenate([-jnp.sin(ang_r), jnp.sin(ang_r), -jnp.sin(ang_c), jnp.sin(ang_c)], axis=-1)
    cos = jnp.concatenate([jnp.ones((Lc, GQA_DIM), F32), cos], axis=0)
    sin = jnp.concatenate([jnp.zeros((Lc, GQA_DIM), F32), sin], axis=0)
    return jnp.tile(cos, (1, LANE // GQA_DIM)), jnp.tile(sin, (1, LANE // GQA_DIM))


def _rope_rot(x, cos, sin):
    quarter = GQA_DIM // 4
    lane = lax.broadcasted_iota(jnp.int32, x.shape, 1)
    first = (lane % (2 * quarter)) < quarter
    rot = jnp.where(first, pltpu.roll(x, LANE - quarter, 1), pltpu.roll(x, quarter, 1))
    return x * cos + rot * sin


def _gqa_kernel(sink_ref, q_ref, k_ref, v_ref, cos_ref, sin_ref, o_ref, kd_ref, vd_ref, *, Lc, N):
    W, dh, KV = GQA_WINDOW, GQA_DIM, GQA_KV_HEADS
    G = GQA_HEADS // KV
    S = Lc + N
    i = pl.program_id(1)
    scale = dh ** -0.5

    @pl.when(i == 0)
    def _():
        low_t = lax.broadcasted_iota(jnp.int32, (TS, LANE), 1) < dh
        zero = jnp.zeros((W, LANE), BF16)
        for kh in range(KV):
            for ref in (kd_ref, vd_ref):
                ref[kh, Lc:Lc + W, :] = zero
                ref[kh, Lc + W + N:Lc + 2 * W + N, :] = zero

        def build(c, carry):
            r0 = pl.multiple_of(c * TS, TS)
            dst = pl.multiple_of(r0 + jnp.where(c >= Lc // TS, W, 0), W)
            cs = cos_ref[pl.ds(r0, TS), :]
            sn = sin_ref[pl.ds(r0, TS), :]
            for hh in range(KV // 2):
                kr = _rope_rot(k_ref[0, pl.ds(r0, TS), hh * LANE:(hh + 1) * LANE], cs, sn)
                vx = v_ref[0, pl.ds(r0, TS), hh * LANE:(hh + 1) * LANE]
                for ref, a in ((kd_ref, kr), (vd_ref, vx)):
                    sw = pltpu.roll(a, dh, 1)
                    ref[2 * hh, pl.ds(dst, TS), :] = jnp.where(low_t, a, sw).astype(BF16)
                    ref[2 * hh + 1, pl.ds(dst, TS), :] = jnp.where(low_t, sw, a).astype(BF16)
            return carry

        lax.fori_loop(0, S // TS, build, 0)

    span = 3 * W
    off = pl.multiple_of(Lc + i * W, W)
    cq = cos_ref[pl.ds(off, W), :]
    sq = sin_ref[pl.ds(off, W), :]
    ii = lax.broadcasted_iota(jnp.int32, (G * W, span), 0) % W
    jj = lax.broadcasted_iota(jnp.int32, (G * W, span), 1)
    kpos = i * W - W + jj
    valid = (kpos >= 0) & (kpos < N) & (jnp.abs(ii - (jj - W)) <= W)
    rg = lax.broadcasted_iota(jnp.int32, (G * W, 1), 0) // W
    low = lax.broadcasted_iota(jnp.int32, (W, LANE), 1) < dh
    for kh in range(KV):
        parts = []
        for slab in range(G // 2):
            j = kh * (G // 2) + slab
            qr = _rope_rot(q_ref[0, :, j * LANE:(j + 1) * LANE] * scale, cq, sq)
            parts += [jnp.where(low, qr, 0.0).astype(BF16), jnp.where(low, 0.0, qr).astype(BF16)]
        q4 = jnp.concatenate(parts, axis=0)
        kw = kd_ref[kh, pl.ds(off, span), :]
        vw = vd_ref[kh, pl.ds(off, span), :]
        kc = kd_ref[kh, 0:Lc, :]
        vc = vd_ref[kh, 0:Lc, :]
        sw = jnp.where(valid, _dot_nt(q4, kw), NEG)
        sx = _dot_nt(q4, kc)
        sk = sink_ref[G * kh + G - 1]
        for g in range(G - 2, -1, -1):
            sk = jnp.where(rg == g, sink_ref[G * kh + g], sk)
        m = jnp.maximum(jnp.maximum(jnp.max(sw, axis=-1, keepdims=True),
                                    jnp.max(sx, axis=-1, keepdims=True)), sk)
        pw = jnp.exp(sw - m)
        px = jnp.exp(sx - m)
        l = jnp.sum(pw, axis=-1, keepdims=True) + jnp.sum(px, axis=-1, keepdims=True) + jnp.exp(sk - m)
        o = (_dot(pw.astype(BF16), vw) + _dot(px.astype(BF16), vc)) / l
        for slab in range(G // 2):
            j = kh * (G // 2) + slab
            o_ref[0, :, j * LANE:(j + 1) * LANE] = jnp.where(
                low, o[2 * slab * W:(2 * slab + 1) * W], o[(2 * slab + 1) * W:(2 * slab + 2) * W]
            ).astype(o_ref.dtype)


def _gqa(P1, cos, sin, sink, Lc):
    B, S, _ = P1.shape
    N = S - Lc
    W = GQA_WINDOW
    rows = Lc + 2 * W + N
    table = pl.BlockSpec((S, LANE), lambda b, i: (0, 0))
    return pl.pallas_call(
        functools.partial(_gqa_kernel, Lc=Lc, N=N), grid=(B, N // W),
        in_specs=[pl.BlockSpec(memory_space=pltpu.SMEM),
                  pl.BlockSpec((1, W, GQA_Q), lambda b, i: (b, i + Lc // W, 0)),
                  pl.BlockSpec((1, S, GQA_KV), lambda b, i: (b, 0, GQA_Q // GQA_KV)),
                  pl.BlockSpec((1, S, GQA_KV), lambda b, i: (b, 0, GQA_Q // GQA_KV + 1)),
                  table, table],
        out_specs=pl.BlockSpec((1, W, GQA_Q), lambda b, i: (b, i, 0)),
        out_shape=jax.ShapeDtypeStruct((B, N, GQA_Q), BF16),
        scratch_shapes=[pltpu.VMEM((GQA_KV_HEADS, rows, LANE), BF16),
                        pltpu.VMEM((GQA_KV_HEADS, rows, LANE), BF16)],
        compiler_params=_cp("parallel", "arbitrary"), name="gqa",
    )(sink, P1, P1, P1, cos, sin)


def _gelu_tanh(x):
    return 0.5 * x * (1.0 + jnp.tanh(np.sqrt(2.0 / np.pi) * (x + 0.044715 * (x * x * x))))


def _lru_kernel(x_ref, g_ref, cw_ref, cb_ref, gw_ref, gb_ref, lam_ref, o_ref,
                af_ref, bf_ref, ar_ref, br_ref, *, Lc, N):
    S = Lc + N
    CH = 128
    LW = x_ref.shape[-1]
    nb = LW // LRU_BS
    nt = S // SUB
    ntc = Lc // SUB
    row = lax.broadcasted_iota(jnp.int32, (CH, LW), 0)
    crow = row % SUB
    first_row = row == 0
    last_row = row == CH - 1
    last_row2 = row == CH - 2
    sps = []
    for d in range(2):
        lam = lam_ref[d:d + 1, :]
        sps.append(jnp.maximum(-lam, 0.0) + jnp.log1p(jnp.exp(-jnp.abs(lam))))

    def gates(c, carry):
        off = pl.multiple_of(c * CH, CH)
        xc = x_ref[0, pl.ds(off, CH), :]
        prev = x_ref[0, pl.ds(pl.multiple_of(jnp.maximum(off - SUB, 0), SUB), SUB), :]
        nxt = x_ref[0, pl.ds(pl.multiple_of(jnp.minimum(off + CH, S - SUB), SUB), SUB), :]
        pz = jnp.where((c == 0) | (c == Lc // CH), 0.0, 1.0)
        nz = jnp.where((c == Lc // CH - 1) | (c == S // CH - 1), 0.0, 1.0)
        p7 = prev[SUB - 1:SUB, :] * pz
        n0 = nxt[0:1, :] * nz
        n1 = nxt[1:2, :] * nz
        xm1 = jnp.where(first_row, p7, pltpu.roll(xc, 1, 0))
        xp1 = jnp.where(last_row, n0, pltpu.roll(xc, CH - 1, 0))
        xp2 = jnp.where(last_row2, n0, jnp.where(last_row, n1, pltpu.roll(xc, CH - 2, 0)))
        u = (xm1 * cw_ref[0:1, :] + cb_ref[...] + xc * cw_ref[1:2, :] + xp1 * cw_ref[2:3, :]
             + xp2 * cw_ref[3:4, :])
        ub = u.astype(BF16)
        for d, (a_ref, b_ref) in enumerate(((af_ref, bf_ref), (ar_ref, br_ref))):
            pre = []
            for gi in range(2):
                pre.append(jnp.concatenate(
                    [_dot(ub[:, k * LRU_BS:(k + 1) * LRU_BS], gw_ref[d, gi, k].astype(BF16))
                     for k in range(nb)], axis=-1) + gb_ref[d, gi:gi + 1, :])
            r = _sigmoid(pre[0])
            ig = _sigmoid(pre[1])
            a = jnp.exp(-LRU_C * r * sps[d])
            b = jnp.sqrt(1.0 - a * a) * (ig * u)
            for s in (1, 2, 4):
                if d == 0:
                    keep = crow >= s
                    a_sh = pltpu.roll(a, s, 0)
                    b_sh = pltpu.roll(b, s, 0)
                else:
                    keep = crow < SUB - s
                    a_sh = pltpu.roll(a, CH - s, 0)
                    b_sh = pltpu.roll(b, CH - s, 0)
                b = jnp.where(keep, a * b_sh + b, b)
                a = jnp.where(keep, a * a_sh, a)
            a_ref[pl.ds(off, CH), :] = a
            b_ref[pl.ds(off, CH), :] = b
        return carry

    lax.fori_loop(0, S // CH, gates, 0)

    def scan(ft, rt, carry):
        hf, hr = carry
        of = pl.multiple_of(ft * SUB, SUB)
        orv = pl.multiple_of(rt * SUB, SUB)
        tf = af_ref[pl.ds(of, SUB), :] * hf + bf_ref[pl.ds(of, SUB), :]
        tr = ar_ref[pl.ds(orv, SUB), :] * hr + br_ref[pl.ds(orv, SUB), :]
        bf_ref[pl.ds(of, SUB), :] = tf
        br_ref[pl.ds(orv, SUB), :] = tr
        return (jnp.broadcast_to(tf[SUB - 1:SUB, :], (SUB, LW)), jnp.broadcast_to(tr[0:1, :], (SUB, LW)))

    zero = jnp.zeros((SUB, LW), F32)
    carry = lax.fori_loop(0, ntc, lambda i, c: scan(i, ntc - 1 - i, c), (zero, zero), unroll=8)
    lax.fori_loop(ntc, nt, lambda i, c: scan(i, nt - 1 - (i - ntc), c), carry, unroll=8)

    def outp(c, carry):
        off = pl.multiple_of(c * CH, CH)
        src = pl.multiple_of(Lc + off, CH)
        hs = bf_ref[pl.ds(src, CH), :] + br_ref[pl.ds(src, CH), :]
        o_ref[0, pl.ds(off, CH), :] = (hs * _gelu_tanh(g_ref[0, pl.ds(src, CH), :])).astype(o_ref.dtype)
        return carry

    lax.fori_loop(0, N // CH, outp, 0)


def _lru(P1, col_x, col_g, conv_w, conv_b, gate_w, gate_b, lam, Lc):
    B, S, _ = P1.shape
    N = S - Lc
    nb = 2
    lw = nb * LRU_BS
    return pl.pallas_call(
        functools.partial(_lru_kernel, Lc=Lc, N=N), grid=(B, LRU_BLOCKS // nb),
        in_specs=[pl.BlockSpec((1, S, lw), lambda b, j: (b, 0, col_x // lw + j)),
                  pl.BlockSpec((1, S, lw), lambda b, j: (b, 0, col_g // lw + j)),
                  pl.BlockSpec((4, lw), lambda b, j: (0, j)),
                  pl.BlockSpec((1, lw), lambda b, j: (0, j)),
                  pl.BlockSpec((2, 2, nb, LRU_BS, LRU_BS), lambda b, j: (0, 0, j, 0, 0)),
                  pl.BlockSpec((2, 2, lw), lambda b, j: (0, 0, j)),
                  pl.BlockSpec((2, lw), lambda b, j: (0, j))],
        out_specs=pl.BlockSpec((1, N, lw), lambda b, j: (b, 0, j)),
        out_shape=jax.ShapeDtypeStruct((B, N, LRU_WIDTH), BF16),
        scratch_shapes=[pltpu.VMEM((S, lw), F32) for _ in range(4)],
        compiler_params=_cp("parallel", "parallel"), name="lru",
    )(P1, P1, conv_w, conv_b.reshape(1, LRU_WIDTH), gate_w, gate_b, lam)


ROUTE_BISECT = 2048


def _cumsum_lanes(x):
    n = x.shape[-1]
    lane = lax.broadcasted_iota(jnp.int32, x.shape, 1)
    s = 1
    while s < n:
        x = x + jnp.where(lane >= s, pltpu.roll(x, s, 1), 0.0)
        s *= 2
    return x


def _route_kernel(lt_ref, idx_ref, g_ref, cm_ref, af_ref, *, segs, S):
    E = lt_ref.shape[1]
    b = pl.program_id(0)
    lt = lt_ref[0]
    ex = jnp.exp(lt - jnp.max(lt, axis=0, keepdims=True))
    aff = ex / jnp.sum(ex, axis=0, keepdims=True)
    lane_e = lax.broadcasted_iota(jnp.int32, (1, LANE), 1)
    slot0 = 0
    for lo, n, cap in segs:
        a = aff[:, lo:lo + n]

        def bisect(carry):
            lo_v, hi_v, c_lo, it, _ = carry
            mid = lo_v + 0.5 * (hi_v - lo_v)
            c = jnp.sum(jnp.where(a >= mid, 1.0, 0.0), axis=1, keepdims=True)
            ge = c >= cap
            lo_v = jnp.where(ge, mid, lo_v)
            hi_v = jnp.where(ge, hi_v, mid)
            c_lo = jnp.where(ge, c, c_lo)
            mid = lo_v + 0.5 * (hi_v - lo_v)
            done = (c_lo == cap) | (mid <= lo_v) | (mid >= hi_v)
            return lo_v, hi_v, c_lo, it + 1, jnp.min(jnp.where(done, 1, 0))

        lo_v, hi_v, _, _, _ = lax.while_loop(
            lambda carry: (carry[4] == 0) & (carry[3] < ROUTE_BISECT), bisect,
            (jnp.zeros((E, 1), F32), jnp.full((E, 1), 2.0, F32), jnp.full((E, 1), float(n), F32),
             jnp.int32(0), jnp.int32(0)))
        top = jnp.where(a >= hi_v, 1.0, 0.0)
        tie = jnp.where((a >= lo_v) & (a < hi_v), 1.0, 0.0)
        need = cap - jnp.sum(top, axis=1, keepdims=True)
        tie_rank = _cumsum_lanes(tie) - tie
        sel = top + tie * jnp.where(tie_rank < need, 1.0, 0.0)
        cm_ref[:, 0:n] = _cumsum_lanes(sel) * sel
        af_ref[:, 0:n] = a
        tok = (lax.broadcasted_iota(jnp.int32, (1, n), 1) + (lo + b * S)).astype(F32)
        jcol = (lax.broadcasted_iota(jnp.int32, (cap, 1), 0) + 1).astype(F32)

        def per_expert(e, carry):
            oi, og = carry
            ce = cm_ref[pl.ds(e, 1), 0:n]
            ae = af_ref[pl.ds(e, 1), 0:n]
            acc_i = jnp.zeros((cap, LANE), F32)
            acc_g = jnp.zeros((cap, LANE), F32)
            for k in range(n // LANE):
                sl = slice(k * LANE, (k + 1) * LANE)
                hit = ce[:, sl] == jcol
                acc_i = acc_i + jnp.where(hit, tok[:, sl], 0.0)
                acc_g = acc_g + jnp.where(hit, ae[:, sl], 0.0)
            mine = lane_e == e
            return (jnp.where(mine, jnp.sum(acc_i, axis=1, keepdims=True), oi),
                    jnp.where(mine, jnp.sum(acc_g, axis=1, keepdims=True), og))

        zero = jnp.zeros((cap, LANE), F32)
        oi, og = lax.fori_loop(0, E, per_expert, (zero, zero))
        idx_ref[0, slot0:slot0 + cap, :] = oi
        g_ref[0, slot0:slot0 + cap, :] = og
        slot0 += cap


def _route(logits_t, segs):
    B, E, S = logits_t.shape
    slots = sum(cap for _, _, cap in segs)
    nmax = max(n for _, n, _ in segs)
    out = jax.ShapeDtypeStruct((B, slots, LANE), F32)
    blk = pl.BlockSpec((1, slots, LANE), lambda b: (b, 0, 0))
    return pl.pallas_call(
        functools.partial(_route_kernel, segs=segs, S=S), grid=(B,),
        in_specs=[pl.BlockSpec((1, E, S), lambda b: (b, 0, 0))],
        out_specs=[blk, blk], out_shape=[out, out],
        scratch_shapes=[pltpu.VMEM((E, nmax), F32), pltpu.VMEM((E, nmax), F32)],
        compiler_params=_cp("parallel"), name="route",
    )(logits_t)


def _ffn_kernel(ids_ref, h_hbm, w1_ref, w3_ref, w2_ref, g_ref, zero_hbm, out_hbm,
                xf_ref, xb_ref, acc_ref, y_ref, xsem, asem, ssem, *, rm):
    del zero_hbm
    step = pl.program_id(0) * pl.num_programs(1) + pl.program_id(1)
    nsteps = pl.num_programs(0) * pl.num_programs(1)
    f = pl.program_id(2)
    nf = pl.num_programs(2)
    f_acc = jnp.minimum(1, nf - 1)

    def rows(src, dst, sem, base, scatter):
        def issue(r, carry):
            tok = ids_ref[base + r]
            if scatter:
                pltpu.make_async_copy(src.at[pl.ds(r, 1), :], dst.at[pl.ds(tok, 1), :], sem).start()
            else:
                pltpu.make_async_copy(src.at[pl.ds(tok, 1), :], dst.at[pl.ds(r, 1), :], sem).start()
            return carry
        lax.fori_loop(0, rm, issue, 0, unroll=8)

    def wait_all(buf, sem):
        pltpu.make_async_copy(buf, buf, sem).wait()

    @pl.when(f == 0)
    def _():
        @pl.when(step == 0)
        def _():
            rows(h_hbm, xf_ref, xsem, 0, False)
        wait_all(xf_ref, xsem)
        xb_ref[...] = xf_ref[...].astype(BF16)

        @pl.when(step + 1 < nsteps)
        def _():
            rows(h_hbm, xf_ref, xsem, (step + 1) * rm, False)

    @pl.when(f == f_acc)
    def _():
        @pl.when(step > 0)
        def _():
            wait_all(acc_ref, ssem)
        rows(out_hbm, acc_ref, asem, step * rm, False)

    x = xb_ref[...]
    u = _dot(x, w1_ref[0, 0].astype(BF16))
    v = _dot(x, w3_ref[0, 0].astype(BF16))
    a = (u * _sigmoid(u) * v).astype(BF16)
    y = _dot(a, w2_ref[0, 0].astype(BF16))

    @pl.when(f == 0)
    def _():
        y_ref[...] = y

    @pl.when(f > 0)
    def _():
        y_ref[...] += y

    @pl.when(f == nf - 1)
    def _():
        wait_all(acc_ref, asem)
        acc_ref[...] = acc_ref[...] + y_ref[...] * g_ref[0]
        rows(acc_ref, out_hbm, ssem, step * rm, True)

        @pl.when(step == nsteps - 1)
        def _():
            wait_all(acc_ref, ssem)


def _ffn(hf, ids, gs, w1, w3, w2, l):
    T, D = hf.shape
    E, R, _ = gs.shape
    FF = w1.shape[-1]
    rm = R // 2
    tf = _pick(FF, (256, 128))
    grid_spec = pltpu.PrefetchScalarGridSpec(
        num_scalar_prefetch=1, grid=(E, 2, FF // tf),
        in_specs=[pl.BlockSpec(memory_space=pl.ANY),
                  pl.BlockSpec((1, 1, D, tf), lambda e, m, f, ids: (l, e, 0, f)),
                  pl.BlockSpec((1, 1, D, tf), lambda e, m, f, ids: (l, e, 0, f)),
                  pl.BlockSpec((1, 1, tf, D), lambda e, m, f, ids: (l, e, f, 0)),
                  pl.BlockSpec((1, rm, 1), lambda e, m, f, ids: (e, m, 0)),
                  pl.BlockSpec(memory_space=pl.ANY)],
        out_specs=pl.BlockSpec(memory_space=pl.ANY),
        scratch_shapes=[pltpu.VMEM((rm, D), F32), pltpu.VMEM((rm, D), BF16),
                        pltpu.VMEM((rm, D), F32), pltpu.VMEM((rm, D), F32),
                        pltpu.SemaphoreType.DMA, pltpu.SemaphoreType.DMA, pltpu.SemaphoreType.DMA])
    return pl.pallas_call(
        functools.partial(_ffn_kernel, rm=rm), grid_spec=grid_spec,
        out_shape=jax.ShapeDtypeStruct((T, D), F32),
        input_output_aliases={6: 0},
        compiler_params=_cp("arbitrary", "arbitrary", "arbitrary"), name="ffn",
    )(ids, hf, w1, w3, w2, gs, jnp.zeros((T, D), F32))


def _moe(h, router_bf, w1, w3, w2, l, Lc):
    B, S, D = h.shape
    E = N_EXPERTS
    hf = h.reshape(B * S, D)
    logits = _matmul(hf, router_bf, F32)[:, :E].reshape(B, S, E)
    segs = tuple((lo, n, EC_FACTOR * n // E) for lo, n in ((Lc, S - Lc), (0, Lc)) if n)
    idx, g = _route(jnp.swapaxes(logits, 1, 2), segs)
    ids = jnp.transpose(idx[:, :, :E], (2, 0, 1)).astype(jnp.int32).reshape(-1)
    gs = jnp.transpose(g[:, :, :E], (2, 0, 1)).reshape(E, -1, 1)
    return _ffn(hf, ids, gs, w1, w3, w2, l).reshape(B, S, D)


def kernel(x, c, ctx, c_ctx, ada_w, ada_b, norm_mix_pre, norm_mix_post, norm_ffn_pre, norm_ffn_post,
           ab_w_in, ab_w_out, mlstm_gate_b, mlstm_norm, na_rpb,
           cd_w_in, cd_w_out, gqa_sink, lru_conv_w, lru_conv_b, lru_gate_w, lru_gate_b, lru_lambda,
           moe_router, moe_w1, moe_w3, moe_w2):
    B, N, D = x.shape
    Lc = ctx.shape[1]
    S = Lc + N
    assert Lc == TS and N % TS == 0 and B < SUB
    ctx_row = B
    xs = jnp.concatenate([ctx, x], axis=1)
    cond = jnp.concatenate([c, c_ctx[None], jnp.zeros((SUB - B - 1, D), F32)], axis=0)
    mod = _ada(cond, ada_w, ada_b).reshape(-1, 1, D)
    router = jnp.pad(moe_router, ((0, 0), (0, 0), (0, LANE - N_EXPERTS))).astype(BF16)

    h = _prenorm(xs, norm_mix_pre[0], mod, 0, ctx_row, True)
    hf = h.reshape(B * S, D)
    w_in = ab_w_in[0]
    o0 = 3 * MLSTM_QK
    n0 = o0 + MLSTM_V + MLSTM_GATES
    w_qkv = jnp.concatenate([w_in[:, :o0], w_in[:, n0:]], axis=1).astype(BF16)
    w_og = jnp.pad(w_in[:, o0:n0], ((0, 0), (0, LANE - MLSTM_GATES))).astype(BF16)
    P = _matmul(hf, w_qkv, BF16).reshape(B, S, -1)
    OG = _matmul(hf, w_og, F32).reshape(B, S, -1)
    gcol = OG[:, :, MLSTM_V:MLSTM_V + MLSTM_GATES]
    Hd = _mlstm(P, gcol, jnp.swapaxes(gcol, 1, 2), mlstm_gate_b[0])
    ML = _mlstm_out(Hd, OG, mlstm_norm[0])
    NL = _na(P, _na_bias(na_rpb[0]), Lc, o0)
    Y = _matmul2(ML.reshape(B * S, -1), NL.reshape(B * S, -1), ab_w_out[0].astype(BF16)).reshape(B, S, D)
    xs, h = _resid(xs, Y, norm_mix_post[0], mod, 0, 2, ctx_row, 0, (norm_ffn_pre[0], 0, 3, 4, F32))
    Y = _moe(h, router[0], moe_w1, moe_w3, moe_w2, 0, Lc)
    xs, h = _resid(xs, Y, norm_ffn_post[0], mod, 0, 5, ctx_row, 0, (norm_mix_pre[1], 1, 0, 1, BF16))

    P1 = _matmul(h.reshape(B * S, D), cd_w_in[0].astype(BF16), F32).reshape(B, S, -1)
    cos, sin = _rope_tables(Lc, N)
    AL = _gqa(P1, cos, sin, gqa_sink[0], Lc)
    RL = _lru(P1, GQA_Q + 2 * GQA_KV, GQA_Q + 2 * GQA_KV + LRU_WIDTH,
              lru_conv_w[0], lru_conv_b[0], lru_gate_w[0], lru_gate_b[0], lru_lambda[0], Lc)
    Y = _matmul2(AL.reshape(B * N, -1), RL.reshape(B * N, -1), cd_w_out[0].astype(BF16)).reshape(B, N, D)
    xl, h = _resid(xs, Y, norm_mix_post[1], mod, 1, 2, None, Lc // TS, (norm_ffn_pre[1], 1, 3, 4, F32))
    Y = _moe(h, router[1], moe_w1, moe_w3, moe_w2, 1, 0)
    xl, _ = _resid(xl, Y, norm_ffn_post[1], mod, 1, 5, None, 0, None)
    return xl
```

```python
import functools

import numpy as np
import jax
import jax.numpy as jnp
from jax import lax
from jax.experimental import pallas as pl
from jax.experimental.pallas import tpu as pltpu

F32 = jnp.float32
BF16 = jnp.bfloat16
EPS = 1e-6
NEG = -1e30

GRID_W = 64
MLSTM_HEADS = 4
MLSTM_DK = 256
MLSTM_DV = 256
MLSTM_QK = MLSTM_HEADS * MLSTM_DK
MLSTM_V = MLSTM_HEADS * MLSTM_DV
MLSTM_GATES = 2 * 2 * MLSTM_HEADS
NA_HEADS = 8
NA_DIM = 128
NA_WIDTH = NA_HEADS * NA_DIM
NA_KH = 8
NA_KW = 16
GQA_HEADS = 16
GQA_KV_HEADS = 4
GQA_DIM = 64
GQA_Q = GQA_HEADS * GQA_DIM
GQA_KV = GQA_KV_HEADS * GQA_DIM
GQA_WINDOW = 128
ROPE_THETA = 10000.0
LRU_WIDTH = 1024
LRU_BLOCKS = 8
LRU_BS = LRU_WIDTH // LRU_BLOCKS
LRU_C = 8.0
N_EXPERTS = 16
EC_FACTOR = 2

TS = 256
LANE = 128
SUB = 8
VMEM_LIMIT = 56 * 1024 * 1024


def _cp(*sem):
    return pltpu.CompilerParams(dimension_semantics=sem, vmem_limit_bytes=VMEM_LIMIT)


def _pick(n, prefs):
    for p in prefs:
        if n % p == 0:
            return p
    return n


def _sigmoid(x):
    return 1.0 / (1.0 + jnp.exp(-x))


def _dot(a, b):
    return jnp.dot(a, b, preferred_element_type=F32)


def _dot_nt(a, b):
    return lax.dot_general(a, b, (((1,), (1,)), ((), ())), preferred_element_type=F32)


def _dot_tn(a, b):
    return lax.dot_general(a, b, (((0,), (0,)), ((), ())), preferred_element_type=F32)


def _rms(x, w):
    return x * lax.rsqrt(jnp.mean(x * x, axis=-1, keepdims=True) + EPS) * w


def _ada_kernel(c_ref, w_ref, b_ref, o_ref):
    c = c_ref[...]
    a = (c * _sigmoid(c)).astype(BF16)
    o_ref[0] = _dot(a, w_ref[0].astype(BF16)) + b_ref[0]


def _ada(cond, ada_w, ada_b):
    L, D, D6 = ada_w.shape
    tn = _pick(D6, (1024, 512, 256, 128))
    return pl.pallas_call(
        _ada_kernel, grid=(L, D6 // tn),
        in_specs=[pl.BlockSpec((SUB, D), lambda l, j: (0, 0)),
                  pl.BlockSpec((1, D, tn), lambda l, j: (l, 0, j)),
                  pl.BlockSpec((1, 1, tn), lambda l, j: (l, 0, j))],
        out_specs=pl.BlockSpec((1, SUB, tn), lambda l, j: (l, 0, j)),
        out_shape=jax.ShapeDtypeStruct((L, SUB, D6), F32),
        compiler_params=_cp("parallel", "parallel"), name="ada",
    )(cond, ada_w, ada_b.reshape(L, 1, D6))


def _mod_map(l, k, ctx_row, has_ctx):
    if has_ctx:
        return lambda b, s: ((l * SUB + jnp.where(s == 0, ctx_row, b)) * 6 + k, 0, 0)
    return lambda b, s: ((l * SUB + b) * 6 + k, 0, 0)


def _prenorm_kernel(x_ref, w_ref, sh_ref, sc_ref, o_ref):
    y = _rms(x_ref[0], w_ref[...])
    o_ref[0] = (y * (1.0 + sc_ref[0]) + sh_ref[0]).astype(o_ref.dtype)


def _prenorm(x, w, mod, l, ctx_row, has_ctx):
    B, S, D = x.shape
    vec = lambda k: pl.BlockSpec((1, 1, D), _mod_map(l, k, ctx_row, has_ctx))
    return pl.pallas_call(
        _prenorm_kernel, grid=(B, S // TS),
        in_specs=[pl.BlockSpec((1, TS, D), lambda b, s: (b, s, 0)),
                  pl.BlockSpec((1, D), lambda b, s: (0, 0)), vec(0), vec(1)],
        out_specs=pl.BlockSpec((1, TS, D), lambda b, s: (b, s, 0)),
        out_shape=jax.ShapeDtypeStruct((B, S, D), BF16),
        compiler_params=_cp("parallel", "parallel"), name="prenorm",
    )(x, w.reshape(1, D), mod, mod)


def _resid_kernel(nxt_mode, x_ref, y_ref, wpost_ref, g_ref, *rest):
    xn = x_ref[0] + g_ref[0] * _rms(y_ref[0], wpost_ref[...])
    if nxt_mode is None:
        (xo_ref,) = rest
    elif nxt_mode == "mixer":
        wpre_ref, sh_ref, sc_ref, xo_ref, ho_ref = rest
    else:
        wpre_ref, sh_ref, sc_ref, router_ref, xo_ref, ho_ref, lo_ref = rest
    xo_ref[0] = xn
    if nxt_mode is not None:
        h = _rms(xn, wpre_ref[...]) * (1.0 + sc_ref[0]) + sh_ref[0]
        ho_ref[0] = h.astype(ho_ref.dtype)
        if nxt_mode == "experts":
            lo_ref[0] = _dot(h.astype(BF16), router_ref[...])


def _resid(x, y, wpost, mod, l, kg, ctx_row, x_off, nxt):
    B, Sy, D = y.shape
    has_ctx = x_off == 0 and x.shape[1] == Sy and ctx_row is not None
    vec = lambda ll, k: pl.BlockSpec((1, 1, D), _mod_map(ll, k, ctx_row, has_ctx))
    row = pl.BlockSpec((1, D), lambda b, s: (0, 0))
    tile = pl.BlockSpec((1, TS, D), lambda b, s: (b, s, 0))
    in_specs = [pl.BlockSpec((1, TS, D), lambda b, s: (b, s + x_off, 0)), tile, row, vec(l, kg)]
    args = [x, y, wpost.reshape(1, D), mod]
    out_specs = [tile]
    out_shape = [jax.ShapeDtypeStruct((B, Sy, D), F32)]
    mode = None
    if nxt is not None:
        wpre, ln, ksh, ksc, router = nxt
        in_specs += [row, vec(ln, ksh), vec(ln, ksc)]
        args += [wpre.reshape(1, D), mod, mod]
        out_specs.append(tile)
        if router is None:
            mode = "mixer"
            out_shape.append(jax.ShapeDtypeStruct((B, Sy, D), BF16))
        else:
            mode = "experts"
            in_specs.append(pl.BlockSpec((D, LANE), lambda b, s: (0, 0)))
            args.append(router)
            out_specs.append(pl.BlockSpec((1, TS, LANE), lambda b, s: (b, s, 0)))
            out_shape += [jax.ShapeDtypeStruct((B, Sy, D), F32), jax.ShapeDtypeStruct((B, Sy, LANE), F32)]
    out = pl.pallas_call(
        functools.partial(_resid_kernel, mode), grid=(B, Sy // TS),
        in_specs=in_specs, out_specs=out_specs, out_shape=out_shape,
        compiler_params=_cp("parallel", "parallel"), name="resid",
    )(*args)
    return out


def _mm_kernel(x_ref, w_ref, o_ref):
    o_ref[...] = _dot(x_ref[...].astype(BF16), w_ref[...]).astype(o_ref.dtype)


def _matmul(x, w, out_dtype):
    M, K = x.shape
    N = w.shape[1]
    tm = _pick(M, (1024, 512, 256))
    tn = _pick(N, (512, 384, 256, 128))
    return pl.pallas_call(
        _mm_kernel, grid=(M // tm, N // tn),
        in_specs=[pl.BlockSpec((tm, K), lambda i, j: (i, 0)),
                  pl.BlockSpec((K, tn), lambda i, j: (0, j))],
        out_specs=pl.BlockSpec((tm, tn), lambda i, j: (i, j)),
        out_shape=jax.ShapeDtypeStruct((M, N), out_dtype),
        compiler_params=_cp("parallel", "parallel"), name="matmul",
    )(x, w)


def _mm2_kernel(x1_ref, x2_ref, w1_ref, w2_ref, o_ref):
    o_ref[...] = _dot(x1_ref[...], w1_ref[...]) + _dot(x2_ref[...], w2_ref[...])


def _matmul2(x1, x2, w):
    M, K1 = x1.shape
    K2 = x2.shape[1]
    assert K1 == K2 and w.shape[0] == K1 + K2
    N = w.shape[1]
    tm = _pick(M, (1024, 512, 256))
    tn = _pick(N, (512, 256, 128))
    return pl.pallas_call(
        _mm2_kernel, grid=(M // tm, N // tn),
        in_specs=[pl.BlockSpec((tm, K1), lambda i, j: (i, 0)),
                  pl.BlockSpec((tm, K2), lambda i, j: (i, 0)),
                  pl.BlockSpec((K1, tn), lambda i, j: (0, j)),
                  pl.BlockSpec((K2, tn), lambda i, j: (1, j))],
        out_specs=pl.BlockSpec((tm, tn), lambda i, j: (i, j)),
        out_shape=jax.ShapeDtypeStruct((M, N), F32),
        compiler_params=_cp("parallel", "parallel"), name="matmul2",
    )(x1, x2, w, w)


def _log_sigmoid(x):
    return jnp.minimum(x, 0.0) - jnp.log1p(jnp.exp(-jnp.abs(x)))


def _mlstm_kernel(q_ref, k_ref, v_ref, gc_ref, gr_ref, bc_ref, br_ref, o_ref, C_ref, n_ref, m_ref):
    H, dk, dv, L = MLSTM_HEADS, MLSTM_DK, MLSTM_DV, TS
    d = pl.program_id(1)
    t = pl.program_id(2)

    @pl.when(t == 0)
    def _():
        C_ref[...] = jnp.zeros_like(C_ref)
        n_ref[...] = jnp.zeros_like(n_ref)
        m_ref[...] = jnp.zeros_like(m_ref)

    fwd = d == 0
    ri = lax.broadcasted_iota(jnp.int32, (L, L), 0)
    ci = lax.broadcasted_iota(jnp.int32, (L, L), 1)
    diff = (ci - ri) * (1 - 2 * d)
    causal = diff <= 0
    causal_f = jnp.where(causal, 1.0, 0.0)
    causal_t = jnp.where(diff >= 0, 1.0, 0.0)
    gc = gc_ref[0] + bc_ref[...]
    gr = gr_ref[0] + br_ref[...]
    lfc = _log_sigmoid(gc)
    lfr = _log_sigmoid(gr)
    hi = lax.Precision.HIGHEST
    bcol_all = jnp.dot(causal_f, lfc, precision=hi, preferred_element_type=F32)
    brow_all = jnp.dot(lfr, causal_t, precision=hi, preferred_element_type=F32)
    tot_all = jnp.sum(lfr, axis=-1, keepdims=True)
    sel = lambda a, b: jnp.where(fwd, a, b)
    scale = dk ** -0.5
    for h in range(H):
        ic = sel(gc[:, h:h + 1], gc[:, 2 * H + h:2 * H + h + 1])
        ir = sel(gr[h:h + 1], gr[2 * H + h:2 * H + h + 1])
        bc = sel(bcol_all[:, H + h:H + h + 1], bcol_all[:, 3 * H + h:3 * H + h + 1])
        br = sel(brow_all[H + h:H + h + 1], brow_all[3 * H + h:3 * H + h + 1])
        tot = sel(tot_all[H + h:H + h + 1], tot_all[3 * H + h:3 * H + h + 1])
        m = m_ref[h]
        dmat = jnp.where(causal, bc - br + ir, NEG)
        inter = bc + m
        mj = jnp.maximum(inter, jnp.max(dmat, axis=-1, keepdims=True))
        p = jnp.exp(dmat - mj)
        q = q_ref[0, :, h * dk:(h + 1) * dk]
        k = k_ref[0, :, h * dk:(h + 1) * dk]
        v = v_ref[0, :, h * dv:(h + 1) * dv]
        s = _dot_nt(q, k) * scale * p
        w_inter = jnp.exp(inter - mj)
        num = w_inter * _dot(q, C_ref[h].astype(BF16)) + _dot(s.astype(BF16), v)
        qn = jnp.sum(q.astype(F32) * n_ref[h], axis=-1, keepdims=True)
        den = w_inter * qn + jnp.sum(s, axis=-1, keepdims=True)
        o_ref[0, 0, :, h * dv:(h + 1) * dv] = num / jnp.maximum(jnp.abs(den), jnp.exp(-mj))
        g_c = tot - bc + ic
        g_r = tot - br + ir
        m_new = jnp.maximum(tot + m, jnp.max(g_r, axis=-1, keepdims=True))
        decay = jnp.exp(tot + m - m_new)
        wk = jnp.exp(g_c - m_new) * (k.astype(F32) * scale)
        C_ref[h] = decay * C_ref[h] + _dot_tn(wk.astype(BF16), v)
        n_ref[h] = decay * n_ref[h] + jnp.sum(wk, axis=0, keepdims=True)
        m_ref[h] = m_new


def _mlstm(P, gcol, grow, gate_b):
    B, S, _ = P.shape
    nc = S // TS
    chunk = lambda d, t: jnp.where(d == 0, t, jnp.where(t == 0, 0, nc - t))
    blk = lambda c: pl.BlockSpec((1, TS, MLSTM_QK), lambda b, d, t: (b, chunk(d, t), c))
    G = MLSTM_GATES
    return pl.pallas_call(
        _mlstm_kernel, grid=(B, 2, nc),
        in_specs=[blk(0), blk(1), blk(2),
                  pl.BlockSpec((1, TS, G), lambda b, d, t: (b, chunk(d, t), 0)),
                  pl.BlockSpec((1, G, TS), lambda b, d, t: (b, 0, chunk(d, t))),
                  pl.BlockSpec((1, G), lambda b, d, t: (0, 0)),
                  pl.BlockSpec((G, 1), lambda b, d, t: (0, 0))],
        out_specs=pl.BlockSpec((1, 1, TS, MLSTM_V), lambda b, d, t: (d, b, chunk(d, t), 0)),
        out_shape=jax.ShapeDtypeStruct((2, B, S, MLSTM_V), F32),
        scratch_shapes=[pltpu.VMEM((MLSTM_HEADS, MLSTM_DK, MLSTM_DV), F32),
                        pltpu.VMEM((MLSTM_HEADS, 1, MLSTM_DK), F32),
                        pltpu.VMEM((MLSTM_HEADS, 1, 1), F32)],
        compiler_params=_cp("parallel", "arbitrary", "arbitrary"), name="mlstm",
    )(P, P, P, gcol, grow, gate_b.reshape(1, G), gate_b.reshape(G, 1))


def _mlstm_out_kernel(hf_ref, hr_ref, o_ref, gain_ref, out_ref):
    dv = MLSTM_DV
    for h in range(MLSTM_HEADS):
        sl = slice(h * dv, (h + 1) * dv)
        x = hf_ref[0, 0, :, sl] + hr_ref[0, 0, :, sl]
        out_ref[0, :, sl] = (_rms(x, gain_ref[:, sl]) * _sigmoid(o_ref[0, :, sl])).astype(out_ref.dtype)


def _mlstm_out(Hd, OG, gain):
    _, B, S, V = Hd.shape
    return pl.pallas_call(
        _mlstm_out_kernel, grid=(B, S // TS),
        in_specs=[pl.BlockSpec((1, 1, TS, V), lambda b, s: (0, b, s, 0)),
                  pl.BlockSpec((1, 1, TS, V), lambda b, s: (1, b, s, 0)),
                  pl.BlockSpec((1, TS, V), lambda b, s: (b, s, 0)),
                  pl.BlockSpec((1, V), lambda b, s: (0, 0))],
        out_specs=pl.BlockSpec((1, TS, V), lambda b, s: (b, s, 0)),
        out_shape=jax.ShapeDtypeStruct((B, S, V), BF16),
        compiler_params=_cp("parallel", "parallel"), name="mlstm_out",
    )(Hd, Hd, OG, gain.reshape(1, V))


NA_QR = 4
NA_KR = NA_KH + 2 * NA_QR - 4


def _na_bias(rpb):
    a = np.arange(NA_QR)
    kr = np.arange(NA_KR)
    qrel = NA_QR * np.arange(3)[:, None] + a[None, :]
    r0rel = np.stack([0 * a, a, NA_KR - NA_KH + 0 * a])
    valid_r = (kr[None, None] >= r0rel[..., None]) & (kr[None, None] < r0rel[..., None] + NA_KH)
    drow = np.clip(kr[None, None] - qrel[..., None] + (NA_KH - 1), 0, 2 * NA_KH - 2)
    w = np.arange(GRID_W)[:, None]
    cc = np.arange(GRID_W)[None, :]
    col_start = np.clip(w - NA_KW // 2, 0, GRID_W - NA_KW)
    valid_c = (cc >= col_start) & (cc < col_start + NA_KW)
    dcol = np.clip(cc - w + (NA_KW - 1), 0, 2 * NA_KW - 2)
    pick_r = jnp.asarray(np.eye(2 * NA_KH - 1, dtype=np.float32)[drow])
    pick_c = jnp.asarray(np.eye(2 * NA_KW - 1, dtype=np.float32)[dcol])
    t = jnp.einsum("cakr,hrs,wzs->chawkz", pick_r, rpb.astype(F32), pick_c,
                   precision=lax.Precision.HIGHEST)
    valid = valid_r[:, None, :, None, :, None] & valid_c[None, None, None, :, None, :]
    t = jnp.where(valid, t, NEG)
    return t.reshape(3, NA_HEADS, NA_QR * GRID_W, NA_KR * GRID_W)


def _na_kernel(q_ref, k_ref, v_ref, bias_ref, o_ref, *, Lc, rows_n):
    W, d = GRID_W, NA_DIM
    scale = d ** -0.5
    kc = k_ref[0, 0:Lc, :]
    vc = v_ref[0, 0:Lc, :]
    sc = _dot_nt(q_ref[0, 0:Lc, :], kc) * scale
    pc = jnp.exp(sc - jnp.max(sc, axis=-1, keepdims=True))
    oc = _dot(pc.astype(BF16), vc) / jnp.sum(pc, axis=-1, keepdims=True)
    o_ref[0, 0:Lc, :] = oc.astype(o_ref.dtype)

    def group(gi, carry):
        r = gi * NA_QR
        kr0 = jnp.clip(r - NA_KH // 2, 0, rows_n - NA_KR)
        qoff = pl.multiple_of(Lc + r * W, NA_QR * W)
        koff = pl.multiple_of(Lc + kr0 * W, W)
        qg = q_ref[0, pl.ds(qoff, NA_QR * W), :]
        kw = k_ref[0, pl.ds(koff, NA_KR * W), :]
        vw = v_ref[0, pl.ds(koff, NA_KR * W), :]
        sw = _dot_nt(qg, kw) * scale + bias_ref[(r - kr0) // NA_QR, 0]
        sx = _dot_nt(qg, kc) * scale
        m = jnp.maximum(jnp.max(sw, axis=-1, keepdims=True), jnp.max(sx, axis=-1, keepdims=True))
        pw = jnp.exp(sw - m)
        px = jnp.exp(sx - m)
        l = jnp.sum(pw, axis=-1, keepdims=True) + jnp.sum(px, axis=-1, keepdims=True)
        o = (_dot(pw.astype(BF16), vw) + _dot(px.astype(BF16), vc)) / l
        o_ref[0, pl.ds(qoff, NA_QR * W), :] = o.astype(o_ref.dtype)
        return carry

    lax.fori_loop(0, rows_n // NA_QR, group, 0, unroll=2)


def _na(P, bias, Lc, col0):
    B, S, _ = P.shape
    rows_n = (S - Lc) // GRID_W
    assert rows_n % NA_QR == 0 and rows_n >= NA_KR
    c0 = col0 // NA_DIM
    blk = lambda c: pl.BlockSpec((1, S, NA_DIM), lambda b, h: (b, 0, c0 + c * NA_HEADS + h))
    return pl.pallas_call(
        functools.partial(_na_kernel, Lc=Lc, rows_n=rows_n), grid=(B, NA_HEADS),
        in_specs=[blk(0), blk(1), blk(2),
                  pl.BlockSpec((3, 1, NA_QR * GRID_W, NA_KR * GRID_W), lambda b, h: (0, h, 0, 0))],
        out_specs=pl.BlockSpec((1, S, NA_DIM), lambda b, h: (b, 0, h)),
        out_shape=jax.ShapeDtypeStruct((B, S, NA_WIDTH), BF16),
        compiler_params=_cp("parallel", "parallel"), name="na",
    )(P, P, P, bias)


def _rope_tables(Lc, N):
    quarter = GQA_DIM // 4
    t = jnp.arange(N)
    inv = ROPE_THETA ** (-(jnp.arange(quarter, dtype=F32) / quarter))
    ang_r = (t // GRID_W).astype(F32)[:, None] * inv[None, :]
    ang_c = (t % GRID_W).astype(F32)[:, None] * inv[None, :]
    cos = jnp.concatenate([jnp.cos(ang_r)] * 2 + [jnp.cos(ang_c)] * 2, axis=-1)
    sin = jnp.concatenate([-jnp.sin(ang_r), jnp.sin(ang_r), -jnp.sin(ang_c), jnp.sin(ang_c)], axis=-1)
    cos = jnp.concatenate([jnp.ones((Lc, GQA_DIM), F32), cos], axis=0)
    sin = jnp.concatenate([jnp.zeros((Lc, GQA_DIM), F32), sin], axis=0)
    return jnp.tile(cos, (1, LANE // GQA_DIM)), jnp.tile(sin, (1, LANE // GQA_DIM))


def _rope_rot(x, cos, sin):
    quarter = GQA_DIM // 4
    lane = lax.broadcasted_iota(jnp.int32, x.shape, 1)
    first = (lane % (2 * quarter)) < quarter
    rot = jnp.where(first, pltpu.roll(x, LANE - quarter, 1), pltpu.roll(x, quarter, 1))
    return x * cos + rot * sin


def _gqa_kernel(sink_ref, q_ref, k_ref, v_ref, cos_ref, sin_ref, o_ref, kd_ref, vd_ref, *, Lc, N):
    W, dh, KV = GQA_WINDOW, GQA_DIM, GQA_KV_HEADS
    G = GQA_HEADS // KV
    S = Lc + N
    i = pl.program_id(1)
    scale = dh ** -0.5

    @pl.when(i == 0)
    def _():
        low_t = lax.broadcasted_iota(jnp.int32, (TS, LANE), 1) < dh
        zero = jnp.zeros((W, LANE), BF16)
        for kh in range(KV):
            for ref in (kd_ref, vd_ref):
                ref[kh, Lc:Lc + W, :] = zero
                ref[kh, Lc + W + N:Lc + 2 * W + N, :] = zero

        def build(c, carry):
            r0 = pl.multiple_of(c * TS, TS)
            dst = pl.multiple_of(r0 + jnp.where(c >= Lc // TS, W, 0), W)
            cs = cos_ref[pl.ds(r0, TS), :]
            sn = sin_ref[pl.ds(r0, TS), :]
            for hh in range(KV // 2):
                kr = _rope_rot(k_ref[0, pl.ds(r0, TS), hh * LANE:(hh + 1) * LANE], cs, sn)
                vx = v_ref[0, pl.ds(r0, TS), hh * LANE:(hh + 1) * LANE]
                for ref, a in ((kd_ref, kr), (vd_ref, vx)):
                    sw = pltpu.roll(a, dh, 1)
                    ref[2 * hh, pl.ds(dst, TS), :] = jnp.where(low_t, a, sw).astype(BF16)
                    ref[2 * hh + 1, pl.ds(dst, TS), :] = jnp.where(low_t, sw, a).astype(BF16)
            return carry

        lax.fori_loop(0, S // TS, build, 0)

    span = 3 * W
    off = pl.multiple_of(Lc + i * W, W)
    cq = cos_ref[pl.ds(off, W), :]
    sq = sin_ref[pl.ds(off, W), :]
    ii = lax.broadcasted_iota(jnp.int32, (G * W, span), 0) % W
    jj = lax.broadcasted_iota(jnp.int32, (G * W, span), 1)
    kpos = i * W - W + jj
    valid = (kpos >= 0) & (kpos < N) & (jnp.abs(ii - (jj - W)) <= W)
    rg = lax.broadcasted_iota(jnp.int32, (G * W, 1), 0) // W
    low = lax.broadcasted_iota(jnp.int32, (W, LANE), 1) < dh
    for kh in range(KV):
        parts = []
        for slab in range(G // 2):
            j = kh * (G // 2) + slab
            qr = _rope_rot(q_ref[0, :, j * LANE:(j + 1) * LANE] * scale, cq, sq)
            parts += [jnp.where(low, qr, 0.0).astype(BF16), jnp.where(low, 0.0, qr).astype(BF16)]
        q4 = jnp.concatenate(parts, axis=0)
        kw = kd_ref[kh, pl.ds(off, span), :]
        vw = vd_ref[kh, pl.ds(off, span), :]
        kc = kd_ref[kh, 0:Lc, :]
        vc = vd_ref[kh, 0:Lc, :]
        sw = jnp.where(valid, _dot_nt(q4, kw), NEG)
        sx = _dot_nt(q4, kc)
        sk = sink_ref[G * kh + G - 1]
        for g in range(G - 2, -1, -1):
            sk = jnp.where(rg == g, sink_ref[G * kh + g], sk)
        m = jnp.maximum(jnp.maximum(jnp.max(sw, axis=-1, keepdims=True),
                                    jnp.max(sx, axis=-1, keepdims=True)), sk)
        pw = jnp.exp(sw - m)
        px = jnp.exp(sx - m)
        l = jnp.sum(pw, axis=-1, keepdims=True) + jnp.sum(px, axis=-1, keepdims=True) + jnp.exp(sk - m)
        o = (_dot(pw.astype(BF16), vw) + _dot(px.astype(BF16), vc)) / l
        for slab in range(G // 2):
            j = kh * (G // 2) + slab
            o_ref[0, :, j * LANE:(j + 1) * LANE] = jnp.where(
                low, o[2 * slab * W:(2 * slab + 1) * W], o[(2 * slab + 1) * W:(2 * slab + 2) * W]
            ).astype(o_ref.dtype)


def _gqa(P1, cos, sin, sink, Lc):
    B, S, _ = P1.shape
    N = S - Lc
    W = GQA_WINDOW
    rows = Lc + 2 * W + N
    table = pl.BlockSpec((S, LANE), lambda b, i: (0, 0))
    return pl.pallas_call(
        functools.partial(_gqa_kernel, Lc=Lc, N=N), grid=(B, N // W),
        in_specs=[pl.BlockSpec(memory_space=pltpu.SMEM),
                  pl.BlockSpec((1, W, GQA_Q), lambda b, i: (b, i + Lc // W, 0)),
                  pl.BlockSpec((1, S, GQA_KV), lambda b, i: (b, 0, GQA_Q // GQA_KV)),
                  pl.BlockSpec((1, S, GQA_KV), lambda b, i: (b, 0, GQA_Q // GQA_KV + 1)),
                  table, table],
        out_specs=pl.BlockSpec((1, W, GQA_Q), lambda b, i: (b, i, 0)),
        out_shape=jax.ShapeDtypeStruct((B, N, GQA_Q), BF16),
        scratch_shapes=[pltpu.VMEM((GQA_KV_HEADS, rows, LANE), BF16),
                        pltpu.VMEM((GQA_KV_HEADS, rows, LANE), BF16)],
        compiler_params=_cp("parallel", "arbitrary"), name="gqa",
    )(sink, P1, P1, P1, cos, sin)


def _gelu_tanh(x):
    return 0.5 * x * (1.0 + jnp.tanh(np.sqrt(2.0 / np.pi) * (x + 0.044715 * (x * x * x))))


def _lru_kernel(x_ref, g_ref, cw_ref, cb_ref, gw_ref, gb_ref, lam_ref, o_ref,
                af_ref, bf_ref, ar_ref, br_ref, *, Lc, N):
    S = Lc + N
    CH = 128
    LW = x_ref.shape[-1]
    nb = LW // LRU_BS
    nt = S // SUB
    ntc = Lc // SUB
    row = lax.broadcasted_iota(jnp.int32, (CH, LW), 0)
    crow = row % SUB
    first_row = row == 0
    last_row = row == CH - 1
    last_row2 = row == CH - 2
    sps = []
    for d in range(2):
        lam = lam_ref[d:d + 1, :]
        sps.append(jnp.maximum(-lam, 0.0) + jnp.log1p(jnp.exp(-jnp.abs(lam))))

    def gates(c, carry):
        off = pl.multiple_of(c * CH, CH)
        xc = x_ref[0, pl.ds(off, CH), :]
        prev = x_ref[0, pl.ds(pl.multiple_of(jnp.maximum(off - SUB, 0), SUB), SUB), :]
        nxt = x_ref[0, pl.ds(pl.multiple_of(jnp.minimum(off + CH, S - SUB), SUB), SUB), :]
        pz = jnp.where((c == 0) | (c == Lc // CH), 0.0, 1.0)
        nz = jnp.where((c == Lc // CH - 1) | (c == S // CH - 1), 0.0, 1.0)
        p7 = prev[SUB - 1:SUB, :] * pz
        n0 = nxt[0:1, :] * nz
        n1 = nxt[1:2, :] * nz
        xm1 = jnp.where(first_row, p7, pltpu.roll(xc, 1, 0))
        xp1 = jnp.where(last_row, n0, pltpu.roll(xc, CH - 1, 0))
        xp2 = jnp.where(last_row2, n0, jnp.where(last_row, n1, pltpu.roll(xc, CH - 2, 0)))
        u = (xm1 * cw_ref[0:1, :] + cb_ref[...] + xc * cw_ref[1:2, :] + xp1 * cw_ref[2:3, :]
             + xp2 * cw_ref[3:4, :])
        ub = u.astype(BF16)
        for d, (a_ref, b_ref) in enumerate(((af_ref, bf_ref), (ar_ref, br_ref))):
            pre = []
            for gi in range(2):
                pre.append(jnp.concatenate(
                    [_dot(ub[:, k * LRU_BS:(k + 1) * LRU_BS], gw_ref[d, gi, k].astype(BF16))
                     for k in range(nb)], axis=-1) + gb_ref[d, gi:gi + 1, :])
            r = _sigmoid(pre[0])
            ig = _sigmoid(pre[1])
            a = jnp.exp(-LRU_C * r * sps[d])
            b = jnp.sqrt(1.0 - a * a) * (ig * u)
            for s in (1, 2, 4):
                if d == 0:
                    keep = crow >= s
                    a_sh = pltpu.roll(a, s, 0)
                    b_sh = pltpu.roll(b, s, 0)
                else:
                    keep = crow < SUB - s
                    a_sh = pltpu.roll(a, CH - s, 0)
                    b_sh = pltpu.roll(b, CH - s, 0)
                b = jnp.where(keep, a * b_sh + b, b)
                a = jnp.where(keep, a * a_sh, a)
            a_ref[pl.ds(off, CH), :] = a
            b_ref[pl.ds(off, CH), :] = b
        return carry

    lax.fori_loop(0, S // CH, gates, 0)

    def scan(ft, rt, carry):
        hf, hr = carry
        of = pl.multiple_of(ft * SUB, SUB)
        orv = pl.multiple_of(rt * SUB, SUB)
        tf = af_ref[pl.ds(of, SUB), :] * hf + bf_ref[pl.ds(of, SUB), :]
        tr = ar_ref[pl.ds(orv, SUB), :] * hr + br_ref[pl.ds(orv, SUB), :]
        bf_ref[pl.ds(of, SUB), :] = tf
        br_ref[pl.ds(orv, SUB), :] = tr
        return (jnp.broadcast_to(tf[SUB - 1:SUB, :], (SUB, LW)), jnp.broadcast_to(tr[0:1, :], (SUB, LW)))

    zero = jnp.zeros((SUB, LW), F32)
    carry = lax.fori_loop(0, ntc, lambda i, c: scan(i, ntc - 1 - i, c), (zero, zero), unroll=8)
    lax.fori_loop(ntc, nt, lambda i, c: scan(i, nt - 1 - (i - ntc), c), carry, unroll=8)

    def outp(c, carry):
        off = pl.multiple_of(c * CH, CH)
        src = pl.multiple_of(Lc + off, CH)
        hs = bf_ref[pl.ds(src, CH), :] + br_ref[pl.ds(src, CH), :]
        o_ref[0, pl.ds(off, CH), :] = (hs * _gelu_tanh(g_ref[0, pl.ds(src, CH), :])).astype(o_ref.dtype)
        return carry

    lax.fori_loop(0, N // CH, outp, 0)


def _lru(P1, col_x, col_g, conv_w, conv_b, gate_w, gate_b, lam, Lc):
    B, S, _ = P1.shape
    N = S - Lc
    nb = 2
    lw = nb * LRU_BS
    return pl.pallas_call(
        functools.partial(_lru_kernel, Lc=Lc, N=N), grid=(B, LRU_BLOCKS // nb),
        in_specs=[pl.BlockSpec((1, S, lw), lambda b, j: (b, 0, col_x // lw + j)),
                  pl.BlockSpec((1, S, lw), lambda b, j: (b, 0, col_g // lw + j)),
                  pl.BlockSpec((4, lw), lambda b, j: (0, j)),
                  pl.BlockSpec((1, lw), lambda b, j: (0, j)),
                  pl.BlockSpec((2, 2, nb, LRU_BS, LRU_BS), lambda b, j: (0, 0, j, 0, 0)),
                  pl.BlockSpec((2, 2, lw), lambda b, j: (0, 0, j)),
                  pl.BlockSpec((2, lw), lambda b, j: (0, j))],
        out_specs=pl.BlockSpec((1, N, lw), lambda b, j: (b, 0, j)),
        out_shape=jax.ShapeDtypeStruct((B, N, LRU_WIDTH), BF16),
        scratch_shapes=[pltpu.VMEM((S, lw), F32) for _ in range(4)],
        compiler_params=_cp("parallel", "parallel"), name="lru",
    )(P1, P1, conv_w, conv_b.reshape(1, LRU_WIDTH), gate_w, gate_b, lam)


ROUTE_BISECT = 2048


def _cumsum_lanes(x):
    n = x.shape[-1]
    lane = lax.broadcasted_iota(jnp.int32, x.shape, 1)
    s = 1
    while s < n:
        x = x + jnp.where(lane >= s, pltpu.roll(x, s, 1), 0.0)
        s *= 2
    return x


def _route_kernel(lt_ref, idx_ref, g_ref, cm_ref, af_ref, *, segs, S):
    E = lt_ref.shape[1]
    b = pl.program_id(0)
    lt = lt_ref[0]
    ex = jnp.exp(lt - jnp.max(lt, axis=0, keepdims=True))
    aff = ex / jnp.sum(ex, axis=0, keepdims=True)
    lane_e = lax.broadcasted_iota(jnp.int32, (1, LANE), 1)
    slot0 = 0
    for lo, n, cap in segs:
        a = aff[:, lo:lo + n]

        def bisect(carry):
            lo_v, hi_v, c_lo, it, _ = carry
            mid = lo_v + 0.5 * (hi_v - lo_v)
            c = jnp.sum(jnp.where(a >= mid, 1.0, 0.0), axis=1, keepdims=True)
            ge = c >= cap
            lo_v = jnp.where(ge, mid, lo_v)
            hi_v = jnp.where(ge, hi_v, mid)
            c_lo = jnp.where(ge, c, c_lo)
            mid = lo_v + 0.5 * (hi_v - lo_v)
            done = (c_lo == cap) | (mid <= lo_v) | (mid >= hi_v)
            return lo_v, hi_v, c_lo, it + 1, jnp.min(jnp.where(done, 1, 0))

        lo_v, hi_v, _, _, _ = lax.while_loop(
            lambda carry: (carry[4] == 0) & (carry[3] < ROUTE_BISECT), bisect,
            (jnp.zeros((E, 1), F32), jnp.full((E, 1), 2.0, F32), jnp.full((E, 1), float(n), F32),
             jnp.int32(0), jnp.int32(0)))
        top = jnp.where(a >= hi_v, 1.0, 0.0)
        tie = jnp.where((a >= lo_v) & (a < hi_v), 1.0, 0.0)
        need = cap - jnp.sum(top, axis=1, keepdims=True)
        tie_rank = _cumsum_lanes(tie) - tie
        sel = top + tie * jnp.where(tie_rank < need, 1.0, 0.0)
        cm_ref[:, 0:n] = _cumsum_lanes(sel) * sel
        af_ref[:, 0:n] = a
        tok = (lax.broadcasted_iota(jnp.int32, (1, n), 1) + (lo + b * S)).astype(F32)
        jcol = (lax.broadcasted_iota(jnp.int32, (cap, 1), 0) + 1).astype(F32)

        def per_expert(e, carry):
            oi, og = carry
            ce = cm_ref[pl.ds(e, 1), 0:n]
            ae = af_ref[pl.ds(e, 1), 0:n]
            acc_i = jnp.zeros((cap, LANE), F32)
            acc_g = jnp.zeros((cap, LANE), F32)
            for k in range(n // LANE):
                sl = slice(k * LANE, (k + 1) * LANE)
                hit = ce[:, sl] == jcol
                acc_i = acc_i + jnp.where(hit, tok[:, sl], 0.0)
                acc_g = acc_g + jnp.where(hit, ae[:, sl], 0.0)
            mine = lane_e == e
            return (jnp.where(mine, jnp.sum(acc_i, axis=1, keepdims=True), oi),
                    jnp.where(mine, jnp.sum(acc_g, axis=1, keepdims=True), og))

        zero = jnp.zeros((cap, LANE), F32)
        oi, og = lax.fori_loop(0, E, per_expert, (zero, zero))
        idx_ref[0, slot0:slot0 + cap, :] = oi
        g_ref[0, slot0:slot0 + cap, :] = og
        slot0 += cap


def _route(logits_t, segs):
    B, E, S = logits_t.shape
    slots = sum(cap for _, _, cap in segs)
    nmax = max(n for _, n, _ in segs)
    out = jax.ShapeDtypeStruct((B, slots, LANE), F32)
    blk = pl.BlockSpec((1, slots, LANE), lambda b: (b, 0, 0))
    return pl.pallas_call(
        functools.partial(_route_kernel, segs=segs, S=S), grid=(B,),
        in_specs=[pl.BlockSpec((1, E, S), lambda b: (b, 0, 0))],
        out_specs=[blk, blk], out_shape=[out, out],
        scratch_shapes=[pltpu.VMEM((E, nmax), F32), pltpu.VMEM((E, nmax), F32)],
        compiler_params=_cp("parallel"), name="route",
    )(logits_t)


def _ffn_kernel(ids_ref, h_hbm, w1_ref, w3_ref, w2_ref, g_ref, zero_hbm, out_hbm,
                xf_ref, xb_ref, y_ref, ring_ref, xsem, asem, ssem, *, rm, ns, nf):
    del zero_hbm
    C = rm // nf
    D = xf_ref.shape[-1]
    step = pl.program_id(0)
    f = pl.program_id(1)
    q = step * nf + f
    slot = q % 2
    cur = step % 2
    f0 = pl.multiple_of(f * C, SUB)

    def gather(src, dst, sem, tb, row0, n):
        for i in range(n):
            pltpu.make_async_copy(src.at[pl.ds(ids_ref[tb + i], 1), :], dst.at[pl.ds(row0 + i, 1), :], sem).start()

    def scatter(src, tb, n):
        for i in range(n):
            pltpu.make_async_copy(src.at[pl.ds(i, 1), :], out_hbm.at[pl.ds(ids_ref[tb + i], 1), :], ssem).start()

    def wait_rows(buf, sem):
        pltpu.make_async_copy(buf, buf, sem).wait()

    comb = jnp.where(step == 0, rm, (step - 1) * rm)
    pstep = jnp.where(f == 0, step - 2, step - 1)
    tb_prev = jnp.where(pstep < 0, rm, pstep * rm) + jnp.where(f == 0, nf - 1, f - 1) * C

    @pl.when(q == 0)
    def _():
        ring_ref[...] = jnp.zeros_like(ring_ref)
        y_ref[1] = jnp.zeros((rm, D), F32)

        def first(r, carry):
            pltpu.make_async_copy(h_hbm.at[pl.ds(ids_ref[r], 1), :], xf_ref.at[pl.ds(r, 1), :], xsem).start()
            return carry
        lax.fori_loop(0, rm, first, 0, unroll=8)
        scatter(ring_ref.at[0], rm, C)

    @pl.when(f == 0)
    def _():
        wait_rows(xf_ref, xsem)
        xb_ref[...] = xf_ref[...].astype(BF16)
        y_ref[cur] = jnp.zeros((rm, D), F32)

    wait_rows(ring_ref.at[0], ssem)
    scatter(ring_ref.at[1 - slot], tb_prev, C)
    gather(out_hbm, ring_ref.at[slot], asem, comb + f * C, 0, C)
    gather(h_hbm, xf_ref, xsem, jnp.minimum(step + 1, ns - 1) * rm + f * C, f0, C)

    x = xb_ref[...]
    u = _dot(x, w1_ref[0, 0].astype(BF16))
    v = _dot(x, w3_ref[0, 0].astype(BF16))
    a = (u * _sigmoid(u) * v).astype(BF16)
    dc = _pick(D, (512, 256, 128))
    for c in range(D // dc):
        y_ref[cur, :, c * dc:(c + 1) * dc] += _dot(a, w2_ref[0, 0, :, c * dc:(c + 1) * dc].astype(BF16))

    wait_rows(ring_ref.at[0], asem)
    ring_ref[slot] = ring_ref[slot] + y_ref[1 - cur, pl.ds(f0, C), :] * g_ref[0, pl.ds(f0, C), :]

    @pl.when(q == (ns + 1) * nf - 1)
    def _():
        scatter(ring_ref.at[slot], comb + f * C, C)
        wait_rows(ring_ref.at[0], ssem)
        wait_rows(ring_ref.at[0], ssem)
        wait_rows(xf_ref, xsem)


def _ffn(hf, ids, gs, w1, w3, w2, l):
    T, D = hf.shape
    E, R, _ = gs.shape
    FF = w1.shape[-1]
    rm = R // 2
    ns = 2 * E
    tf = _pick(FF, (256, 128))
    nf = FF // tf
    assert rm % nf == 0 and (rm // nf) % SUB == 0
    ex = lambda s: jnp.minimum(s, ns - 1) // 2
    grid_spec = pltpu.PrefetchScalarGridSpec(
        num_scalar_prefetch=1, grid=(ns + 1, nf),
        in_specs=[pl.BlockSpec(memory_space=pl.ANY),
                  pl.BlockSpec((1, 1, D, tf), lambda s, f, ids: (l, ex(s), 0, f)),
                  pl.BlockSpec((1, 1, D, tf), lambda s, f, ids: (l, ex(s), 0, f)),
                  pl.BlockSpec((1, 1, tf, D), lambda s, f, ids: (l, ex(s), f, 0)),
                  pl.BlockSpec((1, rm, 1), lambda s, f, ids: (jnp.maximum(s - 1, 0), 0, 0)),
                  pl.BlockSpec(memory_space=pl.ANY)],
        out_specs=pl.BlockSpec(memory_space=pl.ANY),
        scratch_shapes=[pltpu.VMEM((rm, D), F32), pltpu.VMEM((rm, D), BF16),
                        pltpu.VMEM((2, rm, D), F32), pltpu.VMEM((2, rm // nf, D), F32),
                        pltpu.SemaphoreType.DMA, pltpu.SemaphoreType.DMA, pltpu.SemaphoreType.DMA])
    return pl.pallas_call(
        functools.partial(_ffn_kernel, rm=rm, ns=ns, nf=nf), grid_spec=grid_spec,
        out_shape=jax.ShapeDtypeStruct((T, D), F32),
        input_output_aliases={6: 0},
        compiler_params=_cp("arbitrary", "arbitrary"), name="ffn",
    )(ids, hf, w1, w3, w2, gs.reshape(ns, rm, 1), jnp.zeros((T, D), F32))


def _moe(h, logits, w1, w3, w2, l, Lc):
    B, S, D = h.shape
    E = N_EXPERTS
    assert B % 2 == 0
    segs = tuple((lo, n, EC_FACTOR * n // E) for lo, n in ((Lc, S - Lc), (0, Lc)) if n)
    idx, g = _route(jnp.swapaxes(logits[:, :, :E], 1, 2), segs)
    ids = jnp.transpose(idx[:, :, :E], (2, 0, 1)).astype(jnp.int32).reshape(-1)
    gs = jnp.transpose(g[:, :, :E], (2, 0, 1)).reshape(E, -1, 1)
    return _ffn(h.reshape(B * S, D), ids, gs, w1, w3, w2, l).reshape(B, S, D)


def kernel(x, c, ctx, c_ctx, ada_w, ada_b, norm_mix_pre, norm_mix_post, norm_ffn_pre, norm_ffn_post,
           ab_w_in, ab_w_out, mlstm_gate_b, mlstm_norm, na_rpb,
           cd_w_in, cd_w_out, gqa_sink, lru_conv_w, lru_conv_b, lru_gate_w, lru_gate_b, lru_lambda,
           moe_router, moe_w1, moe_w3, moe_w2):
    B, N, D = x.shape
    Lc = ctx.shape[1]
    S = Lc + N
    assert Lc == TS and N % TS == 0 and B < SUB
    ctx_row = B
    xs = jnp.concatenate([ctx, x], axis=1)
    cond = jnp.concatenate([c, c_ctx[None], jnp.zeros((SUB - B - 1, D), F32)], axis=0)
    mod = _ada(cond, ada_w, ada_b).reshape(-1, 1, D)
    router = jnp.pad(moe_router, ((0, 0), (0, 0), (0, LANE - N_EXPERTS))).astype(BF16)

    h = _prenorm(xs, norm_mix_pre[0], mod, 0, ctx_row, True)
    hf = h.reshape(B * S, D)
    w_in = ab_w_in[0]
    o0 = 3 * MLSTM_QK
    n0 = o0 + MLSTM_V + MLSTM_GATES
    w_qkv = jnp.concatenate([w_in[:, :o0], w_in[:, n0:]], axis=1).astype(BF16)
    w_og = jnp.pad(w_in[:, o0:n0], ((0, 0), (0, LANE - MLSTM_GATES))).astype(BF16)
    P = _matmul(hf, w_qkv, BF16).reshape(B, S, -1)
    OG = _matmul(hf, w_og, F32).reshape(B, S, -1)
    gcol = OG[:, :, MLSTM_V:MLSTM_V + MLSTM_GATES]
    Hd = _mlstm(P, gcol, jnp.swapaxes(gcol, 1, 2), mlstm_gate_b[0])
    ML = _mlstm_out(Hd, OG, mlstm_norm[0])
    NL = _na(P, _na_bias(na_rpb[0]), Lc, o0)
    Y = _matmul2(ML.reshape(B * S, -1), NL.reshape(B * S, -1), ab_w_out[0].astype(BF16)).reshape(B, S, D)
    xs, hb, lg = _resid(xs, Y, norm_mix_post[0], mod, 0, 2, ctx_row, 0, (norm_ffn_pre[0], 0, 3, 4, router[0]))
    Y = _moe(hb, lg, moe_w1, moe_w3, moe_w2, 0, Lc)
    xs, h = _resid(xs, Y, norm_ffn_post[0], mod, 0, 5, ctx_row, 0, (norm_mix_pre[1], 1, 0, 1, None))

    P1 = _matmul(h.reshape(B * S, D), cd_w_in[0].astype(BF16), F32).reshape(B, S, -1)
    cos, sin = _rope_tables(Lc, N)
    AL = _gqa(P1, cos, sin, gqa_sink[0], Lc)
    RL = _lru(P1, GQA_Q + 2 * GQA_KV, GQA_Q + 2 * GQA_KV + LRU_WIDTH,
              lru_conv_w[0], lru_conv_b[0], lru_gate_w[0], lru_gate_b[0], lru_lambda[0], Lc)
    Y = _matmul2(AL.reshape(B * N, -1), RL.reshape(B * N, -1), cd_w_out[0].astype(BF16)).reshape(B, N, D)
    xl, hb, lg = _resid(xs, Y, norm_mix_post[1], mod, 1, 2, None, Lc // TS, (norm_ffn_pre[1], 1, 3, 4, router[1]))
    Y = _moe(hb, lg, moe_w1, moe_w3, moe_w2, 1, 0)
    (xl,) = _resid(xl, Y, norm_ffn_post[1], mod, 1, 5, None, 0, None)
    return xl
```

```python
import functools

import numpy as np
import jax
import jax.numpy as jnp
from jax import lax
from jax.experimental import pallas as pl
from jax.experimental.pallas import tpu as pltpu

F32 = jnp.float32
BF16 = jnp.bfloat16
EPS = 1e-6
NEG = -1e30

GRID_W = 64
MLSTM_HEADS = 4
MLSTM_DK = 256
MLSTM_DV = 256
MLSTM_QK = MLSTM_HEADS * MLSTM_DK
MLSTM_V = MLSTM_HEADS * MLSTM_DV
MLSTM_GATES = 2 * 2 * MLSTM_HEADS
NA_HEADS = 8
NA_DIM = 128
NA_WIDTH = NA_HEADS * NA_DIM
NA_KH = 8
NA_KW = 16
GQA_HEADS = 16
GQA_KV_HEADS = 4
GQA_DIM = 64
GQA_Q = GQA_HEADS * GQA_DIM
GQA_KV = GQA_KV_HEADS * GQA_DIM
GQA_WINDOW = 128
ROPE_THETA = 10000.0
LRU_WIDTH = 1024
LRU_BLOCKS = 8
LRU_BS = LRU_WIDTH // LRU_BLOCKS
LRU_C = 8.0
N_EXPERTS = 16
EC_FACTOR = 2

TS = 256
LANE = 128
SUB = 8
VMEM_LIMIT = 56 * 1024 * 1024


def _cp(*sem):
    return pltpu.CompilerParams(dimension_semantics=sem, vmem_limit_bytes=VMEM_LIMIT)


def _pick(n, prefs):
    for p in prefs:
        if n % p == 0:
            return p
    return n


def _sigmoid(x):
    return 1.0 / (1.0 + jnp.exp(-x))


def _dot(a, b):
    return jnp.dot(a, b, preferred_element_type=F32)


def _dot_nt(a, b):
    return lax.dot_general(a, b, (((1,), (1,)), ((), ())), preferred_element_type=F32)


def _dot_tn(a, b):
    return lax.dot_general(a, b, (((0,), (0,)), ((), ())), preferred_element_type=F32)


def _rms(x, w):
    return x * lax.rsqrt(jnp.mean(x * x, axis=-1, keepdims=True) + EPS) * w


def _ada_kernel(c_ref, w_ref, b_ref, o_ref):
    c = c_ref[...]
    a = (c * _sigmoid(c)).astype(BF16)
    o_ref[0] = _dot(a, w_ref[0].astype(BF16)) + b_ref[0]


def _ada(cond, ada_w, ada_b):
    L, D, D6 = ada_w.shape
    tn = _pick(D6, (1024, 512, 256, 128))
    return pl.pallas_call(
        _ada_kernel, grid=(L, D6 // tn),
        in_specs=[pl.BlockSpec((SUB, D), lambda l, j: (0, 0)),
                  pl.BlockSpec((1, D, tn), lambda l, j: (l, 0, j)),
                  pl.BlockSpec((1, 1, tn), lambda l, j: (l, 0, j))],
        out_specs=pl.BlockSpec((1, SUB, tn), lambda l, j: (l, 0, j)),
        out_shape=jax.ShapeDtypeStruct((L, SUB, D6), F32),
        compiler_params=_cp("parallel", "parallel"), name="ada",
    )(cond, ada_w, ada_b.reshape(L, 1, D6))


def _mod_map(l, k, ctx_row, has_ctx):
    if has_ctx:
        return lambda b, s: ((l * SUB + jnp.where(s == 0, ctx_row, b)) * 6 + k, 0, 0)
    return lambda b, s: ((l * SUB + b) * 6 + k, 0, 0)


def _prenorm_kernel(x_ref, w_ref, sh_ref, sc_ref, o_ref):
    y = _rms(x_ref[0], w_ref[...])
    o_ref[0] = (y * (1.0 + sc_ref[0]) + sh_ref[0]).astype(o_ref.dtype)


def _prenorm(x, w, mod, l, ctx_row, has_ctx):
    B, S, D = x.shape
    vec = lambda k: pl.BlockSpec((1, 1, D), _mod_map(l, k, ctx_row, has_ctx))
    return pl.pallas_call(
        _prenorm_kernel, grid=(B, S // TS),
        in_specs=[pl.BlockSpec((1, TS, D), lambda b, s: (b, s, 0)),
                  pl.BlockSpec((1, D), lambda b, s: (0, 0)), vec(0), vec(1)],
        out_specs=pl.BlockSpec((1, TS, D), lambda b, s: (b, s, 0)),
        out_shape=jax.ShapeDtypeStruct((B, S, D), BF16),
        compiler_params=_cp("parallel", "parallel"), name="prenorm",
    )(x, w.reshape(1, D), mod, mod)


def _resid_kernel(nxt_mode, x_ref, y_ref, wpost_ref, g_ref, *rest):
    xn = x_ref[0] + g_ref[0] * _rms(y_ref[0], wpost_ref[...])
    if nxt_mode is None:
        (xo_ref,) = rest
    elif nxt_mode == "mixer":
        wpre_ref, sh_ref, sc_ref, xo_ref, ho_ref = rest
    else:
        wpre_ref, sh_ref, sc_ref, router_ref, xo_ref, ho_ref, lo_ref = rest
    xo_ref[0] = xn
    if nxt_mode is not None:
        h = _rms(xn, wpre_ref[...]) * (1.0 + sc_ref[0]) + sh_ref[0]
        ho_ref[0] = h.astype(ho_ref.dtype)
        if nxt_mode == "experts":
            lo_ref[0] = _dot(h.astype(BF16), router_ref[...])


def _resid(x, y, wpost, mod, l, kg, ctx_row, x_off, nxt):
    B, Sy, D = y.shape
    has_ctx = x_off == 0 and x.shape[1] == Sy and ctx_row is not None
    vec = lambda ll, k: pl.BlockSpec((1, 1, D), _mod_map(ll, k, ctx_row, has_ctx))
    row = pl.BlockSpec((1, D), lambda b, s: (0, 0))
    tile = pl.BlockSpec((1, TS, D), lambda b, s: (b, s, 0))
    in_specs = [pl.BlockSpec((1, TS, D), lambda b, s: (b, s + x_off, 0)), tile, row, vec(l, kg)]
    args = [x, y, wpost.reshape(1, D), mod]
    out_specs = [tile]
    out_shape = [jax.ShapeDtypeStruct((B, Sy, D), F32)]
    mode = None
    if nxt is not None:
        wpre, ln, ksh, ksc, router = nxt
        in_specs += [row, vec(ln, ksh), vec(ln, ksc)]
        args += [wpre.reshape(1, D), mod, mod]
        out_specs.append(tile)
        if router is None:
            mode = "mixer"
            out_shape.append(jax.ShapeDtypeStruct((B, Sy, D), BF16))
        else:
            mode = "experts"
            in_specs.append(pl.BlockSpec((D, LANE), lambda b, s: (0, 0)))
            args.append(router)
            out_specs.append(pl.BlockSpec((1, TS, LANE), lambda b, s: (b, s, 0)))
            out_shape += [jax.ShapeDtypeStruct((B, Sy, D), F32), jax.ShapeDtypeStruct((B, Sy, LANE), F32)]
    out = pl.pallas_call(
        functools.partial(_resid_kernel, mode), grid=(B, Sy // TS),
        in_specs=in_specs, out_specs=out_specs, out_shape=out_shape,
        compiler_params=_cp("parallel", "parallel"), name="resid",
    )(*args)
    return out


def _mm_kernel(x_ref, w_ref, o_ref):
    o_ref[...] = _dot(x_ref[...].astype(BF16), w_ref[...]).astype(o_ref.dtype)


def _matmul(x, w, out_dtype):
    M, K = x.shape
    N = w.shape[1]
    tm = _pick(M, (1024, 512, 256))
    tn = _pick(N, (1024, 512, 384, 256, 128))
    return pl.pallas_call(
        _mm_kernel, grid=(M // tm, N // tn),
        in_specs=[pl.BlockSpec((tm, K), lambda i, j: (i, 0)),
                  pl.BlockSpec((K, tn), lambda i, j: (0, j))],
        out_specs=pl.BlockSpec((tm, tn), lambda i, j: (i, j)),
        out_shape=jax.ShapeDtypeStruct((M, N), out_dtype),
        compiler_params=_cp("parallel", "parallel"), name="matmul",
    )(x, w)


def _mm2_kernel(x1_ref, x2_ref, w1_ref, w2_ref, o_ref):
    o_ref[...] = _dot(x1_ref[...], w1_ref[...]) + _dot(x2_ref[...], w2_ref[...])


def _matmul2(x1, x2, w):
    M, K1 = x1.shape
    K2 = x2.shape[1]
    assert K1 == K2 and w.shape[0] == K1 + K2
    N = w.shape[1]
    tm = _pick(M, (1024, 512, 256))
    tn = _pick(N, (512, 256, 128))
    return pl.pallas_call(
        _mm2_kernel, grid=(M // tm, N // tn),
        in_specs=[pl.BlockSpec((tm, K1), lambda i, j: (i, 0)),
                  pl.BlockSpec((tm, K2), lambda i, j: (i, 0)),
                  pl.BlockSpec((K1, tn), lambda i, j: (0, j)),
                  pl.BlockSpec((K2, tn), lambda i, j: (1, j))],
        out_specs=pl.BlockSpec((tm, tn), lambda i, j: (i, j)),
        out_shape=jax.ShapeDtypeStruct((M, N), F32),
        compiler_params=_cp("parallel", "parallel"), name="matmul2",
    )(x1, x2, w, w)


def _log_sigmoid(x):
    return jnp.minimum(x, 0.0) - jnp.log1p(jnp.exp(-jnp.abs(x)))


def _mlstm_kernel(q_ref, k_ref, v_ref, gc_ref, gr_ref, bc_ref, br_ref, o_ref, C_ref, n_ref, m_ref):
    H, dk, dv, L = MLSTM_HEADS, MLSTM_DK, MLSTM_DV, TS
    d = pl.program_id(1)
    t = pl.program_id(2)

    @pl.when(t == 0)
    def _():
        C_ref[...] = jnp.zeros_like(C_ref)
        n_ref[...] = jnp.zeros_like(n_ref)
        m_ref[...] = jnp.zeros_like(m_ref)

    fwd = d == 0
    ri = lax.broadcasted_iota(jnp.int32, (L, L), 0)
    ci = lax.broadcasted_iota(jnp.int32, (L, L), 1)
    diff = (ci - ri) * (1 - 2 * d)
    causal = diff <= 0
    causal_f = jnp.where(causal, 1.0, 0.0)
    causal_t = jnp.where(diff >= 0, 1.0, 0.0)
    gc = gc_ref[0] + bc_ref[...]
    gr = gr_ref[0] + br_ref[...]
    lfc = _log_sigmoid(gc)
    lfr = _log_sigmoid(gr)
    hi = lax.Precision.HIGHEST
    bcol_all = jnp.dot(causal_f, lfc, precision=hi, preferred_element_type=F32)
    brow_all = jnp.dot(lfr, causal_t, precision=hi, preferred_element_type=F32)
    tot_all = jnp.sum(lfr, axis=-1, keepdims=True)
    sel = lambda a, b: jnp.where(fwd, a, b)
    scale = dk ** -0.5
    for h in range(H):
        ic = sel(gc[:, h:h + 1], gc[:, 2 * H + h:2 * H + h + 1])
        ir = sel(gr[h:h + 1], gr[2 * H + h:2 * H + h + 1])
        bc = sel(bcol_all[:, H + h:H + h + 1], bcol_all[:, 3 * H + h:3 * H + h + 1])
        br = sel(brow_all[H + h:H + h + 1], brow_all[3 * H + h:3 * H + h + 1])
        tot = sel(tot_all[H + h:H + h + 1], tot_all[3 * H + h:3 * H + h + 1])
        m = m_ref[h]
        dmat = jnp.where(causal, bc - br + ir, NEG)
        inter = bc + m
        mj = jnp.maximum(inter, jnp.max(dmat, axis=-1, keepdims=True))
        p = jnp.exp(dmat - mj)
        q = q_ref[0, :, h * dk:(h + 1) * dk]
        k = k_ref[0, :, h * dk:(h + 1) * dk]
        v = v_ref[0, :, h * dv:(h + 1) * dv]
        s = _dot_nt(q, k) * scale * p
        w_inter = jnp.exp(inter - mj)
        num = w_inter * _dot(q, C_ref[h].astype(BF16)) + _dot(s.astype(BF16), v)
        qn = jnp.sum(q.astype(F32) * n_ref[h], axis=-1, keepdims=True)
        den = w_inter * qn + jnp.sum(s, axis=-1, keepdims=True)
        o_ref[0, 0, :, h * dv:(h + 1) * dv] = num / jnp.maximum(jnp.abs(den), jnp.exp(-mj))
        g_c = tot - bc + ic
        g_r = tot - br + ir
        m_new = jnp.maximum(tot + m, jnp.max(g_r, axis=-1, keepdims=True))
        decay = jnp.exp(tot + m - m_new)
        wk = jnp.exp(g_c - m_new) * (k.astype(F32) * scale)
        C_ref[h] = decay * C_ref[h] + _dot_tn(wk.astype(BF16), v)
        n_ref[h] = decay * n_ref[h] + jnp.sum(wk, axis=0, keepdims=True)
        m_ref[h] = m_new


def _mlstm(P, gcol, grow, gate_b):
    B, S, _ = P.shape
    nc = S // TS
    chunk = lambda d, t: jnp.where(d == 0, t, jnp.where(t == 0, 0, nc - t))
    blk = lambda c: pl.BlockSpec((1, TS, MLSTM_QK), lambda b, d, t: (b, chunk(d, t), c))
    G = MLSTM_GATES
    return pl.pallas_call(
        _mlstm_kernel, grid=(B, 2, nc),
        in_specs=[blk(0), blk(1), blk(2),
                  pl.BlockSpec((1, TS, G), lambda b, d, t: (b, chunk(d, t), 0)),
                  pl.BlockSpec((1, G, TS), lambda b, d, t: (b, 0, chunk(d, t))),
                  pl.BlockSpec((1, G), lambda b, d, t: (0, 0)),
                  pl.BlockSpec((G, 1), lambda b, d, t: (0, 0))],
        out_specs=pl.BlockSpec((1, 1, TS, MLSTM_V), lambda b, d, t: (d, b, chunk(d, t), 0)),
        out_shape=jax.ShapeDtypeStruct((2, B, S, MLSTM_V), F32),
        scratch_shapes=[pltpu.VMEM((MLSTM_HEADS, MLSTM_DK, MLSTM_DV), F32),
                        pltpu.VMEM((MLSTM_HEADS, 1, MLSTM_DK), F32),
                        pltpu.VMEM((MLSTM_HEADS, 1, 1), F32)],
        compiler_params=_cp("parallel", "arbitrary", "arbitrary"), name="mlstm",
    )(P, P, P, gcol, grow, gate_b.reshape(1, G), gate_b.reshape(G, 1))


def _mlstm_out_kernel(hf_ref, hr_ref, o_ref, gain_ref, out_ref):
    dv = MLSTM_DV
    for h in range(MLSTM_HEADS):
        sl = slice(h * dv, (h + 1) * dv)
        x = hf_ref[0, 0, :, sl] + hr_ref[0, 0, :, sl]
        out_ref[0, :, sl] = (_rms(x, gain_ref[:, sl]) * _sigmoid(o_ref[0, :, sl])).astype(out_ref.dtype)


def _mlstm_out(Hd, OG, gain):
    _, B, S, V = Hd.shape
    return pl.pallas_call(
        _mlstm_out_kernel, grid=(B, S // TS),
        in_specs=[pl.BlockSpec((1, 1, TS, V), lambda b, s: (0, b, s, 0)),
                  pl.BlockSpec((1, 1, TS, V), lambda b, s: (1, b, s, 0)),
                  pl.BlockSpec((1, TS, V), lambda b, s: (b, s, 0)),
                  pl.BlockSpec((1, V), lambda b, s: (0, 0))],
        out_specs=pl.BlockSpec((1, TS, V), lambda b, s: (b, s, 0)),
        out_shape=jax.ShapeDtypeStruct((B, S, V), BF16),
        compiler_params=_cp("parallel", "parallel"), name="mlstm_out",
    )(Hd, Hd, OG, gain.reshape(1, V))


NA_QR = 4
NA_KR = NA_KH + 2 * NA_QR - 4


def _na_bias(rpb):
    a = np.arange(NA_QR)
    kr = np.arange(NA_KR)
    qrel = NA_QR * np.arange(3)[:, None] + a[None, :]
    r0rel = np.stack([0 * a, a, NA_KR - NA_KH + 0 * a])
    valid_r = (kr[None, None] >= r0rel[..., None]) & (kr[None, None] < r0rel[..., None] + NA_KH)
    drow = np.clip(kr[None, None] - qrel[..., None] + (NA_KH - 1), 0, 2 * NA_KH - 2)
    w = np.arange(GRID_W)[:, None]
    cc = np.arange(GRID_W)[None, :]
    col_start = np.clip(w - NA_KW // 2, 0, GRID_W - NA_KW)
    valid_c = (cc >= col_start) & (cc < col_start + NA_KW)
    dcol = np.clip(cc - w + (NA_KW - 1), 0, 2 * NA_KW - 2)
    pick_r = jnp.asarray(np.eye(2 * NA_KH - 1, dtype=np.float32)[drow])
    pick_c = jnp.asarray(np.eye(2 * NA_KW - 1, dtype=np.float32)[dcol])
    t = jnp.einsum("cakr,hrs,wzs->chawkz", pick_r, rpb.astype(F32), pick_c,
                   precision=lax.Precision.HIGHEST)
    valid = valid_r[:, None, :, None, :, None] & valid_c[None, None, None, :, None, :]
    t = jnp.where(valid, t, NEG)
    return t.reshape(3, NA_HEADS, NA_QR * GRID_W, NA_KR * GRID_W)


def _na_kernel(q_ref, k_ref, v_ref, bias_ref, o_ref, *, Lc, rows_n):
    W, d = GRID_W, NA_DIM
    scale = d ** -0.5
    kc = k_ref[0, 0:Lc, :]
    vc = v_ref[0, 0:Lc, :]
    sc = _dot_nt(q_ref[0, 0:Lc, :], kc) * scale
    pc = jnp.exp(sc - jnp.max(sc, axis=-1, keepdims=True))
    oc = _dot(pc.astype(BF16), vc) / jnp.sum(pc, axis=-1, keepdims=True)
    o_ref[0, 0:Lc, :] = oc.astype(o_ref.dtype)

    def group(gi, carry):
        r = gi * NA_QR
        kr0 = jnp.clip(r - NA_KH // 2, 0, rows_n - NA_KR)
        qoff = pl.multiple_of(Lc + r * W, NA_QR * W)
        koff = pl.multiple_of(Lc + kr0 * W, W)
        qg = q_ref[0, pl.ds(qoff, NA_QR * W), :]
        kw = k_ref[0, pl.ds(koff, NA_KR * W), :]
        vw = v_ref[0, pl.ds(koff, NA_KR * W), :]
        sw = _dot_nt(qg, kw) * scale + bias_ref[(r - kr0) // NA_QR, 0]
        sx = _dot_nt(qg, kc) * scale
        m = jnp.maximum(jnp.max(sw, axis=-1, keepdims=True), jnp.max(sx, axis=-1, keepdims=True))
        pw = jnp.exp(sw - m)
        px = jnp.exp(sx - m)
        l = jnp.sum(pw, axis=-1, keepdims=True) + jnp.sum(px, axis=-1, keepdims=True)
        o = (_dot(pw.astype(BF16), vw) + _dot(px.astype(BF16), vc)) / l
        o_ref[0, pl.ds(qoff, NA_QR * W), :] = o.astype(o_ref.dtype)
        return carry

    lax.fori_loop(0, rows_n // NA_QR, group, 0, unroll=2)


def _na(P, bias, Lc, col0):
    B, S, _ = P.shape
    rows_n = (S - Lc) // GRID_W
    assert rows_n % NA_QR == 0 and rows_n >= NA_KR
    c0 = col0 // NA_DIM
    blk = lambda c: pl.BlockSpec((1, S, NA_DIM), lambda b, h: (b, 0, c0 + c * NA_HEADS + h))
    return pl.pallas_call(
        functools.partial(_na_kernel, Lc=Lc, rows_n=rows_n), grid=(B, NA_HEADS),
        in_specs=[blk(0), blk(1), blk(2),
                  pl.BlockSpec((3, 1, NA_QR * GRID_W, NA_KR * GRID_W), lambda b, h: (0, h, 0, 0))],
        out_specs=pl.BlockSpec((1, S, NA_DIM), lambda b, h: (b, 0, h)),
        out_shape=jax.ShapeDtypeStruct((B, S, NA_WIDTH), BF16),
        compiler_params=_cp("parallel", "parallel"), name="na",
    )(P, P, P, bias)


def _rope_tables(Lc, N):
    quarter = GQA_DIM // 4
    t = jnp.arange(N)
    inv = ROPE_THETA ** (-(jnp.arange(quarter, dtype=F32) / quarter))
    ang_r = (t // GRID_W).astype(F32)[:, None] * inv[None, :]
    ang_c = (t % GRID_W).astype(F32)[:, None] * inv[None, :]
    cos = jnp.concatenate([jnp.cos(ang_r)] * 2 + [jnp.cos(ang_c)] * 2, axis=-1)
    sin = jnp.concatenate([-jnp.sin(ang_r), jnp.sin(ang_r), -jnp.sin(ang_c), jnp.sin(ang_c)], axis=-1)
    cos = jnp.concatenate([jnp.ones((Lc, GQA_DIM), F32), cos], axis=0)
    sin = jnp.concatenate([jnp.zeros((Lc, GQA_DIM), F32), sin], axis=0)
    return jnp.tile(cos, (1, LANE // GQA_DIM)), jnp.tile(sin, (1, LANE // GQA_DIM))


def _rope_rot(x, cos, sin):
    quarter = GQA_DIM // 4
    lane = lax.broadcasted_iota(jnp.int32, x.shape, 1)
    first = (lane % (2 * quarter)) < quarter
    rot = jnp.where(first, pltpu.roll(x, LANE - quarter, 1), pltpu.roll(x, quarter, 1))
    return x * cos + rot * sin


def _gqa_kernel(sink_ref, q_ref, k_ref, v_ref, cos_ref, sin_ref, o_ref, kd_ref, vd_ref, *, Lc, N):
    W, dh, KV = GQA_WINDOW, GQA_DIM, GQA_KV_HEADS
    G = GQA_HEADS // KV
    S = Lc + N
    i = pl.program_id(1)
    scale = dh ** -0.5

    @pl.when(i == 0)
    def _():
        low_t = lax.broadcasted_iota(jnp.int32, (TS, LANE), 1) < dh
        zero = jnp.zeros((W, LANE), BF16)
        for kh in range(KV):
            for ref in (kd_ref, vd_ref):
                ref[kh, Lc:Lc + W, :] = zero
                ref[kh, Lc + W + N:Lc + 2 * W + N, :] = zero

        def build(c, carry):
            r0 = pl.multiple_of(c * TS, TS)
            dst = pl.multiple_of(r0 + jnp.where(c >= Lc // TS, W, 0), W)
            cs = cos_ref[pl.ds(r0, TS), :]
            sn = sin_ref[pl.ds(r0, TS), :]
            for hh in range(KV // 2):
                kr = _rope_rot(k_ref[0, pl.ds(r0, TS), hh * LANE:(hh + 1) * LANE], cs, sn)
                vx = v_ref[0, pl.ds(r0, TS), hh * LANE:(hh + 1) * LANE]
                for ref, a in ((kd_ref, kr), (vd_ref, vx)):
                    sw = pltpu.roll(a, dh, 1)
                    ref[2 * hh, pl.ds(dst, TS), :] = jnp.where(low_t, a, sw).astype(BF16)
                    ref[2 * hh + 1, pl.ds(dst, TS), :] = jnp.where(low_t, sw, a).astype(BF16)
            return carry

        lax.fori_loop(0, S // TS, build, 0)

    span = 3 * W
    off = pl.multiple_of(Lc + i * W, W)
    cq = cos_ref[pl.ds(off, W), :]
    sq = sin_ref[pl.ds(off, W), :]
    ii = lax.broadcasted_iota(jnp.int32, (G * W, span), 0) % W
    jj = lax.broadcasted_iota(jnp.int32, (G * W, span), 1)
    kpos = i * W - W + jj
    valid = (kpos >= 0) & (kpos < N) & (jnp.abs(ii - (jj - W)) <= W)
    rg = lax.broadcasted_iota(jnp.int32, (G * W, 1), 0) // W
    low = lax.broadcasted_iota(jnp.int32, (W, LANE), 1) < dh
    for kh in range(KV):
        parts = []
        for slab in range(G // 2):
            j = kh * (G // 2) + slab
            qr = _rope_rot(q_ref[0, :, j * LANE:(j + 1) * LANE] * scale, cq, sq)
            parts += [jnp.where(low, qr, 0.0).astype(BF16), jnp.where(low, 0.0, qr).astype(BF16)]
        q4 = jnp.concatenate(parts, axis=0)
        kw = kd_ref[kh, pl.ds(off, span), :]
        vw = vd_ref[kh, pl.ds(off, span), :]
        kc = kd_ref[kh, 0:Lc, :]
        vc = vd_ref[kh, 0:Lc, :]
        sw = jnp.where(valid, _dot_nt(q4, kw), NEG)
        sx = _dot_nt(q4, kc)
        sk = sink_ref[G * kh + G - 1]
        for g in range(G - 2, -1, -1):
            sk = jnp.where(rg == g, sink_ref[G * kh + g], sk)
        m = jnp.maximum(jnp.maximum(jnp.max(sw, axis=-1, keepdims=True),
                                    jnp.max(sx, axis=-1, keepdims=True)), sk)
        pw = jnp.exp(sw - m)
        px = jnp.exp(sx - m)
        l = jnp.sum(pw, axis=-1, keepdims=True) + jnp.sum(px, axis=-1, keepdims=True) + jnp.exp(sk - m)
        o = (_dot(pw.astype(BF16), vw) + _dot(px.astype(BF16), vc)) / l
        for slab in range(G // 2):
            j = kh * (G // 2) + slab
            o_ref[0, :, j * LANE:(j + 1) * LANE] = jnp.where(
                low, o[2 * slab * W:(2 * slab + 1) * W], o[(2 * slab + 1) * W:(2 * slab + 2) * W]
            ).astype(o_ref.dtype)


def _gqa(P1, cos, sin, sink, Lc):
    B, S, _ = P1.shape
    N = S - Lc
    W = GQA_WINDOW
    rows = Lc + 2 * W + N
    table = pl.BlockSpec((S, LANE), lambda b, i: (0, 0))
    return pl.pallas_call(
        functools.partial(_gqa_kernel, Lc=Lc, N=N), grid=(B, N // W),
        in_specs=[pl.BlockSpec(memory_space=pltpu.SMEM),
                  pl.BlockSpec((1, W, GQA_Q), lambda b, i: (b, i + Lc // W, 0)),
                  pl.BlockSpec((1, S, GQA_KV), lambda b, i: (b, 0, GQA_Q // GQA_KV)),
                  pl.BlockSpec((1, S, GQA_KV), lambda b, i: (b, 0, GQA_Q // GQA_KV + 1)),
                  table, table],
        out_specs=pl.BlockSpec((1, W, GQA_Q), lambda b, i: (b, i, 0)),
        out_shape=jax.ShapeDtypeStruct((B, N, GQA_Q), BF16),
        scratch_shapes=[pltpu.VMEM((GQA_KV_HEADS, rows, LANE), BF16),
                        pltpu.VMEM((GQA_KV_HEADS, rows, LANE), BF16)],
        compiler_params=_cp("parallel", "arbitrary"), name="gqa",
    )(sink, P1, P1, P1, cos, sin)


def _gelu_tanh(x):
    return 0.5 * x * (1.0 + jnp.tanh(np.sqrt(2.0 / np.pi) * (x + 0.044715 * (x * x * x))))


def _lru_kernel(x_ref, g_ref, cw_ref, cb_ref, gw_ref, gb_ref, lam_ref, o_ref,
                af_ref, bf_ref, ar_ref, br_ref, *, Lc, N):
    S = Lc + N
    CH = 128
    LW = x_ref.shape[-1]
    nb = LW // LRU_BS
    nt = S // SUB
    ntc = Lc // SUB
    row = lax.broadcasted_iota(jnp.int32, (CH, LW), 0)
    crow = row % SUB
    first_row = row == 0
    last_row = row == CH - 1
    last_row2 = row == CH - 2
    sps = []
    for d in range(2):
        lam = lam_ref[d:d + 1, :]
        sps.append(jnp.maximum(-lam, 0.0) + jnp.log1p(jnp.exp(-jnp.abs(lam))))

    def gates(c, carry):
        off = pl.multiple_of(c * CH, CH)
        xc = x_ref[0, pl.ds(off, CH), :]
        prev = x_ref[0, pl.ds(pl.multiple_of(jnp.maximum(off - SUB, 0), SUB), SUB), :]
        nxt = x_ref[0, pl.ds(pl.multiple_of(jnp.minimum(off + CH, S - SUB), SUB), SUB), :]
        pz = jnp.where((c == 0) | (c == Lc // CH), 0.0, 1.0)
        nz = jnp.where((c == Lc // CH - 1) | (c == S // CH - 1), 0.0, 1.0)
        p7 = prev[SUB - 1:SUB, :] * pz
        n0 = nxt[0:1, :] * nz
        n1 = nxt[1:2, :] * nz
        xm1 = jnp.where(first_row, p7, pltpu.roll(xc, 1, 0))
        xp1 = jnp.where(last_row, n0, pltpu.roll(xc, CH - 1, 0))
        xp2 = jnp.where(last_row2, n0, jnp.where(last_row, n1, pltpu.roll(xc, CH - 2, 0)))
        u = (xm1 * cw_ref[0:1, :] + cb_ref[...] + xc * cw_ref[1:2, :] + xp1 * cw_ref[2:3, :]
             + xp2 * cw_ref[3:4, :])
        ub = u.astype(BF16)
        for d, (a_ref, b_ref) in enumerate(((af_ref, bf_ref), (ar_ref, br_ref))):
            pre = []
            for gi in range(2):
                pre.append(jnp.concatenate(
                    [_dot(ub[:, k * LRU_BS:(k + 1) * LRU_BS], gw_ref[d, gi, k].astype(BF16))
                     for k in range(nb)], axis=-1) + gb_ref[d, gi:gi + 1, :])
            r = _sigmoid(pre[0])
            ig = _sigmoid(pre[1])
            a = jnp.exp(-LRU_C * r * sps[d])
            b = jnp.sqrt(1.0 - a * a) * (ig * u)
            for s in (1, 2, 4):
                if d == 0:
                    keep = crow >= s
                    a_sh = pltpu.roll(a, s, 0)
                    b_sh = pltpu.roll(b, s, 0)
                else:
                    keep = crow < SUB - s
                    a_sh = pltpu.roll(a, CH - s, 0)
                    b_sh = pltpu.roll(b, CH - s, 0)
                b = jnp.where(keep, a * b_sh + b, b)
                a = jnp.where(keep, a * a_sh, a)
            a_ref[pl.ds(off, CH), :] = a
            b_ref[pl.ds(off, CH), :] = b
        return carry

    lax.fori_loop(0, S // CH, gates, 0)

    def scan(ft, rt, carry):
        hf, hr = carry
        of = pl.multiple_of(ft * SUB, SUB)
        orv = pl.multiple_of(rt * SUB, SUB)
        tf = af_ref[pl.ds(of, SUB), :] * hf + bf_ref[pl.ds(of, SUB), :]
        tr = ar_ref[pl.ds(orv, SUB), :] * hr + br_ref[pl.ds(orv, SUB), :]
        bf_ref[pl.ds(of, SUB), :] = tf
        br_ref[pl.ds(orv, SUB), :] = tr
        return (jnp.broadcast_to(tf[SUB - 1:SUB, :], (SUB, LW)), jnp.broadcast_to(tr[0:1, :], (SUB, LW)))

    zero = jnp.zeros((SUB, LW), F32)
    carry = lax.fori_loop(0, ntc, lambda i, c: scan(i, ntc - 1 - i, c), (zero, zero), unroll=8)
    lax.fori_loop(ntc, nt, lambda i, c: scan(i, nt - 1 - (i - ntc), c), carry, unroll=8)

    def outp(c, carry):
        off = pl.multiple_of(c * CH, CH)
        src = pl.multiple_of(Lc + off, CH)
        hs = bf_ref[pl.ds(src, CH), :] + br_ref[pl.ds(src, CH), :]
        o_ref[0, pl.ds(off, CH), :] = (hs * _gelu_tanh(g_ref[0, pl.ds(src, CH), :])).astype(o_ref.dtype)
        return carry

    lax.fori_loop(0, N // CH, outp, 0)


def _lru(P1, col_x, col_g, conv_w, conv_b, gate_w, gate_b, lam, Lc):
    B, S, _ = P1.shape
    N = S - Lc
    nb = 2
    lw = nb * LRU_BS
    return pl.pallas_call(
        functools.partial(_lru_kernel, Lc=Lc, N=N), grid=(B, LRU_BLOCKS // nb),
        in_specs=[pl.BlockSpec((1, S, lw), lambda b, j: (b, 0, col_x // lw + j)),
                  pl.BlockSpec((1, S, lw), lambda b, j: (b, 0, col_g // lw + j)),
                  pl.BlockSpec((4, lw), lambda b, j: (0, j)),
                  pl.BlockSpec((1, lw), lambda b, j: (0, j)),
                  pl.BlockSpec((2, 2, nb, LRU_BS, LRU_BS), lambda b, j: (0, 0, j, 0, 0)),
                  pl.BlockSpec((2, 2, lw), lambda b, j: (0, 0, j)),
                  pl.BlockSpec((2, lw), lambda b, j: (0, j))],
        out_specs=pl.BlockSpec((1, N, lw), lambda b, j: (b, 0, j)),
        out_shape=jax.ShapeDtypeStruct((B, N, LRU_WIDTH), BF16),
        scratch_shapes=[pltpu.VMEM((S, lw), F32) for _ in range(4)],
        compiler_params=_cp("parallel", "parallel"), name="lru",
    )(P1, P1, conv_w, conv_b.reshape(1, LRU_WIDTH), gate_w, gate_b, lam)


ROUTE_BISECT = 2048


def _cumsum_lanes(x):
    n = x.shape[-1]
    lane = lax.broadcasted_iota(jnp.int32, x.shape, 1)
    s = 1
    while s < n:
        x = x + jnp.where(lane >= s, pltpu.roll(x, s, 1), 0.0)
        s *= 2
    return x


def _route_kernel(lt_ref, idx_ref, g_ref, cm_ref, af_ref, *, segs, S):
    E = lt_ref.shape[1]
    b = pl.program_id(0)
    lt = lt_ref[0]
    ex = jnp.exp(lt - jnp.max(lt, axis=0, keepdims=True))
    aff = ex / jnp.sum(ex, axis=0, keepdims=True)
    lane_e = lax.broadcasted_iota(jnp.int32, (1, LANE), 1)
    slot0 = 0
    for lo, n, cap in segs:
        a = aff[:, lo:lo + n]

        def bisect(carry):
            lo_v, hi_v, c_lo, it, _ = carry
            mid = lo_v + 0.5 * (hi_v - lo_v)
            c = jnp.sum(jnp.where(a >= mid, 1.0, 0.0), axis=1, keepdims=True)
            ge = c >= cap
            lo_v = jnp.where(ge, mid, lo_v)
            hi_v = jnp.where(ge, hi_v, mid)
            c_lo = jnp.where(ge, c, c_lo)
            mid = lo_v + 0.5 * (hi_v - lo_v)
            done = (c_lo == cap) | (mid <= lo_v) | (mid >= hi_v)
            return lo_v, hi_v, c_lo, it + 1, jnp.min(jnp.where(done, 1, 0))

        lo_v, hi_v, _, _, _ = lax.while_loop(
            lambda carry: (carry[4] == 0) & (carry[3] < ROUTE_BISECT), bisect,
            (jnp.zeros((E, 1), F32), jnp.full((E, 1), 2.0, F32), jnp.full((E, 1), float(n), F32),
             jnp.int32(0), jnp.int32(0)))
        top = jnp.where(a >= hi_v, 1.0, 0.0)
        tie = jnp.where((a >= lo_v) & (a < hi_v), 1.0, 0.0)
        need = cap - jnp.sum(top, axis=1, keepdims=True)
        tie_rank = _cumsum_lanes(tie) - tie
        sel = top + tie * jnp.where(tie_rank < need, 1.0, 0.0)
        cm_ref[:, 0:n] = _cumsum_lanes(sel) * sel
        af_ref[:, 0:n] = a
        tok = (lax.broadcasted_iota(jnp.int32, (1, n), 1) + (lo + b * S)).astype(F32)
        jcol = (lax.broadcasted_iota(jnp.int32, (cap, 1), 0) + 1).astype(F32)

        def per_expert(e, carry):
            oi, og = carry
            ce = cm_ref[pl.ds(e, 1), 0:n]
            ae = af_ref[pl.ds(e, 1), 0:n]
            acc_i = jnp.zeros((cap, LANE), F32)
            acc_g = jnp.zeros((cap, LANE), F32)
            for k in range(n // LANE):
                sl = slice(k * LANE, (k + 1) * LANE)
                hit = ce[:, sl] == jcol
                acc_i = acc_i + jnp.where(hit, tok[:, sl], 0.0)
                acc_g = acc_g + jnp.where(hit, ae[:, sl], 0.0)
            mine = lane_e == e
            return (jnp.where(mine, jnp.sum(acc_i, axis=1, keepdims=True), oi),
                    jnp.where(mine, jnp.sum(acc_g, axis=1, keepdims=True), og))

        zero = jnp.zeros((cap, LANE), F32)
        oi, og = lax.fori_loop(0, E, per_expert, (zero, zero))
        idx_ref[0, slot0:slot0 + cap, :] = oi
        g_ref[0, slot0:slot0 + cap, :] = og
        slot0 += cap


def _route(logits_t, segs):
    B, E, S = logits_t.shape
    slots = sum(cap for _, _, cap in segs)
    nmax = max(n for _, n, _ in segs)
    out = jax.ShapeDtypeStruct((B, slots, LANE), F32)
    blk = pl.BlockSpec((1, slots, LANE), lambda b: (b, 0, 0))
    return pl.pallas_call(
        functools.partial(_route_kernel, segs=segs, S=S), grid=(B,),
        in_specs=[pl.BlockSpec((1, E, S), lambda b: (b, 0, 0))],
        out_specs=[blk, blk], out_shape=[out, out],
        scratch_shapes=[pltpu.VMEM((E, nmax), F32), pltpu.VMEM((E, nmax), F32)],
        compiler_params=_cp("parallel"), name="route",
    )(logits_t)


RING = 4


def _ffn_kernel(ids_ref, h_hbm, w1_ref, w3_ref, w2_ref, g_ref, zero_hbm, out_hbm,
                xf_ref, xb_ref, y_ref, ring_ref, xsem, asem, ssem, *, rm, ns, nf):
    del zero_hbm
    C = rm // nf
    D = xf_ref.shape[-1]
    step = pl.program_id(0)
    f = pl.program_id(1)
    q = step * nf + f
    cur = step % 2
    f0 = pl.multiple_of(f * C, SUB)

    def chunk_tokens(c_step, c_f):
        p = c_step - 1
        return jnp.where(p < 0, rm, jnp.minimum(p, ns - 1) * rm) + c_f * C

    def gather(src, dst, sem, tb, row0, n):
        for i in range(n):
            pltpu.make_async_copy(src.at[pl.ds(ids_ref[tb + i], 1), :], dst.at[pl.ds(row0 + i, 1), :], sem).start()

    def scatter(src, tb, n, sem):
        for i in range(n):
            pltpu.make_async_copy(src.at[pl.ds(i, 1), :], out_hbm.at[pl.ds(ids_ref[tb + i], 1), :], sem).start()

    def wait_rows(buf, sem):
        pltpu.make_async_copy(buf, buf, sem).wait()

    chunk = ring_ref.at[0]
    tb_prev = jnp.where(f == 0, chunk_tokens(step - 1, nf - 1), chunk_tokens(step, f - 1))
    tb_next = jnp.where(f == nf - 1, chunk_tokens(step + 1, 0), chunk_tokens(step, f + 1))

    @pl.when(q == 0)
    def _():
        ring_ref[...] = jnp.zeros_like(ring_ref)
        y_ref[1] = jnp.zeros((rm, D), F32)

        def first(r, carry):
            pltpu.make_async_copy(h_hbm.at[pl.ds(ids_ref[r], 1), :], xf_ref.at[pl.ds(r, 1), :], xsem).start()
            return carry
        lax.fori_loop(0, rm, first, 0, unroll=8)
        scatter(y_ref.at[1, pl.ds(0, C)], chunk_tokens(0, nf - 2), C, ssem.at[0])
        scatter(y_ref.at[1, pl.ds(0, C)], chunk_tokens(0, nf - 3), C, ssem.at[1])
        gather(out_hbm, ring_ref.at[0], asem.at[0], chunk_tokens(0, 0), 0, C)

    @pl.when(f == 0)
    def _():
        wait_rows(xf_ref, xsem)
        xb_ref[...] = xf_ref[...].astype(BF16)
        y_ref[cur] = jnp.zeros((rm, D), F32)

    wait_rows(chunk, ssem.at[q % 3])
    scatter(ring_ref.at[(q + RING - 1) % RING], tb_prev, C, ssem.at[(q + 2) % 3])
    gather(out_hbm, ring_ref.at[(q + 1) % RING], asem.at[(q + 1) % 2], tb_next, 0, C)
    gather(h_hbm, xf_ref, xsem, jnp.minimum(step + 1, ns - 1) * rm + f * C, f0, C)

    x = xb_ref[...]
    u = _dot(x, w1_ref[0, 0].astype(BF16))
    v = _dot(x, w3_ref[0, 0].astype(BF16))
    a = (u * _sigmoid(u) * v).astype(BF16)
    dc = _pick(D, (512, 256, 128))
    for c in range(D // dc):
        y_ref[cur, :, c * dc:(c + 1) * dc] += _dot(a, w2_ref[0, 0, :, c * dc:(c + 1) * dc].astype(BF16))

    slot = q % RING
    wait_rows(chunk, asem.at[q % 2])
    ring_ref[slot] = ring_ref[slot] + y_ref[1 - cur, pl.ds(f0, C), :] * g_ref[0, pl.ds(f0, C), :]

    @pl.when(q == (ns + 1) * nf - 1)
    def _():
        scatter(ring_ref.at[slot], chunk_tokens(step, f), C, ssem.at[q % 3])
        for k in range(3):
            wait_rows(chunk, ssem.at[k])
        wait_rows(chunk, asem.at[(q + 1) % 2])
        wait_rows(xf_ref, xsem)


def _ffn(hf, ids, gs, w1, w3, w2, l):
    T, D = hf.shape
    E, R, _ = gs.shape
    FF = w1.shape[-1]
    rm = R // 2
    ns = 2 * E
    tf = _pick(FF, (256, 128))
    nf = FF // tf
    assert nf >= 3 and rm % nf == 0 and (rm // nf) % SUB == 0
    ex = lambda s: jnp.minimum(s, ns - 1) // 2
    grid_spec = pltpu.PrefetchScalarGridSpec(
        num_scalar_prefetch=1, grid=(ns + 1, nf),
        in_specs=[pl.BlockSpec(memory_space=pl.ANY),
                  pl.BlockSpec((1, 1, D, tf), lambda s, f, ids: (l, ex(s), 0, f)),
                  pl.BlockSpec((1, 1, D, tf), lambda s, f, ids: (l, ex(s), 0, f)),
                  pl.BlockSpec((1, 1, tf, D), lambda s, f, ids: (l, ex(s), f, 0)),
                  pl.BlockSpec((1, rm, 1), lambda s, f, ids: (jnp.maximum(s - 1, 0), 0, 0)),
                  pl.BlockSpec(memory_space=pl.ANY)],
        out_specs=pl.BlockSpec(memory_space=pl.ANY),
        scratch_shapes=[pltpu.VMEM((rm, D), F32), pltpu.VMEM((rm, D), BF16),
                        pltpu.VMEM((2, rm, D), F32), pltpu.VMEM((RING, rm // nf, D), F32),
                        pltpu.SemaphoreType.DMA, pltpu.SemaphoreType.DMA((2,)), pltpu.SemaphoreType.DMA((3,))])
    return pl.pallas_call(
        functools.partial(_ffn_kernel, rm=rm, ns=ns, nf=nf), grid_spec=grid_spec,
        out_shape=jax.ShapeDtypeStruct((T, D), F32),
        input_output_aliases={6: 0},
        compiler_params=_cp("arbitrary", "arbitrary"), name="ffn",
    )(ids, hf, w1, w3, w2, gs.reshape(ns, rm, 1), jnp.zeros((T, D), F32))


def _moe(h, logits, w1, w3, w2, l, Lc):
    B, S, D = h.shape
    E = N_EXPERTS
    assert B % 2 == 0
    segs = tuple((lo, n, EC_FACTOR * n // E) for lo, n in ((Lc, S - Lc), (0, Lc)) if n)
    idx, g = _route(jnp.swapaxes(logits[:, :, :E], 1, 2), segs)
    ids = jnp.transpose(idx[:, :, :E], (2, 0, 1)).astype(jnp.int32).reshape(-1)
    gs = jnp.transpose(g[:, :, :E], (2, 0, 1)).reshape(E, -1, 1)
    return _ffn(h.reshape(B * S, D), ids, gs, w1, w3, w2, l).reshape(B, S, D)


def kernel(x, c, ctx, c_ctx, ada_w, ada_b, norm_mix_pre, norm_mix_post, norm_ffn_pre, norm_ffn_post,
           ab_w_in, ab_w_out, mlstm_gate_b, mlstm_norm, na_rpb,
           cd_w_in, cd_w_out, gqa_sink, lru_conv_w, lru_conv_b, lru_gate_w, lru_gate_b, lru_lambda,
           moe_router, moe_w1, moe_w3, moe_w2):
    B, N, D = x.shape
    Lc = ctx.shape[1]
    S = Lc + N
    assert Lc == TS and N % TS == 0 and B < SUB
    ctx_row = B
    xs = jnp.concatenate([ctx, x], axis=1)
    cond = jnp.concatenate([c, c_ctx[None], jnp.zeros((SUB - B - 1, D), F32)], axis=0)
    mod = _ada(cond, ada_w, ada_b).reshape(-1, 1, D)
    router = jnp.pad(moe_router, ((0, 0), (0, 0), (0, LANE - N_EXPERTS))).astype(BF16)

    h = _prenorm(xs, norm_mix_pre[0], mod, 0, ctx_row, True)
    hf = h.reshape(B * S, D)
    w_in = ab_w_in[0]
    o0 = 3 * MLSTM_QK
    n0 = o0 + MLSTM_V + MLSTM_GATES
    w_qkv = jnp.concatenate([w_in[:, :o0], w_in[:, n0:]], axis=1).astype(BF16)
    w_og = jnp.pad(w_in[:, o0:n0], ((0, 0), (0, LANE - MLSTM_GATES))).astype(BF16)
    P = _matmul(hf, w_qkv, BF16).reshape(B, S, -1)
    OG = _matmul(hf, w_og, F32).reshape(B, S, -1)
    gcol = OG[:, :, MLSTM_V:MLSTM_V + MLSTM_GATES]
    Hd = _mlstm(P, gcol, jnp.swapaxes(gcol, 1, 2), mlstm_gate_b[0])
    ML = _mlstm_out(Hd, OG, mlstm_norm[0])
    NL = _na(P, _na_bias(na_rpb[0]), Lc, o0)
    Y = _matmul2(ML.reshape(B * S, -1), NL.reshape(B * S, -1), ab_w_out[0].astype(BF16)).reshape(B, S, D)
    xs, hb, lg = _resid(xs, Y, norm_mix_post[0], mod, 0, 2, ctx_row, 0, (norm_ffn_pre[0], 0, 3, 4, router[0]))
    Y = _moe(hb, lg, moe_w1, moe_w3, moe_w2, 0, Lc)
    xs, h = _resid(xs, Y, norm_ffn_post[0], mod, 0, 5, ctx_row, 0, (norm_mix_pre[1], 1, 0, 1, None))

    P1 = _matmul(h.reshape(B * S, D), cd_w_in[0].astype(BF16), F32).reshape(B, S, -1)
    cos, sin = _rope_tables(Lc, N)
    AL = _gqa(P1, cos, sin, gqa_sink[0], Lc)
    RL = _lru(P1, GQA_Q + 2 * GQA_KV, GQA_Q + 2 * GQA_KV + LRU_WIDTH,
              lru_conv_w[0], lru_conv_b[0], lru_gate_w[0], lru_gate_b[0], lru_lambda[0], Lc)
    Y = _matmul2(AL.reshape(B * N, -1), RL.reshape(B * N, -1), cd_w_out[0].astype(BF16)).reshape(B, N, D)
    xl, hb, lg = _resid(xs, Y, norm_mix_post[1], mod, 1, 2, None, Lc // TS, (norm_ffn_pre[1], 1, 3, 4, router[1]))
    Y = _moe(hb, lg, moe_w1, moe_w3, moe_w2, 1, 0)
    (xl,) = _resid(xl, Y, norm_ffn_post[1], mod, 1, 5, None, 0, None)
    return xl
```

```python
import functools

import numpy as np
import jax
import jax.numpy as jnp
from jax import lax
from jax.experimental import pallas as pl
from jax.experimental.pallas import tpu as pltpu

F32 = jnp.float32
BF16 = jnp.bfloat16
EPS = 1e-6
NEG = -1e30

GRID_W = 64
MLSTM_HEADS = 4
MLSTM_DK = 256
MLSTM_DV = 256
MLSTM_QK = MLSTM_HEADS * MLSTM_DK
MLSTM_V = MLSTM_HEADS * MLSTM_DV
MLSTM_GATES = 2 * 2 * MLSTM_HEADS
NA_HEADS = 8
NA_DIM = 128
NA_WIDTH = NA_HEADS * NA_DIM
NA_KH = 8
NA_KW = 16
GQA_HEADS = 16
GQA_KV_HEADS = 4
GQA_DIM = 64
GQA_Q = GQA_HEADS * GQA_DIM
GQA_KV = GQA_KV_HEADS * GQA_DIM
GQA_WINDOW = 128
ROPE_THETA = 10000.0
LRU_WIDTH = 1024
LRU_BLOCKS = 8
LRU_BS = LRU_WIDTH // LRU_BLOCKS
LRU_C = 8.0
N_EXPERTS = 16
EC_FACTOR = 2

TS = 256
LANE = 128
SUB = 8
VMEM_LIMIT = 56 * 1024 * 1024


def _cp(*sem):
    return pltpu.CompilerParams(dimension_semantics=sem, vmem_limit_bytes=VMEM_LIMIT)


def _pick(n, prefs):
    for p in prefs:
        if n % p == 0:
            return p
    return n


def _sigmoid(x):
    return 1.0 / (1.0 + jnp.exp(-x))


def _dot(a, b):
    return jnp.dot(a, b, preferred_element_type=F32)


def _dot_nt(a, b):
    return lax.dot_general(a, b, (((1,), (1,)), ((), ())), preferred_element_type=F32)


def _dot_tn(a, b):
    return lax.dot_general(a, b, (((0,), (0,)), ((), ())), preferred_element_type=F32)


def _rms(x, w):
    return x * lax.rsqrt(jnp.mean(x * x, axis=-1, keepdims=True) + EPS) * w


def _ada_kernel(c_ref, w_ref, b_ref, o_ref):
    c = c_ref[...]
    a = (c * _sigmoid(c)).astype(BF16)
    o_ref[0] = _dot(a, w_ref[0].astype(BF16)) + b_ref[0]


def _ada(cond, ada_w, ada_b):
    L, D, D6 = ada_w.shape
    tn = _pick(D6, (1024, 512, 256, 128))
    return pl.pallas_call(
        _ada_kernel, grid=(L, D6 // tn),
        in_specs=[pl.BlockSpec((SUB, D), lambda l, j: (0, 0)),
                  pl.BlockSpec((1, D, tn), lambda l, j: (l, 0, j)),
                  pl.BlockSpec((1, 1, tn), lambda l, j: (l, 0, j))],
        out_specs=pl.BlockSpec((1, SUB, tn), lambda l, j: (l, 0, j)),
        out_shape=jax.ShapeDtypeStruct((L, SUB, D6), F32),
        compiler_params=_cp("parallel", "parallel"), name="ada",
    )(cond, ada_w, ada_b.reshape(L, 1, D6))


def _mod_map(l, k, ctx_row, has_ctx):
    if has_ctx:
        return lambda b, s: ((l * SUB + jnp.where(s == 0, ctx_row, b)) * 6 + k, 0, 0)
    return lambda b, s: ((l * SUB + b) * 6 + k, 0, 0)


def _ctx_lat_specs(D):
    return [pl.BlockSpec((1, TS, D), lambda b, s: (b, 0, 0)),
            pl.BlockSpec((1, TS, D), lambda b, s: (b, jnp.maximum(s - 1, 0), 0))]


def _ctx_lat_tile(ctx_ref, lat_ref):
    return jnp.where(pl.program_id(1) == 0, ctx_ref[0], lat_ref[0])


def _prenorm_kernel(ctx_ref, lat_ref, w_ref, sh_ref, sc_ref, o_ref):
    y = _rms(_ctx_lat_tile(ctx_ref, lat_ref), w_ref[...])
    o_ref[0] = (y * (1.0 + sc_ref[0]) + sh_ref[0]).astype(o_ref.dtype)


def _prenorm(ctx, lat, w, mod, l, ctx_row):
    B, N, D = lat.shape
    S = ctx.shape[1] + N
    vec = lambda k: pl.BlockSpec((1, 1, D), _mod_map(l, k, ctx_row, True))
    return pl.pallas_call(
        _prenorm_kernel, grid=(B, S // TS),
        in_specs=_ctx_lat_specs(D) + [pl.BlockSpec((1, D), lambda b, s: (0, 0)), vec(0), vec(1)],
        out_specs=pl.BlockSpec((1, TS, D), lambda b, s: (b, s, 0)),
        out_shape=jax.ShapeDtypeStruct((B, S, D), BF16),
        compiler_params=_cp("parallel", "parallel"), name="prenorm",
    )(ctx, lat, w.reshape(1, D), mod, mod)


def _resid_kernel(nxt_mode, split_x, x_ref, *rest):
    if split_x:
        x = _ctx_lat_tile(x_ref, rest[0])
        rest = rest[1:]
    else:
        x = x_ref[0]
    y_ref, wpost_ref, g_ref, *rest = rest
    xn = x + g_ref[0] * _rms(y_ref[0], wpost_ref[...])
    if nxt_mode is None:
        (xo_ref,) = rest
    elif nxt_mode == "mixer":
        wpre_ref, sh_ref, sc_ref, xo_ref, ho_ref = rest
    else:
        wpre_ref, sh_ref, sc_ref, router_ref, xo_ref, ho_ref, lo_ref = rest
    xo_ref[0] = xn
    if nxt_mode is not None:
        h = _rms(xn, wpre_ref[...]) * (1.0 + sc_ref[0]) + sh_ref[0]
        ho_ref[0] = h.astype(ho_ref.dtype)
        if nxt_mode == "experts":
            lo_ref[0] = _dot(h.astype(BF16), router_ref[...])


def _resid(x, y, wpost, mod, l, kg, ctx_row, x_off, nxt):
    B, Sy, D = y.shape
    split_x = isinstance(x, tuple)
    has_ctx = ctx_row is not None and (split_x or (x_off == 0 and x.shape[1] == Sy))
    vec = lambda ll, k: pl.BlockSpec((1, 1, D), _mod_map(ll, k, ctx_row, has_ctx))
    row = pl.BlockSpec((1, D), lambda b, s: (0, 0))
    tile = pl.BlockSpec((1, TS, D), lambda b, s: (b, s, 0))
    if split_x:
        in_specs = _ctx_lat_specs(D)
        args = list(x)
    else:
        in_specs = [pl.BlockSpec((1, TS, D), lambda b, s: (b, s + x_off, 0))]
        args = [x]
    in_specs += [tile, row, vec(l, kg)]
    args += [y, wpost.reshape(1, D), mod]
    out_specs = [tile]
    out_shape = [jax.ShapeDtypeStruct((B, Sy, D), F32)]
    mode = None
    if nxt is not None:
        wpre, ln, ksh, ksc, router = nxt
        in_specs += [row, vec(ln, ksh), vec(ln, ksc)]
        args += [wpre.reshape(1, D), mod, mod]
        out_specs.append(tile)
        if router is None:
            mode = "mixer"
            out_shape.append(jax.ShapeDtypeStruct((B, Sy, D), BF16))
        else:
            mode = "experts"
            in_specs.append(pl.BlockSpec((D, LANE), lambda b, s: (0, 0)))
            args.append(router)
            out_specs.append(pl.BlockSpec((1, TS, LANE), lambda b, s: (b, s, 0)))
            out_shape += [jax.ShapeDtypeStruct((B, Sy, D), F32), jax.ShapeDtypeStruct((B, Sy, LANE), F32)]
    out = pl.pallas_call(
        functools.partial(_resid_kernel, mode, split_x), grid=(B, Sy // TS),
        in_specs=in_specs, out_specs=out_specs, out_shape=out_shape,
        compiler_params=_cp("parallel", "parallel"), name="resid",
    )(*args)
    return out


def _mm_kernel(x_ref, w_ref, o_ref):
    o_ref[...] = _dot(x_ref[...].astype(BF16), w_ref[...]).astype(o_ref.dtype)


def _matmul(x, w, out_dtype):
    M, K = x.shape
    N = w.shape[1]
    tm = _pick(M, (1024, 512, 256))
    tn = _pick(N, (1024, 512, 384, 256, 128))
    return pl.pallas_call(
        _mm_kernel, grid=(M // tm, N // tn),
        in_specs=[pl.BlockSpec((tm, K), lambda i, j: (i, 0)),
                  pl.BlockSpec((K, tn), lambda i, j: (0, j))],
        out_specs=pl.BlockSpec((tm, tn), lambda i, j: (i, j)),
        out_shape=jax.ShapeDtypeStruct((M, N), out_dtype),
        compiler_params=_cp("parallel", "parallel"), name="matmul",
    )(x, w)


def _mm2_kernel(x1_ref, x2_ref, w1_ref, w2_ref, o_ref):
    o_ref[...] = _dot(x1_ref[...], w1_ref[...]) + _dot(x2_ref[...], w2_ref[...])


def _matmul2(x1, x2, w):
    M, K1 = x1.shape
    K2 = x2.shape[1]
    assert K1 == K2 and w.shape[0] == K1 + K2
    N = w.shape[1]
    tm = _pick(M, (1024, 512, 256))
    tn = _pick(N, (512, 256, 128))
    return pl.pallas_call(
        _mm2_kernel, grid=(M // tm, N // tn),
        in_specs=[pl.BlockSpec((tm, K1), lambda i, j: (i, 0)),
                  pl.BlockSpec((tm, K2), lambda i, j: (i, 0)),
                  pl.BlockSpec((K1, tn), lambda i, j: (0, j)),
                  pl.BlockSpec((K2, tn), lambda i, j: (1, j))],
        out_specs=pl.BlockSpec((tm, tn), lambda i, j: (i, j)),
        out_shape=jax.ShapeDtypeStruct((M, N), F32),
        compiler_params=_cp("parallel", "parallel"), name="matmul2",
    )(x1, x2, w, w)


def _log_sigmoid(x):
    return jnp.minimum(x, 0.0) - jnp.log1p(jnp.exp(-jnp.abs(x)))


def _mlstm_kernel(q_ref, k_ref, v_ref, gc_ref, gr_ref, bc_ref, br_ref, o_ref, C_ref, n_ref, m_ref):
    H, dk, dv, L = MLSTM_HEADS, MLSTM_DK, MLSTM_DV, TS
    d = pl.program_id(1)
    t = pl.program_id(2)

    @pl.when(t == 0)
    def _():
        C_ref[...] = jnp.zeros_like(C_ref)
        n_ref[...] = jnp.zeros_like(n_ref)
        m_ref[...] = jnp.zeros_like(m_ref)

    fwd = d == 0
    ri = lax.broadcasted_iota(jnp.int32, (L, L), 0)
    ci = lax.broadcasted_iota(jnp.int32, (L, L), 1)
    diff = (ci - ri) * (1 - 2 * d)
    causal = diff <= 0
    causal_f = jnp.where(causal, 1.0, 0.0)
    causal_t = jnp.where(diff >= 0, 1.0, 0.0)
    gc = gc_ref[0] + bc_ref[...]
    gr = gr_ref[0] + br_ref[...]
    lfc = _log_sigmoid(gc)
    lfr = _log_sigmoid(gr)
    hi = lax.Precision.HIGHEST
    bcol_all = jnp.dot(causal_f, lfc, precision=hi, preferred_element_type=F32)
    brow_all = jnp.dot(lfr, causal_t, precision=hi, preferred_element_type=F32)
    tot_all = jnp.sum(lfr, axis=-1, keepdims=True)
    sel = lambda a, b: jnp.where(fwd, a, b)
    scale = dk ** -0.5
    for h in range(H):
        ic = sel(gc[:, h:h + 1], gc[:, 2 * H + h:2 * H + h + 1])
        ir = sel(gr[h:h + 1], gr[2 * H + h:2 * H + h + 1])
        bc = sel(bcol_all[:, H + h:H + h + 1], bcol_all[:, 3 * H + h:3 * H + h + 1])
        br = sel(brow_all[H + h:H + h + 1], brow_all[3 * H + h:3 * H + h + 1])
        tot = sel(tot_all[H + h:H + h + 1], tot_all[3 * H + h:3 * H + h + 1])
        m = m_ref[h]
        dmat = jnp.where(causal, bc - br + ir, NEG)
        inter = bc + m
        mj = jnp.maximum(inter, jnp.max(dmat, axis=-1, keepdims=True))
        p = jnp.exp(dmat - mj)
        q = q_ref[0, :, h * dk:(h + 1) * dk]
        k = k_ref[0, :, h * dk:(h + 1) * dk]
        v = v_ref[0, :, h * dv:(h + 1) * dv]
        s = _dot_nt(q, k) * scale * p
        w_inter = jnp.exp(inter - mj)
        num = w_inter * _dot(q, C_ref[h].astype(BF16)) + _dot(s.astype(BF16), v)
        qn = jnp.sum(q.astype(F32) * n_ref[h], axis=-1, keepdims=True)
        den = w_inter * qn + jnp.sum(s, axis=-1, keepdims=True)
        o_ref[0, 0, :, h * dv:(h + 1) * dv] = num / jnp.maximum(jnp.abs(den), jnp.exp(-mj))
        g_c = tot - bc + ic
        g_r = tot - br + ir
        m_new = jnp.maximum(tot + m, jnp.max(g_r, axis=-1, keepdims=True))
        decay = jnp.exp(tot + m - m_new)
        wk = jnp.exp(g_c - m_new) * (k.astype(F32) * scale)
        C_ref[h] = decay * C_ref[h] + _dot_tn(wk.astype(BF16), v)
        n_ref[h] = decay * n_ref[h] + jnp.sum(wk, axis=0, keepdims=True)
        m_ref[h] = m_new


def _mlstm(P, gcol, grow, gate_b):
    B, S, _ = P.shape
    nc = S // TS
    chunk = lambda d, t: jnp.where(d == 0, t, jnp.where(t == 0, 0, nc - t))
    blk = lambda c: pl.BlockSpec((1, TS, MLSTM_QK), lambda b, d, t: (b, chunk(d, t), c))
    G = MLSTM_GATES
    return pl.pallas_call(
        _mlstm_kernel, grid=(B, 2, nc),
        in_specs=[blk(0), blk(1), blk(2),
                  pl.BlockSpec((1, TS, G), lambda b, d, t: (b, chunk(d, t), 0)),
                  pl.BlockSpec((1, G, TS), lambda b, d, t: (b, 0, chunk(d, t))),
                  pl.BlockSpec((1, G), lambda b, d, t: (0, 0)),
                  pl.BlockSpec((G, 1), lambda b, d, t: (0, 0))],
        out_specs=pl.BlockSpec((1, 1, TS, MLSTM_V), lambda b, d, t: (d, b, chunk(d, t), 0)),
        out_shape=jax.ShapeDtypeStruct((2, B, S, MLSTM_V), F32),
        scratch_shapes=[pltpu.VMEM((MLSTM_HEADS, MLSTM_DK, MLSTM_DV), F32),
                        pltpu.VMEM((MLSTM_HEADS, 1, MLSTM_DK), F32),
                        pltpu.VMEM((MLSTM_HEADS, 1, 1), F32)],
        compiler_params=_cp("parallel", "arbitrary", "arbitrary"), name="mlstm",
    )(P, P, P, gcol, grow, gate_b.reshape(1, G), gate_b.reshape(G, 1))


def _mlstm_out_kernel(hf_ref, hr_ref, o_ref, gain_ref, out_ref):
    dv = MLSTM_DV
    for h in range(MLSTM_HEADS):
        sl = slice(h * dv, (h + 1) * dv)
        x = hf_ref[0, 0, :, sl] + hr_ref[0, 0, :, sl]
        out_ref[0, :, sl] = (_rms(x, gain_ref[:, sl]) * _sigmoid(o_ref[0, :, sl])).astype(out_ref.dtype)


def _mlstm_out(Hd, OG, gain):
    _, B, S, V = Hd.shape
    return pl.pallas_call(
        _mlstm_out_kernel, grid=(B, S // TS),
        in_specs=[pl.BlockSpec((1, 1, TS, V), lambda b, s: (0, b, s, 0)),
                  pl.BlockSpec((1, 1, TS, V), lambda b, s: (1, b, s, 0)),
                  pl.BlockSpec((1, TS, V), lambda b, s: (b, s, 0)),
                  pl.BlockSpec((1, V), lambda b, s: (0, 0))],
        out_specs=pl.BlockSpec((1, TS, V), lambda b, s: (b, s, 0)),
        out_shape=jax.ShapeDtypeStruct((B, S, V), BF16),
        compiler_params=_cp("parallel", "parallel"), name="mlstm_out",
    )(Hd, Hd, OG, gain.reshape(1, V))


NA_QR = 4
NA_KR = NA_KH + 2 * NA_QR - 4


def _na_bias(rpb):
    a = np.arange(NA_QR)
    kr = np.arange(NA_KR)
    qrel = NA_QR * np.arange(3)[:, None] + a[None, :]
    r0rel = np.stack([0 * a, a, NA_KR - NA_KH + 0 * a])
    valid_r = (kr[None, None] >= r0rel[..., None]) & (kr[None, None] < r0rel[..., None] + NA_KH)
    drow = np.clip(kr[None, None] - qrel[..., None] + (NA_KH - 1), 0, 2 * NA_KH - 2)
    w = np.arange(GRID_W)[:, None]
    cc = np.arange(GRID_W)[None, :]
    col_start = np.clip(w - NA_KW // 2, 0, GRID_W - NA_KW)
    valid_c = (cc >= col_start) & (cc < col_start + NA_KW)
    dcol = np.clip(cc - w + (NA_KW - 1), 0, 2 * NA_KW - 2)
    pick_r = jnp.asarray(np.eye(2 * NA_KH - 1, dtype=np.float32)[drow])
    pick_c = jnp.asarray(np.eye(2 * NA_KW - 1, dtype=np.float32)[dcol])
    t = jnp.einsum("cakr,hrs,wzs->chawkz", pick_r, rpb.astype(F32), pick_c,
                   precision=lax.Precision.HIGHEST)
    valid = valid_r[:, None, :, None, :, None] & valid_c[None, None, None, :, None, :]
    t = jnp.where(valid, t, NEG)
    return t.reshape(3, NA_HEADS, NA_QR * GRID_W, NA_KR * GRID_W)


def _na_kernel(q_ref, k_ref, v_ref, bias_ref, o_ref, *, Lc, rows_n):
    W, d = GRID_W, NA_DIM
    scale = d ** -0.5
    kc = k_ref[0, 0:Lc, :]
    vc = v_ref[0, 0:Lc, :]
    sc = _dot_nt(q_ref[0, 0:Lc, :], kc) * scale
    pc = jnp.exp(sc - jnp.max(sc, axis=-1, keepdims=True))
    oc = _dot(pc.astype(BF16), vc) / jnp.sum(pc, axis=-1, keepdims=True)
    o_ref[0, 0:Lc, :] = oc.astype(o_ref.dtype)

    def group(gi, carry):
        r = gi * NA_QR
        kr0 = jnp.clip(r - NA_KH // 2, 0, rows_n - NA_KR)
        qoff = pl.multiple_of(Lc + r * W, NA_QR * W)
        koff = pl.multiple_of(Lc + kr0 * W, W)
        qg = q_ref[0, pl.ds(qoff, NA_QR * W), :]
        kw = k_ref[0, pl.ds(koff, NA_KR * W), :]
        vw = v_ref[0, pl.ds(koff, NA_KR * W), :]
        sw = _dot_nt(qg, kw) * scale + bias_ref[(r - kr0) // NA_QR, 0]
        sx = _dot_nt(qg, kc) * scale
        m = jnp.maximum(jnp.max(sw, axis=-1, keepdims=True), jnp.max(sx, axis=-1, keepdims=True))
        pw = jnp.exp(sw - m)
        px = jnp.exp(sx - m)
        l = jnp.sum(pw, axis=-1, keepdims=True) + jnp.sum(px, axis=-1, keepdims=True)
        o = (_dot(pw.astype(BF16), vw) + _dot(px.astype(BF16), vc)) / l
        o_ref[0, pl.ds(qoff, NA_QR * W), :] = o.astype(o_ref.dtype)
        return carry

    lax.fori_loop(0, rows_n // NA_QR, group, 0, unroll=2)


def _na(P, bias, Lc, col0):
    B, S, _ = P.shape
    rows_n = (S - Lc) // GRID_W
    assert rows_n % NA_QR == 0 and rows_n >= NA_KR
    c0 = col0 // NA_DIM
    blk = lambda c: pl.BlockSpec((1, S, NA_DIM), lambda b, h: (b, 0, c0 + c * NA_HEADS + h))
    return pl.pallas_call(
        functools.partial(_na_kernel, Lc=Lc, rows_n=rows_n), grid=(B, NA_HEADS),
        in_specs=[blk(0), blk(1), blk(2),
                  pl.BlockSpec((3, 1, NA_QR * GRID_W, NA_KR * GRID_W), lambda b, h: (0, h, 0, 0))],
        out_specs=pl.BlockSpec((1, S, NA_DIM), lambda b, h: (b, 0, h)),
        out_shape=jax.ShapeDtypeStruct((B, S, NA_WIDTH), BF16),
        compiler_params=_cp("parallel", "parallel"), name="na",
    )(P, P, P, bias)


def _rope_tables(Lc, N):
    quarter = GQA_DIM // 4
    t = jnp.arange(N)
    inv = ROPE_THETA ** (-(jnp.arange(quarter, dtype=F32) / quarter))
    ang_r = (t // GRID_W).astype(F32)[:, None] * inv[None, :]
    ang_c = (t % GRID_W).astype(F32)[:, None] * inv[None, :]
    cos = jnp.concatenate([jnp.cos(ang_r)] * 2 + [jnp.cos(ang_c)] * 2, axis=-1)
    sin = jnp.concatenate([-jnp.sin(ang_r), jnp.sin(ang_r), -jnp.sin(ang_c), jnp.sin(ang_c)], axis=-1)
    cos = jnp.concatenate([jnp.ones((Lc, GQA_DIM), F32), cos], axis=0)
    sin = jnp.concatenate([jnp.zeros((Lc, GQA_DIM), F32), sin], axis=0)
    return jnp.tile(cos, (1, LANE // GQA_DIM)), jnp.tile(sin, (1, LANE // GQA_DIM))


def _rope_rot(x, cos, sin):
    quarter = GQA_DIM // 4
    lane = lax.broadcasted_iota(jnp.int32, x.shape, 1)
    first = (lane % (2 * quarter)) < quarter
    rot = jnp.where(first, pltpu.roll(x, LANE - quarter, 1), pltpu.roll(x, quarter, 1))
    return x * cos + rot * sin


def _gqa_kernel(sink_ref, q_ref, k_ref, v_ref, cos_ref, sin_ref, o_ref, kd_ref, vd_ref, *, Lc, N):
    W, dh, KV = GQA_WINDOW, GQA_DIM, GQA_KV_HEADS
    G = GQA_HEADS // KV
    S = Lc + N
    i = pl.program_id(1)
    scale = dh ** -0.5

    @pl.when(i == 0)
    def _():
        low_t = lax.broadcasted_iota(jnp.int32, (TS, LANE), 1) < dh
        zero = jnp.zeros((W, LANE), BF16)
        for kh in range(KV):
            for ref in (kd_ref, vd_ref):
                ref[kh, Lc:Lc + W, :] = zero
                ref[kh, Lc + W + N:Lc + 2 * W + N, :] = zero

        def build(c, carry):
            r0 = pl.multiple_of(c * TS, TS)
            dst = pl.multiple_of(r0 + jnp.where(c >= Lc // TS, W, 0), W)
            cs = cos_ref[pl.ds(r0, TS), :]
            sn = sin_ref[pl.ds(r0, TS), :]
            for hh in range(KV // 2):
                kr = _rope_rot(k_ref[0, pl.ds(r0, TS), hh * LANE:(hh + 1) * LANE], cs, sn)
                vx = v_ref[0, pl.ds(r0, TS), hh * LANE:(hh + 1) * LANE]
                for ref, a in ((kd_ref, kr), (vd_ref, vx)):
                    sw = pltpu.roll(a, dh, 1)
                    ref[2 * hh, pl.ds(dst, TS), :] = jnp.where(low_t, a, sw).astype(BF16)
                    ref[2 * hh + 1, pl.ds(dst, TS), :] = jnp.where(low_t, sw, a).astype(BF16)
            return carry

        lax.fori_loop(0, S // TS, build, 0)

    span = 3 * W
    off = pl.multiple_of(Lc + i * W, W)
    cq = cos_ref[pl.ds(off, W), :]
    sq = sin_ref[pl.ds(off, W), :]
    ii = lax.broadcasted_iota(jnp.int32, (G * W, span), 0) % W
    jj = lax.broadcasted_iota(jnp.int32, (G * W, span), 1)
    kpos = i * W - W + jj
    valid = (kpos >= 0) & (kpos < N) & (jnp.abs(ii - (jj - W)) <= W)
    rg = lax.broadcasted_iota(jnp.int32, (G * W, 1), 0) // W
    low = lax.broadcasted_iota(jnp.int32, (W, LANE), 1) < dh
    for kh in range(KV):
        parts = []
        for slab in range(G // 2):
            j = kh * (G // 2) + slab
            qr = _rope_rot(q_ref[0, :, j * LANE:(j + 1) * LANE] * scale, cq, sq)
            parts += [jnp.where(low, qr, 0.0).astype(BF16), jnp.where(low, 0.0, qr).astype(BF16)]
        q4 = jnp.concatenate(parts, axis=0)
        kw = kd_ref[kh, pl.ds(off, span), :]
        vw = vd_ref[kh, pl.ds(off, span), :]
        kc = kd_ref[kh, 0:Lc, :]
        vc = vd_ref[kh, 0:Lc, :]
        sw = jnp.where(valid, _dot_nt(q4, kw), NEG)
        sx = _dot_nt(q4, kc)
        sk = sink_ref[G * kh + G - 1]
        for g in range(G - 2, -1, -1):
            sk = jnp.where(rg == g, sink_ref[G * kh + g], sk)
        m = jnp.maximum(jnp.maximum(jnp.max(sw, axis=-1, keepdims=True),
                                    jnp.max(sx, axis=-1, keepdims=True)), sk)
        pw = jnp.exp(sw - m)
        px = jnp.exp(sx - m)
        l = jnp.sum(pw, axis=-1, keepdims=True) + jnp.sum(px, axis=-1, keepdims=True) + jnp.exp(sk - m)
        o = (_dot(pw.astype(BF16), vw) + _dot(px.astype(BF16), vc)) / l
        for slab in range(G // 2):
            j = kh * (G // 2) + slab
            o_ref[0, :, j * LANE:(j + 1) * LANE] = jnp.where(
                low, o[2 * slab * W:(2 * slab + 1) * W], o[(2 * slab + 1) * W:(2 * slab + 2) * W]
            ).astype(o_ref.dtype)


def _gqa(P1, cos, sin, sink, Lc):
    B, S, _ = P1.shape
    N = S - Lc
    W = GQA_WINDOW
    rows = Lc + 2 * W + N
    table = pl.BlockSpec((S, LANE), lambda b, i: (0, 0))
    return pl.pallas_call(
        functools.partial(_gqa_kernel, Lc=Lc, N=N), grid=(B, N // W),
        in_specs=[pl.BlockSpec(memory_space=pltpu.SMEM),
                  pl.BlockSpec((1, W, GQA_Q), lambda b, i: (b, i + Lc // W, 0)),
                  pl.BlockSpec((1, S, GQA_KV), lambda b, i: (b, 0, GQA_Q // GQA_KV)),
                  pl.BlockSpec((1, S, GQA_KV), lambda b, i: (b, 0, GQA_Q // GQA_KV + 1)),
                  table, table],
        out_specs=pl.BlockSpec((1, W, GQA_Q), lambda b, i: (b, i, 0)),
        out_shape=jax.ShapeDtypeStruct((B, N, GQA_Q), BF16),
        scratch_shapes=[pltpu.VMEM((GQA_KV_HEADS, rows, LANE), BF16),
                        pltpu.VMEM((GQA_KV_HEADS, rows, LANE), BF16)],
        compiler_params=_cp("parallel", "arbitrary"), name="gqa",
    )(sink, P1, P1, P1, cos, sin)


def _gelu_tanh(x):
    return 0.5 * x * (1.0 + jnp.tanh(np.sqrt(2.0 / np.pi) * (x + 0.044715 * (x * x * x))))


def _lru_kernel(x_ref, g_ref, cw_ref, cb_ref, gw_ref, gb_ref, lam_ref, o_ref,
                af_ref, bf_ref, ar_ref, br_ref, *, Lc, N):
    S = Lc + N
    CH = 128
    LW = x_ref.shape[-1]
    nb = LW // LRU_BS
    nt = S // SUB
    ntc = Lc // SUB
    row = lax.broadcasted_iota(jnp.int32, (CH, LW), 0)
    crow = row % SUB
    first_row = row == 0
    last_row = row == CH - 1
    last_row2 = row == CH - 2
    sps = []
    for d in range(2):
        lam = lam_ref[d:d + 1, :]
        sps.append(jnp.maximum(-lam, 0.0) + jnp.log1p(jnp.exp(-jnp.abs(lam))))

    def gates(c, carry):
        off = pl.multiple_of(c * CH, CH)
        xc = x_ref[0, pl.ds(off, CH), :]
        prev = x_ref[0, pl.ds(pl.multiple_of(jnp.maximum(off - SUB, 0), SUB), SUB), :]
        nxt = x_ref[0, pl.ds(pl.multiple_of(jnp.minimum(off + CH, S - SUB), SUB), SUB), :]
        pz = jnp.where((c == 0) | (c == Lc // CH), 0.0, 1.0)
        nz = jnp.where((c == Lc // CH - 1) | (c == S // CH - 1), 0.0, 1.0)
        p7 = prev[SUB - 1:SUB, :] * pz
        n0 = nxt[0:1, :] * nz
        n1 = nxt[1:2, :] * nz
        xm1 = jnp.where(first_row, p7, pltpu.roll(xc, 1, 0))
        xp1 = jnp.where(last_row, n0, pltpu.roll(xc, CH - 1, 0))
        xp2 = jnp.where(last_row2, n0, jnp.where(last_row, n1, pltpu.roll(xc, CH - 2, 0)))
        u = (xm1 * cw_ref[0:1, :] + cb_ref[...] + xc * cw_ref[1:2, :] + xp1 * cw_ref[2:3, :]
             + xp2 * cw_ref[3:4, :])
        ub = u.astype(BF16)
        for d, (a_ref, b_ref) in enumerate(((af_ref, bf_ref), (ar_ref, br_ref))):
            pre = []
            for gi in range(2):
                pre.append(jnp.concatenate(
                    [_dot(ub[:, k * LRU_BS:(k + 1) * LRU_BS], gw_ref[d, gi, k].astype(BF16))
                     for k in range(nb)], axis=-1) + gb_ref[d, gi:gi + 1, :])
            r = _sigmoid(pre[0])
            ig = _sigmoid(pre[1])
            a = jnp.exp(-LRU_C * r * sps[d])
            b = jnp.sqrt(1.0 - a * a) * (ig * u)
            for s in (1, 2, 4):
                if d == 0:
                    keep = crow >= s
                    a_sh = pltpu.roll(a, s, 0)
                    b_sh = pltpu.roll(b, s, 0)
                else:
                    keep = crow < SUB - s
                    a_sh = pltpu.roll(a, CH - s, 0)
                    b_sh = pltpu.roll(b, CH - s, 0)
                b = jnp.where(keep, a * b_sh + b, b)
                a = jnp.where(keep, a * a_sh, a)
            a_ref[pl.ds(off, CH), :] = a
            b_ref[pl.ds(off, CH), :] = b
        return carry

    lax.fori_loop(0, S // CH, gates, 0)

    def scan(ft, rt, carry):
        hf, hr = carry
        of = pl.multiple_of(ft * SUB, SUB)
        orv = pl.multiple_of(rt * SUB, SUB)
        tf = af_ref[pl.ds(of, SUB), :] * hf + bf_ref[pl.ds(of, SUB), :]
        tr = ar_ref[pl.ds(orv, SUB), :] * hr + br_ref[pl.ds(orv, SUB), :]
        bf_ref[pl.ds(of, SUB), :] = tf
        br_ref[pl.ds(orv, SUB), :] = tr
        return (jnp.broadcast_to(tf[SUB - 1:SUB, :], (SUB, LW)), jnp.broadcast_to(tr[0:1, :], (SUB, LW)))

    zero = jnp.zeros((SUB, LW), F32)
    carry = lax.fori_loop(0, ntc, lambda i, c: scan(i, ntc - 1 - i, c), (zero, zero), unroll=8)
    lax.fori_loop(ntc, nt, lambda i, c: scan(i, nt - 1 - (i - ntc), c), carry, unroll=8)

    def outp(c, carry):
        off = pl.multiple_of(c * CH, CH)
        src = pl.multiple_of(Lc + off, CH)
        hs = bf_ref[pl.ds(src, CH), :] + br_ref[pl.ds(src, CH), :]
        o_ref[0, pl.ds(off, CH), :] = (hs * _gelu_tanh(g_ref[0, pl.ds(src, CH), :])).astype(o_ref.dtype)
        return carry

    lax.fori_loop(0, N // CH, outp, 0)


def _lru(P1, col_x, col_g, conv_w, conv_b, gate_w, gate_b, lam, Lc):
    B, S, _ = P1.shape
    N = S - Lc
    nb = 2
    lw = nb * LRU_BS
    return pl.pallas_call(
        functools.partial(_lru_kernel, Lc=Lc, N=N), grid=(B, LRU_BLOCKS // nb),
        in_specs=[pl.BlockSpec((1, S, lw), lambda b, j: (b, 0, col_x // lw + j)),
                  pl.BlockSpec((1, S, lw), lambda b, j: (b, 0, col_g // lw + j)),
                  pl.BlockSpec((4, lw), lambda b, j: (0, j)),
                  pl.BlockSpec((1, lw), lambda b, j: (0, j)),
                  pl.BlockSpec((2, 2, nb, LRU_BS, LRU_BS), lambda b, j: (0, 0, j, 0, 0)),
                  pl.BlockSpec((2, 2, lw), lambda b, j: (0, 0, j)),
                  pl.BlockSpec((2, lw), lambda b, j: (0, j))],
        out_specs=pl.BlockSpec((1, N, lw), lambda b, j: (b, 0, j)),
        out_shape=jax.ShapeDtypeStruct((B, N, LRU_WIDTH), BF16),
        scratch_shapes=[pltpu.VMEM((S, lw), F32) for _ in range(4)],
        compiler_params=_cp("parallel", "parallel"), name="lru",
    )(P1, P1, conv_w, conv_b.reshape(1, LRU_WIDTH), gate_w, gate_b, lam)


ROUTE_BISECT = 2048


def _cumsum_lanes(x):
    n = x.shape[-1]
    lane = lax.broadcasted_iota(jnp.int32, x.shape, 1)
    s = 1
    while s < n:
        x = x + jnp.where(lane >= s, pltpu.roll(x, s, 1), 0.0)
        s *= 2
    return x


def _route_kernel(lt_ref, idx_ref, g_ref, cm_ref, af_ref, *, segs, S):
    E = lt_ref.shape[1]
    b = pl.program_id(0)
    lt = lt_ref[0]
    ex = jnp.exp(lt - jnp.max(lt, axis=0, keepdims=True))
    aff = ex / jnp.sum(ex, axis=0, keepdims=True)
    lane_e = lax.broadcasted_iota(jnp.int32, (1, LANE), 1)
    slot0 = 0
    for lo, n, cap in segs:
        a = aff[:, lo:lo + n]

        def bisect(carry):
            lo_v, hi_v, c_lo, it, _ = carry
            mid = lo_v + 0.5 * (hi_v - lo_v)
            c = jnp.sum(jnp.where(a >= mid, 1.0, 0.0), axis=1, keepdims=True)
            ge = c >= cap
            lo_v = jnp.where(ge, mid, lo_v)
            hi_v = jnp.where(ge, hi_v, mid)
            c_lo = jnp.where(ge, c, c_lo)
            mid = lo_v + 0.5 * (hi_v - lo_v)
            done = (c_lo == cap) | (mid <= lo_v) | (mid >= hi_v)
            return lo_v, hi_v, c_lo, it + 1, jnp.min(jnp.where(done, 1, 0))

        lo_v, hi_v, _, _, _ = lax.while_loop(
            lambda carry: (carry[4] == 0) & (carry[3] < ROUTE_BISECT), bisect,
            (jnp.zeros((E, 1), F32), jnp.full((E, 1), 2.0, F32), jnp.full((E, 1), float(n), F32),
             jnp.int32(0), jnp.int32(0)))
        top = jnp.where(a >= hi_v, 1.0, 0.0)
        tie = jnp.where((a >= lo_v) & (a < hi_v), 1.0, 0.0)
        need = cap - jnp.sum(top, axis=1, keepdims=True)
        tie_rank = _cumsum_lanes(tie) - tie
        sel = top + tie * jnp.where(tie_rank < need, 1.0, 0.0)
        cm_ref[:, 0:n] = _cumsum_lanes(sel) * sel
        af_ref[:, 0:n] = a
        tok = (lax.broadcasted_iota(jnp.int32, (1, n), 1) + (lo + b * S)).astype(F32)
        jcol = (lax.broadcasted_iota(jnp.int32, (cap, 1), 0) + 1).astype(F32)

        def per_expert(e, carry):
            oi, og = carry
            ce = cm_ref[pl.ds(e, 1), 0:n]
            ae = af_ref[pl.ds(e, 1), 0:n]
            acc_i = jnp.zeros((cap, LANE), F32)
            acc_g = jnp.zeros((cap, LANE), F32)
            for k in range(n // LANE):
                sl = slice(k * LANE, (k + 1) * LANE)
                hit = ce[:, sl] == jcol
                acc_i = acc_i + jnp.where(hit, tok[:, sl], 0.0)
                acc_g = acc_g + jnp.where(hit, ae[:, sl], 0.0)
            mine = lane_e == e
            return (jnp.where(mine, jnp.sum(acc_i, axis=1, keepdims=True), oi),
                    jnp.where(mine, jnp.sum(acc_g, axis=1, keepdims=True), og))

        zero = jnp.zeros((cap, LANE), F32)
        oi, og = lax.fori_loop(0, E, per_expert, (zero, zero))
        idx_ref[0, slot0:slot0 + cap, :] = oi
        g_ref[0, slot0:slot0 + cap, :] = og
        slot0 += cap


def _route(logits_t, segs):
    B, E, S = logits_t.shape
    slots = sum(cap for _, _, cap in segs)
    nmax = max(n for _, n, _ in segs)
    out = jax.ShapeDtypeStruct((B, slots, LANE), F32)
    blk = pl.BlockSpec((1, slots, LANE), lambda b: (b, 0, 0))
    return pl.pallas_call(
        functools.partial(_route_kernel, segs=segs, S=S), grid=(B,),
        in_specs=[pl.BlockSpec((1, E, S), lambda b: (b, 0, 0))],
        out_specs=[blk, blk], out_shape=[out, out],
        scratch_shapes=[pltpu.VMEM((E, nmax), F32), pltpu.VMEM((E, nmax), F32)],
        compiler_params=_cp("parallel"), name="route",
    )(logits_t)


RING = 4


def _ffn_kernel(ids_ref, h_hbm, w1_ref, w3_ref, w2_ref, g_ref, zero_hbm, out_hbm,
                xf_ref, xb_ref, y_ref, ring_ref, xsem, asem, ssem, *, rm, ns, nf):
    del zero_hbm
    C = rm // nf
    D = xf_ref.shape[-1]
    step = pl.program_id(0)
    f = pl.program_id(1)
    q = step * nf + f
    cur = step % 2
    f0 = pl.multiple_of(f * C, SUB)

    def chunk_tokens(c_step, c_f):
        p = c_step - 1
        return jnp.where(p < 0, rm, jnp.minimum(p, ns - 1) * rm) + c_f * C

    def gather(src, dst, sem, tb, row0, n):
        for i in range(n):
            pltpu.make_async_copy(src.at[pl.ds(ids_ref[tb + i], 1), :], dst.at[pl.ds(row0 + i, 1), :], sem).start()

    def scatter(src, tb, n, sem):
        for i in range(n):
            pltpu.make_async_copy(src.at[pl.ds(i, 1), :], out_hbm.at[pl.ds(ids_ref[tb + i], 1), :], sem).start()

    def wait_rows(buf, sem):
        pltpu.make_async_copy(buf, buf, sem).wait()

    chunk = ring_ref.at[0]
    tb_prev = jnp.where(f == 0, chunk_tokens(step - 1, nf - 1), chunk_tokens(step, f - 1))
    tb_next = jnp.where(f == nf - 1, chunk_tokens(step + 1, 0), chunk_tokens(step, f + 1))

    @pl.when(q == 0)
    def _():
        ring_ref[...] = jnp.zeros_like(ring_ref)
        y_ref[1] = jnp.zeros((rm, D), F32)

        def first(r, carry):
            pltpu.make_async_copy(h_hbm.at[pl.ds(ids_ref[r], 1), :], xf_ref.at[pl.ds(r, 1), :], xsem).start()
            return carry
        lax.fori_loop(0, rm, first, 0, unroll=8)
        scatter(y_ref.at[1, pl.ds(0, C)], chunk_tokens(0, nf - 2), C, ssem.at[0])
        scatter(y_ref.at[1, pl.ds(0, C)], chunk_tokens(0, nf - 3), C, ssem.at[1])
        gather(out_hbm, ring_ref.at[0], asem.at[0], chunk_tokens(0, 0), 0, C)

    @pl.when(f == 0)
    def _():
        wait_rows(xf_ref, xsem)
        xb_ref[...] = xf_ref[...].astype(BF16)
        y_ref[cur] = jnp.zeros((rm, D), F32)

    wait_rows(chunk, ssem.at[q % 3])
    scatter(ring_ref.at[(q + RING - 1) % RING], tb_prev, C, ssem.at[(q + 2) % 3])
    gather(out_hbm, ring_ref.at[(q + 1) % RING], asem.at[(q + 1) % 2], tb_next, 0, C)
    gather(h_hbm, xf_ref, xsem, jnp.minimum(step + 1, ns - 1) * rm + f * C, f0, C)

    x = xb_ref[...]
    u = _dot(x, w1_ref[0, 0].astype(BF16))
    v = _dot(x, w3_ref[0, 0].astype(BF16))
    a = (u * _sigmoid(u) * v).astype(BF16)
    dc = _pick(D, (512, 256, 128))
    for c in range(D // dc):
        y_ref[cur, :, c * dc:(c + 1) * dc] += _dot(a, w2_ref[0, 0, :, c * dc:(c + 1) * dc].astype(BF16))

    slot = q % RING
    wait_rows(chunk, asem.at[q % 2])
    mine = lax.broadcasted_iota(jnp.int32, (C, LANE), 1) == jnp.clip(step - 1, 0, ns - 1) // 2
    gate = jnp.sum(jnp.where(mine, g_ref[pl.ds(f0, C), :], 0.0), axis=-1, keepdims=True)
    ring_ref[slot] = ring_ref[slot] + y_ref[1 - cur, pl.ds(f0, C), :] * gate

    @pl.when(q == (ns + 1) * nf - 1)
    def _():
        scatter(ring_ref.at[slot], chunk_tokens(step, f), C, ssem.at[q % 3])
        for k in range(3):
            wait_rows(chunk, ssem.at[k])
        wait_rows(chunk, asem.at[(q + 1) % 2])
        wait_rows(xf_ref, xsem)


def _ffn(hf, ids, g, w1, w3, w2, l):
    T, D = hf.shape
    E = w1.shape[1]
    R = g.shape[0]
    FF = w1.shape[-1]
    rm = R // 2
    ns = 2 * E
    tf = _pick(FF, (256, 128))
    nf = FF // tf
    assert nf >= 3 and rm % nf == 0 and (rm // nf) % SUB == 0
    ex = lambda s: jnp.minimum(s, ns - 1) // 2
    grid_spec = pltpu.PrefetchScalarGridSpec(
        num_scalar_prefetch=1, grid=(ns + 1, nf),
        in_specs=[pl.BlockSpec(memory_space=pl.ANY),
                  pl.BlockSpec((1, 1, D, tf), lambda s, f, ids: (l, ex(s), 0, f)),
                  pl.BlockSpec((1, 1, D, tf), lambda s, f, ids: (l, ex(s), 0, f)),
                  pl.BlockSpec((1, 1, tf, D), lambda s, f, ids: (l, ex(s), f, 0)),
                  pl.BlockSpec((rm, LANE), lambda s, f, ids: (jnp.maximum(s - 1, 0) % 2, 0)),
                  pl.BlockSpec(memory_space=pl.ANY)],
        out_specs=pl.BlockSpec(memory_space=pl.ANY),
        scratch_shapes=[pltpu.VMEM((rm, D), F32), pltpu.VMEM((rm, D), BF16),
                        pltpu.VMEM((2, rm, D), F32), pltpu.VMEM((RING, rm // nf, D), F32),
                        pltpu.SemaphoreType.DMA, pltpu.SemaphoreType.DMA((2,)), pltpu.SemaphoreType.DMA((3,))])
    return pl.pallas_call(
        functools.partial(_ffn_kernel, rm=rm, ns=ns, nf=nf), grid_spec=grid_spec,
        out_shape=jax.ShapeDtypeStruct((T, D), F32),
        input_output_aliases={6: 0},
        compiler_params=_cp("arbitrary", "arbitrary"), name="ffn",
    )(ids, hf, w1, w3, w2, g, jnp.zeros((T, D), F32))


def _moe(h, logits, w1, w3, w2, l, Lc):
    B, S, D = h.shape
    E = N_EXPERTS
    assert B % 2 == 0
    segs = tuple((lo, n, EC_FACTOR * n // E) for lo, n in ((Lc, S - Lc), (0, Lc)) if n)
    idx, g = _route(jnp.swapaxes(logits[:, :, :E], 1, 2), segs)
    ids = jnp.transpose(idx[:, :, :E], (2, 0, 1)).astype(jnp.int32).reshape(-1)
    return _ffn(h.reshape(B * S, D), ids, g.reshape(-1, LANE), w1, w3, w2, l).reshape(B, S, D)


def kernel(x, c, ctx, c_ctx, ada_w, ada_b, norm_mix_pre, norm_mix_post, norm_ffn_pre, norm_ffn_post,
           ab_w_in, ab_w_out, mlstm_gate_b, mlstm_norm, na_rpb,
           cd_w_in, cd_w_out, gqa_sink, lru_conv_w, lru_conv_b, lru_gate_w, lru_gate_b, lru_lambda,
           moe_router, moe_w1, moe_w3, moe_w2):
    B, N, D = x.shape
    Lc = ctx.shape[1]
    S = Lc + N
    assert Lc == TS and N % TS == 0 and B < SUB
    ctx_row = B
    cond = jnp.concatenate([c, c_ctx[None], jnp.zeros((SUB - B - 1, D), F32)], axis=0)
    mod = _ada(cond, ada_w, ada_b).reshape(-1, 1, D)
    router = jnp.pad(moe_router, ((0, 0), (0, 0), (0, LANE - N_EXPERTS))).astype(BF16)

    h = _prenorm(ctx, x, norm_mix_pre[0], mod, 0, ctx_row)
    hf = h.reshape(B * S, D)
    w_in = ab_w_in[0]
    o0 = 3 * MLSTM_QK
    n0 = o0 + MLSTM_V + MLSTM_GATES
    w_og = jnp.pad(w_in[:, o0:n0], ((0, 0), (0, LANE - MLSTM_GATES))).astype(BF16)
    P = _matmul(hf, w_in[:, :o0].astype(BF16), BF16).reshape(B, S, -1)
    PN = _matmul(hf, w_in[:, n0:].astype(BF16), BF16).reshape(B, S, -1)
    OG = _matmul(hf, w_og, F32).reshape(B, S, -1)
    gcol = OG[:, :, MLSTM_V:MLSTM_V + MLSTM_GATES]
    Hd = _mlstm(P, gcol, jnp.swapaxes(gcol, 1, 2), mlstm_gate_b[0])
    ML = _mlstm_out(Hd, OG, mlstm_norm[0])
    NL = _na(PN, _na_bias(na_rpb[0]), Lc, 0)
    Y = _matmul2(ML.reshape(B * S, -1), NL.reshape(B * S, -1), ab_w_out[0].astype(BF16)).reshape(B, S, D)
    xs, hb, lg = _resid((ctx, x), Y, norm_mix_post[0], mod, 0, 2, ctx_row, 0, (norm_ffn_pre[0], 0, 3, 4, router[0]))
    Y = _moe(hb, lg, moe_w1, moe_w3, moe_w2, 0, Lc)
    xs, h = _resid(xs, Y, norm_ffn_post[0], mod, 0, 5, ctx_row, 0, (norm_mix_pre[1], 1, 0, 1, None))

    P1 = _matmul(h.reshape(B * S, D), cd_w_in[0].astype(BF16), F32).reshape(B, S, -1)
    cos, sin = _rope_tables(Lc, N)
    AL = _gqa(P1, cos, sin, gqa_sink[0], Lc)
    RL = _lru(P1, GQA_Q + 2 * GQA_KV, GQA_Q + 2 * GQA_KV + LRU_WIDTH,
              lru_conv_w[0], lru_conv_b[0], lru_gate_w[0], lru_gate_b[0], lru_lambda[0], Lc)
    Y = _matmul2(AL.reshape(B * N, -1), RL.reshape(B * N, -1), cd_w_out[0].astype(BF16)).reshape(B, N, D)
    xl, hb, lg = _resid(xs, Y, norm_mix_post[1], mod, 1, 2, None, Lc // TS, (norm_ffn_pre[1], 1, 3, 4, router[1]))
    Y = _moe(hb, lg, moe_w1, moe_w3, moe_w2, 1, 0)
    (xl,) = _resid(xl, Y, norm_ffn_post[1], mod, 1, 5, None, 0, None)
    return xl
```

```python
import functools

import numpy as np
import jax
import jax.numpy as jnp
from jax import lax
from jax.experimental import pallas as pl
from jax.experimental.pallas import tpu as pltpu

F32 = jnp.float32
BF16 = jnp.bfloat16
EPS = 1e-6
NEG = -1e30

GRID_W = 64
MLSTM_HEADS = 4
MLSTM_DK = 256
MLSTM_DV = 256
MLSTM_QK = MLSTM_HEADS * MLSTM_DK
MLSTM_V = MLSTM_HEADS * MLSTM_DV
MLSTM_GATES = 2 * 2 * MLSTM_HEADS
NA_HEADS = 8
NA_DIM = 128
NA_WIDTH = NA_HEADS * NA_DIM
NA_KH = 8
NA_KW = 16
GQA_HEADS = 16
GQA_KV_HEADS = 4
GQA_DIM = 64
GQA_Q = GQA_HEADS * GQA_DIM
GQA_KV = GQA_KV_HEADS * GQA_DIM
GQA_WINDOW = 128
ROPE_THETA = 10000.0
LRU_WIDTH = 1024
LRU_BLOCKS = 8
LRU_BS = LRU_WIDTH // LRU_BLOCKS
LRU_C = 8.0
N_EXPERTS = 16
EC_FACTOR = 2

TS = 256
LANE = 128
SUB = 8
VMEM_LIMIT = 56 * 1024 * 1024


def _cp(*sem):
    return pltpu.CompilerParams(dimension_semantics=sem, vmem_limit_bytes=VMEM_LIMIT)


def _pick(n, prefs):
    for p in prefs:
        if n % p == 0:
            return p
    return n


def _sigmoid(x):
    return 1.0 / (1.0 + jnp.exp(-x))


def _dot(a, b):
    return jnp.dot(a, b, preferred_element_type=F32)


def _dot_nt(a, b):
    return lax.dot_general(a, b, (((1,), (1,)), ((), ())), preferred_element_type=F32)


def _dot_tn(a, b):
    return lax.dot_general(a, b, (((0,), (0,)), ((), ())), preferred_element_type=F32)


def _rms(x, w):
    return x * lax.rsqrt(jnp.mean(x * x, axis=-1, keepdims=True) + EPS) * w


def _ada_kernel(c_ref, w_ref, b_ref, o_ref):
    c = c_ref[...]
    a = (c * _sigmoid(c)).astype(BF16)
    o_ref[0] = _dot(a, w_ref[0].astype(BF16)) + b_ref[0]


def _ada(cond, ada_w, ada_b):
    L, D, D6 = ada_w.shape
    tn = _pick(D6, (1024, 512, 256, 128))
    return pl.pallas_call(
        _ada_kernel, grid=(L, D6 // tn),
        in_specs=[pl.BlockSpec((SUB, D), lambda l, j: (0, 0)),
                  pl.BlockSpec((1, D, tn), lambda l, j: (l, 0, j)),
                  pl.BlockSpec((1, 1, tn), lambda l, j: (l, 0, j))],
        out_specs=pl.BlockSpec((1, SUB, tn), lambda l, j: (l, 0, j)),
        out_shape=jax.ShapeDtypeStruct((L, SUB, D6), F32),
        compiler_params=_cp("parallel", "parallel"), name="ada",
    )(cond, ada_w, ada_b.reshape(L, 1, D6))


def _mod_map(l, k, ctx_row, has_ctx):
    if has_ctx:
        return lambda b, s: ((l * SUB + jnp.where(s == 0, ctx_row, b)) * 6 + k, 0, 0)
    return lambda b, s: ((l * SUB + b) * 6 + k, 0, 0)


def _ctx_lat_specs(D):
    return [pl.BlockSpec((1, TS, D), lambda b, s: (b, 0, 0)),
            pl.BlockSpec((1, TS, D), lambda b, s: (b, jnp.maximum(s - 1, 0), 0))]


def _ctx_lat_tile(ctx_ref, lat_ref):
    return jnp.where(pl.program_id(1) == 0, ctx_ref[0], lat_ref[0])


def _prenorm_kernel(ctx_ref, lat_ref, w_ref, sh_ref, sc_ref, o_ref):
    y = _rms(_ctx_lat_tile(ctx_ref, lat_ref), w_ref[...])
    o_ref[0] = (y * (1.0 + sc_ref[0]) + sh_ref[0]).astype(o_ref.dtype)


def _prenorm(ctx, lat, w, mod, l, ctx_row):
    B, N, D = lat.shape
    S = ctx.shape[1] + N
    vec = lambda k: pl.BlockSpec((1, 1, D), _mod_map(l, k, ctx_row, True))
    return pl.pallas_call(
        _prenorm_kernel, grid=(B, S // TS),
        in_specs=_ctx_lat_specs(D) + [pl.BlockSpec((1, D), lambda b, s: (0, 0)), vec(0), vec(1)],
        out_specs=pl.BlockSpec((1, TS, D), lambda b, s: (b, s, 0)),
        out_shape=jax.ShapeDtypeStruct((B, S, D), BF16),
        compiler_params=_cp("parallel", "parallel"), name="prenorm",
    )(ctx, lat, w.reshape(1, D), mod, mod)


def _resid_kernel(nxt_mode, split_x, x_ref, *rest):
    if split_x:
        x = _ctx_lat_tile(x_ref, rest[0])
        rest = rest[1:]
    else:
        x = x_ref[0]
    y_ref, wpost_ref, g_ref, *rest = rest
    xn = x + g_ref[0] * _rms(y_ref[0], wpost_ref[...])
    if nxt_mode is None:
        (xo_ref,) = rest
    elif nxt_mode == "mixer":
        wpre_ref, sh_ref, sc_ref, xo_ref, ho_ref = rest
    else:
        wpre_ref, sh_ref, sc_ref, router_ref, xo_ref, ho_ref, lo_ref = rest
    xo_ref[0] = xn
    if nxt_mode is not None:
        h = _rms(xn, wpre_ref[...]) * (1.0 + sc_ref[0]) + sh_ref[0]
        ho_ref[0] = h.astype(ho_ref.dtype)
        if nxt_mode == "experts":
            lo_ref[0] = _dot(h.astype(BF16), router_ref[...])


def _resid(x, y, wpost, mod, l, kg, ctx_row, x_off, nxt):
    B, Sy, D = y.shape
    split_x = isinstance(x, tuple)
    has_ctx = ctx_row is not None and (split_x or (x_off == 0 and x.shape[1] == Sy))
    vec = lambda ll, k: pl.BlockSpec((1, 1, D), _mod_map(ll, k, ctx_row, has_ctx))
    row = pl.BlockSpec((1, D), lambda b, s: (0, 0))
    tile = pl.BlockSpec((1, TS, D), lambda b, s: (b, s, 0))
    if split_x:
        in_specs = _ctx_lat_specs(D)
        args = list(x)
    else:
        in_specs = [pl.BlockSpec((1, TS, D), lambda b, s: (b, s + x_off, 0))]
        args = [x]
    in_specs += [tile, row, vec(l, kg)]
    args += [y, wpost.reshape(1, D), mod]
    out_specs = [tile]
    out_shape = [jax.ShapeDtypeStruct((B, Sy, D), F32)]
    mode = None
    if nxt is not None:
        wpre, ln, ksh, ksc, router = nxt
        in_specs += [row, vec(ln, ksh), vec(ln, ksc)]
        args += [wpre.reshape(1, D), mod, mod]
        out_specs.append(tile)
        if router is None:
            mode = "mixer"
            out_shape.append(jax.ShapeDtypeStruct((B, Sy, D), BF16))
        else:
            mode = "experts"
            in_specs.append(pl.BlockSpec((D, LANE), lambda b, s: (0, 0)))
            args.append(router)
            out_specs.append(pl.BlockSpec((1, TS, LANE), lambda b, s: (b, s, 0)))
            out_shape += [jax.ShapeDtypeStruct((B, Sy, D), F32), jax.ShapeDtypeStruct((B, Sy, LANE), F32)]
    out = pl.pallas_call(
        functools.partial(_resid_kernel, mode, split_x), grid=(B, Sy // TS),
        in_specs=in_specs, out_specs=out_specs, out_shape=out_shape,
        compiler_params=_cp("parallel", "parallel"), name="resid",
    )(*args)
    return out


def _mm_kernel(x_ref, w_ref, o_ref):
    o_ref[...] = _dot(x_ref[...].astype(BF16), w_ref[...]).astype(o_ref.dtype)


def _matmul(x, w, out_dtype):
    M, K = x.shape
    N = w.shape[1]
    tm = _pick(M, (1024, 512, 256))
    tn = _pick(N, (1024, 512, 384, 256, 128))
    return pl.pallas_call(
        _mm_kernel, grid=(M // tm, N // tn),
        in_specs=[pl.BlockSpec((tm, K), lambda i, j: (i, 0)),
                  pl.BlockSpec((K, tn), lambda i, j: (0, j))],
        out_specs=pl.BlockSpec((tm, tn), lambda i, j: (i, j)),
        out_shape=jax.ShapeDtypeStruct((M, N), out_dtype),
        compiler_params=_cp("parallel", "parallel"), name="matmul",
    )(x, w)


def _mm2_kernel(x1_ref, x2_ref, w1_ref, w2_ref, o_ref):
    o_ref[...] = _dot(x1_ref[...], w1_ref[...]) + _dot(x2_ref[...], w2_ref[...])


def _matmul2(x1, x2, w):
    M, K1 = x1.shape
    K2 = x2.shape[1]
    assert K1 == K2 and w.shape[0] == K1 + K2
    N = w.shape[1]
    tm = _pick(M, (1024, 512, 256))
    tn = _pick(N, (512, 256, 128))
    return pl.pallas_call(
        _mm2_kernel, grid=(M // tm, N // tn),
        in_specs=[pl.BlockSpec((tm, K1), lambda i, j: (i, 0)),
                  pl.BlockSpec((tm, K2), lambda i, j: (i, 0)),
                  pl.BlockSpec((K1, tn), lambda i, j: (0, j)),
                  pl.BlockSpec((K2, tn), lambda i, j: (1, j))],
        out_specs=pl.BlockSpec((tm, tn), lambda i, j: (i, j)),
        out_shape=jax.ShapeDtypeStruct((M, N), F32),
        compiler_params=_cp("parallel", "parallel"), name="matmul2",
    )(x1, x2, w, w)


def _log_sigmoid(x):
    return jnp.minimum(x, 0.0) - jnp.log1p(jnp.exp(-jnp.abs(x)))


def _mlstm_kernel(q_ref, k_ref, v_ref, gc_ref, gr_ref, bc_ref, br_ref, o_ref, C_ref, n_ref, m_ref):
    H, dk, dv, L = MLSTM_HEADS, MLSTM_DK, MLSTM_DV, TS
    d = pl.program_id(1)
    t = pl.program_id(2)

    @pl.when(t == 0)
    def _():
        C_ref[...] = jnp.zeros_like(C_ref)
        n_ref[...] = jnp.zeros_like(n_ref)
        m_ref[...] = jnp.zeros_like(m_ref)

    fwd = d == 0
    ri = lax.broadcasted_iota(jnp.int32, (L, L), 0)
    ci = lax.broadcasted_iota(jnp.int32, (L, L), 1)
    diff = (ci - ri) * (1 - 2 * d)
    causal = diff <= 0
    causal_f = jnp.where(causal, 1.0, 0.0)
    causal_t = jnp.where(diff >= 0, 1.0, 0.0)
    gc = gc_ref[0] + bc_ref[...]
    gr = gr_ref[0] + br_ref[...]
    lfc = _log_sigmoid(gc)
    lfr = _log_sigmoid(gr)
    hi = lax.Precision.HIGHEST
    bcol_all = jnp.dot(causal_f, lfc, precision=hi, preferred_element_type=F32)
    brow_all = jnp.dot(lfr, causal_t, precision=hi, preferred_element_type=F32)
    tot_all = jnp.sum(lfr, axis=-1, keepdims=True)
    sel = lambda a, b: jnp.where(fwd, a, b)
    scale = dk ** -0.5
    heads = range(H)
    qs = [q_ref[0, :, h * dk:(h + 1) * dk] for h in heads]
    ks = [k_ref[0, :, h * dk:(h + 1) * dk] * scale for h in heads]
    vs = [v_ref[0, :, h * dv:(h + 1) * dv] for h in heads]
    qk = [_dot_nt(qs[h], ks[h]) for h in heads]
    qc = [_dot(qs[h], C_ref[h].astype(BF16)) for h in heads]
    ms = [m_ref[h] for h in heads]
    bcs, brs, irs, decays, m_news, wks, kvs = [], [], [], [], [], [], []
    for h in heads:
        ic = sel(gc[:, h:h + 1], gc[:, 2 * H + h:2 * H + h + 1])
        ir = sel(gr[h:h + 1], gr[2 * H + h:2 * H + h + 1])
        bc = sel(bcol_all[:, H + h:H + h + 1], bcol_all[:, 3 * H + h:3 * H + h + 1])
        br = sel(brow_all[H + h:H + h + 1], brow_all[3 * H + h:3 * H + h + 1])
        tot = sel(tot_all[H + h:H + h + 1], tot_all[3 * H + h:3 * H + h + 1])
        g_c = tot - bc + ic
        g_r = tot - br + ir
        m_new = jnp.maximum(tot + ms[h], jnp.max(g_r, axis=-1, keepdims=True))
        wk = jnp.exp(g_c - m_new) * ks[h].astype(F32)
        bcs.append(bc); brs.append(br); irs.append(ir); m_news.append(m_new); wks.append(wk)
        decays.append(jnp.exp(tot + ms[h] - m_new))
        kvs.append(_dot_tn(wk.astype(BF16), vs[h]))
    ss, mjs, w_inters = [], [], []
    for h in heads:
        dmat = jnp.where(causal, bcs[h] - brs[h] + irs[h], NEG)
        inter = bcs[h] + ms[h]
        mj = jnp.maximum(inter, jnp.max(dmat, axis=-1, keepdims=True))
        ss.append(qk[h] * jnp.exp(dmat - mj))
        mjs.append(mj)
        w_inters.append(jnp.exp(inter - mj))
    sv = [_dot(ss[h].astype(BF16), vs[h]) for h in heads]
    for h in heads:
        num = w_inters[h] * qc[h] + sv[h]
        qn = jnp.sum(qs[h].astype(F32) * n_ref[h], axis=-1, keepdims=True)
        den = w_inters[h] * qn + jnp.sum(ss[h], axis=-1, keepdims=True)
        o_ref[0, 0, :, h * dv:(h + 1) * dv] = num / jnp.maximum(jnp.abs(den), jnp.exp(-mjs[h]))
        C_ref[h] = decays[h] * C_ref[h] + kvs[h]
        n_ref[h] = decays[h] * n_ref[h] + jnp.sum(wks[h], axis=0, keepdims=True)
        m_ref[h] = m_news[h]


def _mlstm(P, gcol, grow, gate_b):
    B, S, _ = P.shape
    nc = S // TS
    chunk = lambda d, t: jnp.where(d == 0, t, jnp.where(t == 0, 0, nc - t))
    blk = lambda c: pl.BlockSpec((1, TS, MLSTM_QK), lambda b, d, t: (b, chunk(d, t), c))
    G = MLSTM_GATES
    return pl.pallas_call(
        _mlstm_kernel, grid=(B, 2, nc),
        in_specs=[blk(0), blk(1), blk(2),
                  pl.BlockSpec((1, TS, G), lambda b, d, t: (b, chunk(d, t), 0)),
                  pl.BlockSpec((1, G, TS), lambda b, d, t: (b, 0, chunk(d, t))),
                  pl.BlockSpec((1, G), lambda b, d, t: (0, 0)),
                  pl.BlockSpec((G, 1), lambda b, d, t: (0, 0))],
        out_specs=pl.BlockSpec((1, 1, TS, MLSTM_V), lambda b, d, t: (d, b, chunk(d, t), 0)),
        out_shape=jax.ShapeDtypeStruct((2, B, S, MLSTM_V), F32),
        scratch_shapes=[pltpu.VMEM((MLSTM_HEADS, MLSTM_DK, MLSTM_DV), F32),
                        pltpu.VMEM((MLSTM_HEADS, 1, MLSTM_DK), F32),
                        pltpu.VMEM((MLSTM_HEADS, 1, 1), F32)],
        compiler_params=_cp("parallel", "arbitrary", "arbitrary"), name="mlstm",
    )(P, P, P, gcol, grow, gate_b.reshape(1, G), gate_b.reshape(G, 1))


def _mlstm_out_kernel(hf_ref, hr_ref, o_ref, gain_ref, out_ref):
    dv = MLSTM_DV
    for h in range(MLSTM_HEADS):
        sl = slice(h * dv, (h + 1) * dv)
        x = hf_ref[0, 0, :, sl] + hr_ref[0, 0, :, sl]
        out_ref[0, :, sl] = (_rms(x, gain_ref[:, sl]) * _sigmoid(o_ref[0, :, sl])).astype(out_ref.dtype)


def _mlstm_out(Hd, OG, gain):
    _, B, S, V = Hd.shape
    return pl.pallas_call(
        _mlstm_out_kernel, grid=(B, S // TS),
        in_specs=[pl.BlockSpec((1, 1, TS, V), lambda b, s: (0, b, s, 0)),
                  pl.BlockSpec((1, 1, TS, V), lambda b, s: (1, b, s, 0)),
                  pl.BlockSpec((1, TS, V), lambda b, s: (b, s, 0)),
                  pl.BlockSpec((1, V), lambda b, s: (0, 0))],
        out_specs=pl.BlockSpec((1, TS, V), lambda b, s: (b, s, 0)),
        out_shape=jax.ShapeDtypeStruct((B, S, V), BF16),
        compiler_params=_cp("parallel", "parallel"), name="mlstm_out",
    )(Hd, Hd, OG, gain.reshape(1, V))


NA_QR = 4
NA_KR = NA_KH + 2 * NA_QR - 4


def _na_bias(rpb):
    a = np.arange(NA_QR)
    kr = np.arange(NA_KR)
    qrel = NA_QR * np.arange(3)[:, None] + a[None, :]
    r0rel = np.stack([0 * a, a, NA_KR - NA_KH + 0 * a])
    valid_r = (kr[None, None] >= r0rel[..., None]) & (kr[None, None] < r0rel[..., None] + NA_KH)
    drow = np.clip(kr[None, None] - qrel[..., None] + (NA_KH - 1), 0, 2 * NA_KH - 2)
    w = np.arange(GRID_W)[:, None]
    cc = np.arange(GRID_W)[None, :]
    col_start = np.clip(w - NA_KW // 2, 0, GRID_W - NA_KW)
    valid_c = (cc >= col_start) & (cc < col_start + NA_KW)
    dcol = np.clip(cc - w + (NA_KW - 1), 0, 2 * NA_KW - 2)
    pick_r = jnp.asarray(np.eye(2 * NA_KH - 1, dtype=np.float32)[drow])
    pick_c = jnp.asarray(np.eye(2 * NA_KW - 1, dtype=np.float32)[dcol])
    t = jnp.einsum("cakr,hrs,wzs->chawkz", pick_r, rpb.astype(F32), pick_c,
                   precision=lax.Precision.HIGHEST)
    valid = valid_r[:, None, :, None, :, None] & valid_c[None, None, None, :, None, :]
    t = jnp.where(valid, t, NEG)
    return t.reshape(3, NA_HEADS, NA_QR * GRID_W, NA_KR * GRID_W)


def _na_kernel(q_ref, k_ref, v_ref, bias_ref, o_ref, *, Lc, rows_n):
    W, d = GRID_W, NA_DIM
    scale = d ** -0.5
    kc = k_ref[0, 0:Lc, :]
    vc = v_ref[0, 0:Lc, :]
    sc = _dot_nt(q_ref[0, 0:Lc, :], kc) * scale
    pc = jnp.exp(sc - jnp.max(sc, axis=-1, keepdims=True))
    oc = _dot(pc.astype(BF16), vc) / jnp.sum(pc, axis=-1, keepdims=True)
    o_ref[0, 0:Lc, :] = oc.astype(o_ref.dtype)

    def group(gi, carry):
        r = gi * NA_QR
        kr0 = jnp.clip(r - NA_KH // 2, 0, rows_n - NA_KR)
        qoff = pl.multiple_of(Lc + r * W, NA_QR * W)
        koff = pl.multiple_of(Lc + kr0 * W, W)
        qg = q_ref[0, pl.ds(qoff, NA_QR * W), :]
        kw = k_ref[0, pl.ds(koff, NA_KR * W), :]
        vw = v_ref[0, pl.ds(koff, NA_KR * W), :]
        sw = _dot_nt(qg, kw) * scale + bias_ref[(r - kr0) // NA_QR, 0]
        sx = _dot_nt(qg, kc) * scale
        m = jnp.maximum(jnp.max(sw, axis=-1, keepdims=True), jnp.max(sx, axis=-1, keepdims=True))
        pw = jnp.exp(sw - m)
        px = jnp.exp(sx - m)
        l = jnp.sum(pw, axis=-1, keepdims=True) + jnp.sum(px, axis=-1, keepdims=True)
        o = (_dot(pw.astype(BF16), vw) + _dot(px.astype(BF16), vc)) / l
        o_ref[0, pl.ds(qoff, NA_QR * W), :] = o.astype(o_ref.dtype)
        return carry

    lax.fori_loop(0, rows_n // NA_QR, group, 0, unroll=4)


def _na(P, bias, Lc, col0):
    B, S, _ = P.shape
    rows_n = (S - Lc) // GRID_W
    assert rows_n % NA_QR == 0 and rows_n >= NA_KR
    c0 = col0 // NA_DIM
    blk = lambda c: pl.BlockSpec((1, S, NA_DIM), lambda b, h: (b, 0, c0 + c * NA_HEADS + h))
    return pl.pallas_call(
        functools.partial(_na_kernel, Lc=Lc, rows_n=rows_n), grid=(B, NA_HEADS),
        in_specs=[blk(0), blk(1), blk(2),
                  pl.BlockSpec((3, 1, NA_QR * GRID_W, NA_KR * GRID_W), lambda b, h: (0, h, 0, 0))],
        out_specs=pl.BlockSpec((1, S, NA_DIM), lambda b, h: (b, 0, h)),
        out_shape=jax.ShapeDtypeStruct((B, S, NA_WIDTH), BF16),
        compiler_params=_cp("parallel", "parallel"), name="na",
    )(P, P, P, bias)


def _rope_tables(Lc, N):
    quarter = GQA_DIM // 4
    t = jnp.arange(N)
    inv = ROPE_THETA ** (-(jnp.arange(quarter, dtype=F32) / quarter))
    ang_r = (t // GRID_W).astype(F32)[:, None] * inv[None, :]
    ang_c = (t % GRID_W).astype(F32)[:, None] * inv[None, :]
    cos = jnp.concatenate([jnp.cos(ang_r)] * 2 + [jnp.cos(ang_c)] * 2, axis=-1)
    sin = jnp.concatenate([-jnp.sin(ang_r), jnp.sin(ang_r), -jnp.sin(ang_c), jnp.sin(ang_c)], axis=-1)
    cos = jnp.concatenate([jnp.ones((Lc, GQA_DIM), F32), cos], axis=0)
    sin = jnp.concatenate([jnp.zeros((Lc, GQA_DIM), F32), sin], axis=0)
    return jnp.tile(cos, (1, LANE // GQA_DIM)), jnp.tile(sin, (1, LANE // GQA_DIM))


def _rope_rot(x, cos, sin):
    quarter = GQA_DIM // 4
    lane = lax.broadcasted_iota(jnp.int32, x.shape, 1)
    first = (lane % (2 * quarter)) < quarter
    rot = jnp.where(first, pltpu.roll(x, LANE - quarter, 1), pltpu.roll(x, quarter, 1))
    return x * cos + rot * sin


def _gqa_kernel(sink_ref, q_ref, k_ref, v_ref, cos_ref, sin_ref, o_ref, kd_ref, vd_ref, *, Lc, N):
    W, dh, KV = GQA_WINDOW, GQA_DIM, GQA_KV_HEADS
    G = GQA_HEADS // KV
    S = Lc + N
    i = pl.program_id(1)
    scale = dh ** -0.5

    @pl.when(i == 0)
    def _():
        low_t = lax.broadcasted_iota(jnp.int32, (TS, LANE), 1) < dh
        zero = jnp.zeros((W, LANE), BF16)
        for kh in range(KV):
            for ref in (kd_ref, vd_ref):
                ref[kh, Lc:Lc + W, :] = zero
                ref[kh, Lc + W + N:Lc + 2 * W + N, :] = zero

        def build(c, carry):
            r0 = pl.multiple_of(c * TS, TS)
            dst = pl.multiple_of(r0 + jnp.where(c >= Lc // TS, W, 0), W)
            cs = cos_ref[pl.ds(r0, TS), :]
            sn = sin_ref[pl.ds(r0, TS), :]
            for hh in range(KV // 2):
                kr = _rope_rot(k_ref[0, pl.ds(r0, TS), hh * LANE:(hh + 1) * LANE], cs, sn)
                vx = v_ref[0, pl.ds(r0, TS), hh * LANE:(hh + 1) * LANE]
                for ref, a in ((kd_ref, kr), (vd_ref, vx)):
                    sw = pltpu.roll(a, dh, 1)
                    ref[2 * hh, pl.ds(dst, TS), :] = jnp.where(low_t, a, sw).astype(BF16)
                    ref[2 * hh + 1, pl.ds(dst, TS), :] = jnp.where(low_t, sw, a).astype(BF16)
            return carry

        lax.fori_loop(0, S // TS, build, 0)

    span = 3 * W
    off = pl.multiple_of(Lc + i * W, W)
    cq = cos_ref[pl.ds(off, W), :]
    sq = sin_ref[pl.ds(off, W), :]
    ii = lax.broadcasted_iota(jnp.int32, (G * W, span), 0) % W
    jj = lax.broadcasted_iota(jnp.int32, (G * W, span), 1)
    kpos = i * W - W + jj
    valid = (kpos >= 0) & (kpos < N) & (jnp.abs(ii - (jj - W)) <= W)
    rg = lax.broadcasted_iota(jnp.int32, (G * W, 1), 0) // W
    low = lax.broadcasted_iota(jnp.int32, (W, LANE), 1) < dh
    for kh in range(KV):
        parts = []
        for slab in range(G // 2):
            j = kh * (G // 2) + slab
            qr = _rope_rot(q_ref[0, :, j * LANE:(j + 1) * LANE] * scale, cq, sq)
            parts += [jnp.where(low, qr, 0.0).astype(BF16), jnp.where(low, 0.0, qr).astype(BF16)]
        q4 = jnp.concatenate(parts, axis=0)
        kw = kd_ref[kh, pl.ds(off, span), :]
        vw = vd_ref[kh, pl.ds(off, span), :]
        kc = kd_ref[kh, 0:Lc, :]
        vc = vd_ref[kh, 0:Lc, :]
        sw = jnp.where(valid, _dot_nt(q4, kw), NEG)
        sx = _dot_nt(q4, kc)
        sk = sink_ref[G * kh + G - 1]
        for g in range(G - 2, -1, -1):
            sk = jnp.where(rg == g, sink_ref[G * kh + g], sk)
        m = jnp.maximum(jnp.maximum(jnp.max(sw, axis=-1, keepdims=True),
                                    jnp.max(sx, axis=-1, keepdims=True)), sk)
        pw = jnp.exp(sw - m)
        px = jnp.exp(sx - m)
        l = jnp.sum(pw, axis=-1, keepdims=True) + jnp.sum(px, axis=-1, keepdims=True) + jnp.exp(sk - m)
        o = (_dot(pw.astype(BF16), vw) + _dot(px.astype(BF16), vc)) / l
        for slab in range(G // 2):
            j = kh * (G // 2) + slab
            o_ref[0, :, j * LANE:(j + 1) * LANE] = jnp.where(
                low, o[2 * slab * W:(2 * slab + 1) * W], o[(2 * slab + 1) * W:(2 * slab + 2) * W]
            ).astype(o_ref.dtype)


def _gqa(P1, cos, sin, sink, Lc):
    B, S, _ = P1.shape
    N = S - Lc
    W = GQA_WINDOW
    rows = Lc + 2 * W + N
    table = pl.BlockSpec((S, LANE), lambda b, i: (0, 0))
    return pl.pallas_call(
        functools.partial(_gqa_kernel, Lc=Lc, N=N), grid=(B, N // W),
        in_specs=[pl.BlockSpec(memory_space=pltpu.SMEM),
                  pl.BlockSpec((1, W, GQA_Q), lambda b, i: (b, i + Lc // W, 0)),
                  pl.BlockSpec((1, S, GQA_KV), lambda b, i: (b, 0, GQA_Q // GQA_KV)),
                  pl.BlockSpec((1, S, GQA_KV), lambda b, i: (b, 0, GQA_Q // GQA_KV + 1)),
                  table, table],
        out_specs=pl.BlockSpec((1, W, GQA_Q), lambda b, i: (b, i, 0)),
        out_shape=jax.ShapeDtypeStruct((B, N, GQA_Q), BF16),
        scratch_shapes=[pltpu.VMEM((GQA_KV_HEADS, rows, LANE), BF16),
                        pltpu.VMEM((GQA_KV_HEADS, rows, LANE), BF16)],
        compiler_params=_cp("parallel", "arbitrary"), name="gqa",
    )(sink, P1, P1, P1, cos, sin)


def _gelu_tanh(x):
    return 0.5 * x * (1.0 + jnp.tanh(np.sqrt(2.0 / np.pi) * (x + 0.044715 * (x * x * x))))


def _lru_kernel(x_ref, g_ref, cw_ref, cb_ref, gw_ref, gb_ref, lam_ref, o_ref,
                af_ref, bf_ref, ar_ref, br_ref, *, Lc, N):
    S = Lc + N
    CH = 128
    LW = x_ref.shape[-1]
    nb = LW // LRU_BS
    nt = S // SUB
    ntc = Lc // SUB
    row = lax.broadcasted_iota(jnp.int32, (CH, LW), 0)
    crow = row % SUB
    first_row = row == 0
    last_row = row == CH - 1
    last_row2 = row == CH - 2
    sps = []
    for d in range(2):
        lam = lam_ref[d:d + 1, :]
        sps.append(jnp.maximum(-lam, 0.0) + jnp.log1p(jnp.exp(-jnp.abs(lam))))

    def gates(c, carry):
        off = pl.multiple_of(c * CH, CH)
        xc = x_ref[0, pl.ds(off, CH), :]
        prev = x_ref[0, pl.ds(pl.multiple_of(jnp.maximum(off - SUB, 0), SUB), SUB), :]
        nxt = x_ref[0, pl.ds(pl.multiple_of(jnp.minimum(off + CH, S - SUB), SUB), SUB), :]
        pz = jnp.where((c == 0) | (c == Lc // CH), 0.0, 1.0)
        nz = jnp.where((c == Lc // CH - 1) | (c == S // CH - 1), 0.0, 1.0)
        p7 = prev[SUB - 1:SUB, :] * pz
        n0 = nxt[0:1, :] * nz
        n1 = nxt[1:2, :] * nz
        xm1 = jnp.where(first_row, p7, pltpu.roll(xc, 1, 0))
        xp1 = jnp.where(last_row, n0, pltpu.roll(xc, CH - 1, 0))
        xp2 = jnp.where(last_row2, n0, jnp.where(last_row, n1, pltpu.roll(xc, CH - 2, 0)))
        u = (xm1 * cw_ref[0:1, :] + cb_ref[...] + xc * cw_ref[1:2, :] + xp1 * cw_ref[2:3, :]
             + xp2 * cw_ref[3:4, :])
        ub = u.astype(BF16)
        for d, (a_ref, b_ref) in enumerate(((af_ref, bf_ref), (ar_ref, br_ref))):
            pre = []
            for gi in range(2):
                pre.append(jnp.concatenate(
                    [_dot(ub[:, k * LRU_BS:(k + 1) * LRU_BS], gw_ref[d, gi, k].astype(BF16))
                     for k in range(nb)], axis=-1) + gb_ref[d, gi:gi + 1, :])
            r = _sigmoid(pre[0])
            ig = _sigmoid(pre[1])
            a = jnp.exp(-LRU_C * r * sps[d])
            b = jnp.sqrt(1.0 - a * a) * (ig * u)
            for s in (1, 2, 4):
                if d == 0:
                    keep = crow >= s
                    a_sh = pltpu.roll(a, s, 0)
                    b_sh = pltpu.roll(b, s, 0)
                else:
                    keep = crow < SUB - s
                    a_sh = pltpu.roll(a, CH - s, 0)
                    b_sh = pltpu.roll(b, CH - s, 0)
                b = jnp.where(keep, a * b_sh + b, b)
                a = jnp.where(keep, a * a_sh, a)
            a_ref[pl.ds(off, CH), :] = a
            b_ref[pl.ds(off, CH), :] = b
        return carry

    lax.fori_loop(0, S // CH, gates, 0)

    def scan(ft, rt, carry):
        hf, hr = carry
        of = pl.multiple_of(ft * SUB, SUB)
        orv = pl.multiple_of(rt * SUB, SUB)
        tf = af_ref[pl.ds(of, SUB), :] * hf + bf_ref[pl.ds(of, SUB), :]
        tr = ar_ref[pl.ds(orv, SUB), :] * hr + br_ref[pl.ds(orv, SUB), :]
        bf_ref[pl.ds(of, SUB), :] = tf
        br_ref[pl.ds(orv, SUB), :] = tr
        return (jnp.broadcast_to(tf[SUB - 1:SUB, :], (SUB, LW)), jnp.broadcast_to(tr[0:1, :], (SUB, LW)))

    zero = jnp.zeros((SUB, LW), F32)
    carry = lax.fori_loop(0, ntc, lambda i, c: scan(i, ntc - 1 - i, c), (zero, zero), unroll=8)
    lax.fori_loop(ntc, nt, lambda i, c: scan(i, nt - 1 - (i - ntc), c), carry, unroll=8)

    def outp(c, carry):
        off = pl.multiple_of(c * CH, CH)
        src = pl.multiple_of(Lc + off, CH)
        hs = bf_ref[pl.ds(src, CH), :] + br_ref[pl.ds(src, CH), :]
        o_ref[0, pl.ds(off, CH), :] = (hs * _gelu_tanh(g_ref[0, pl.ds(src, CH), :])).astype(o_ref.dtype)
        return carry

    lax.fori_loop(0, N // CH, outp, 0)


def _lru(P1, col_x, col_g, conv_w, conv_b, gate_w, gate_b, lam, Lc):
    B, S, _ = P1.shape
    N = S - Lc
    nb = 2
    lw = nb * LRU_BS
    return pl.pallas_call(
        functools.partial(_lru_kernel, Lc=Lc, N=N), grid=(B, LRU_BLOCKS // nb),
        in_specs=[pl.BlockSpec((1, S, lw), lambda b, j: (b, 0, col_x // lw + j)),
                  pl.BlockSpec((1, S, lw), lambda b, j: (b, 0, col_g // lw + j)),
                  pl.BlockSpec((4, lw), lambda b, j: (0, j)),
                  pl.BlockSpec((1, lw), lambda b, j: (0, j)),
                  pl.BlockSpec((2, 2, nb, LRU_BS, LRU_BS), lambda b, j: (0, 0, j, 0, 0)),
                  pl.BlockSpec((2, 2, lw), lambda b, j: (0, 0, j)),
                  pl.BlockSpec((2, lw), lambda b, j: (0, j))],
        out_specs=pl.BlockSpec((1, N, lw), lambda b, j: (b, 0, j)),
        out_shape=jax.ShapeDtypeStruct((B, N, LRU_WIDTH), BF16),
        scratch_shapes=[pltpu.VMEM((S, lw), F32) for _ in range(4)],
        compiler_params=_cp("parallel", "parallel"), name="lru",
    )(P1, P1, conv_w, conv_b.reshape(1, LRU_WIDTH), gate_w, gate_b, lam)


ROUTE_BISECT = 2048


def _cumsum_lanes(x):
    n = x.shape[-1]
    lane = lax.broadcasted_iota(jnp.int32, x.shape, 1)
    s = 1
    while s < n:
        x = x + jnp.where(lane >= s, pltpu.roll(x, s, 1), 0.0)
        s *= 2
    return x


def _route_kernel(lt_ref, idx_ref, g_ref, cm_ref, af_ref, *, segs, S):
    E = lt_ref.shape[1]
    b = pl.program_id(0)
    lt = lt_ref[0]
    ex = jnp.exp(lt - jnp.max(lt, axis=0, keepdims=True))
    aff = ex / jnp.sum(ex, axis=0, keepdims=True)
    lane_e = lax.broadcasted_iota(jnp.int32, (1, LANE), 1)
    slot0 = 0
    for lo, n, cap in segs:
        a = aff[:, lo:lo + n]

        def bisect(carry):
            lo_v, hi_v, c_lo, it, _ = carry
            mid = lo_v + 0.5 * (hi_v - lo_v)
            c = jnp.sum(jnp.where(a >= mid, 1.0, 0.0), axis=1, keepdims=True)
            ge = c >= cap
            lo_v = jnp.where(ge, mid, lo_v)
            hi_v = jnp.where(ge, hi_v, mid)
            c_lo = jnp.where(ge, c, c_lo)
            mid = lo_v + 0.5 * (hi_v - lo_v)
            done = (c_lo == cap) | (mid <= lo_v) | (mid >= hi_v)
            return lo_v, hi_v, c_lo, it + 1, jnp.min(jnp.where(done, 1, 0))

        lo_v, hi_v, _, _, _ = lax.while_loop(
            lambda carry: (carry[4] == 0) & (carry[3] < ROUTE_BISECT), bisect,
            (jnp.zeros((E, 1), F32), jnp.full((E, 1), 2.0, F32), jnp.full((E, 1), float(n), F32),
             jnp.int32(0), jnp.int32(0)))
        top = jnp.where(a >= hi_v, 1.0, 0.0)
        tie = jnp.where((a >= lo_v) & (a < hi_v), 1.0, 0.0)
        need = cap - jnp.sum(top, axis=1, keepdims=True)
        tie_rank = _cumsum_lanes(tie) - tie
        sel = top + tie * jnp.where(tie_rank < need, 1.0, 0.0)
        cm_ref[:, 0:n] = _cumsum_lanes(sel) * sel
        af_ref[:, 0:n] = a
        tok = (lax.broadcasted_iota(jnp.int32, (1, n), 1) + (lo + b * S)).astype(F32)
        jcol = (lax.broadcasted_iota(jnp.int32, (cap, 1), 0) + 1).astype(F32)

        def per_expert(e, carry):
            oi, og = carry
            ce = cm_ref[pl.ds(e, 1), 0:n]
            ae = af_ref[pl.ds(e, 1), 0:n]
            acc_i = jnp.zeros((cap, LANE), F32)
            acc_g = jnp.zeros((cap, LANE), F32)
            for k in range(n // LANE):
                sl = slice(k * LANE, (k + 1) * LANE)
                hit = ce[:, sl] == jcol
                acc_i = acc_i + jnp.where(hit, tok[:, sl], 0.0)
                acc_g = acc_g + jnp.where(hit, ae[:, sl], 0.0)
            mine = lane_e == e
            return (jnp.where(mine, jnp.sum(acc_i, axis=1, keepdims=True), oi),
                    jnp.where(mine, jnp.sum(acc_g, axis=1, keepdims=True), og))

        zero = jnp.zeros((cap, LANE), F32)
        oi, og = lax.fori_loop(0, E, per_expert, (zero, zero))
        idx_ref[0, slot0:slot0 + cap, :] = oi
        g_ref[0, slot0:slot0 + cap, :] = og
        slot0 += cap


def _route(logits_t, segs):
    B, E, S = logits_t.shape
    slots = sum(cap for _, _, cap in segs)
    nmax = max(n for _, n, _ in segs)
    out = jax.ShapeDtypeStruct((B, slots, LANE), F32)
    blk = pl.BlockSpec((1, slots, LANE), lambda b: (b, 0, 0))
    return pl.pallas_call(
        functools.partial(_route_kernel, segs=segs, S=S), grid=(B,),
        in_specs=[pl.BlockSpec((1, E, S), lambda b: (b, 0, 0))],
        out_specs=[blk, blk], out_shape=[out, out],
        scratch_shapes=[pltpu.VMEM((E, nmax), F32), pltpu.VMEM((E, nmax), F32)],
        compiler_params=_cp("parallel"), name="route",
    )(logits_t)


RING = 4


def _ffn_kernel(ids_ref, h_hbm, w1_ref, w3_ref, w2_ref, g_ref, zero_hbm, out_hbm,
                xf_ref, xb_ref, y_ref, ring_ref, xsem, asem, ssem, *, rm, ns, nf):
    del zero_hbm
    C = rm // nf
    D = xf_ref.shape[-1]
    step = pl.program_id(0)
    f = pl.program_id(1)
    q = step * nf + f
    cur = step % 2
    f0 = pl.multiple_of(f * C, SUB)

    def chunk_tokens(c_step, c_f):
        p = c_step - 1
        return jnp.where(p < 0, rm, jnp.minimum(p, ns - 1) * rm) + c_f * C

    def gather(src, dst, sem, tb, row0, n):
        for i in range(n):
            pltpu.make_async_copy(src.at[pl.ds(ids_ref[tb + i], 1), :], dst.at[pl.ds(row0 + i, 1), :], sem).start()

    def scatter(src, tb, n, sem):
        for i in range(n):
            pltpu.make_async_copy(src.at[pl.ds(i, 1), :], out_hbm.at[pl.ds(ids_ref[tb + i], 1), :], sem).start()

    def wait_rows(buf, sem):
        pltpu.make_async_copy(buf, buf, sem).wait()

    chunk = ring_ref.at[0]
    tb_prev = jnp.where(f == 0, chunk_tokens(step - 1, nf - 1), chunk_tokens(step, f - 1))
    tb_next = jnp.where(f == nf - 1, chunk_tokens(step + 1, 0), chunk_tokens(step, f + 1))

    @pl.when(q == 0)
    def _():
        ring_ref[...] = jnp.zeros_like(ring_ref)
        y_ref[1] = jnp.zeros((rm, D), F32)

        def first(r, carry):
            pltpu.make_async_copy(h_hbm.at[pl.ds(ids_ref[r], 1), :], xf_ref.at[pl.ds(r, 1), :], xsem).start()
            return carry
        lax.fori_loop(0, rm, first, 0, unroll=8)
        scatter(y_ref.at[1, pl.ds(0, C)], chunk_tokens(0, nf - 2), C, ssem.at[0])
        scatter(y_ref.at[1, pl.ds(0, C)], chunk_tokens(0, nf - 3), C, ssem.at[1])
        gather(out_hbm, ring_ref.at[0], asem.at[0], chunk_tokens(0, 0), 0, C)

    @pl.when(f == 0)
    def _():
        wait_rows(xf_ref, xsem)
        xb_ref[...] = xf_ref[...].astype(BF16)
        y_ref[cur] = jnp.zeros((rm, D), F32)

    wait_rows(chunk, ssem.at[q % 3])
    scatter(ring_ref.at[(q + RING - 1) % RING], tb_prev, C, ssem.at[(q + 2) % 3])
    gather(out_hbm, ring_ref.at[(q + 1) % RING], asem.at[(q + 1) % 2], tb_next, 0, C)
    gather(h_hbm, xf_ref, xsem, jnp.minimum(step + 1, ns - 1) * rm + f * C, f0, C)

    x = xb_ref[...]
    u = _dot(x, w1_ref[0, 0].astype(BF16))
    v = _dot(x, w3_ref[0, 0].astype(BF16))
    a = (u * _sigmoid(u) * v).astype(BF16)
    dc = _pick(D, (512, 256, 128))
    for c in range(D // dc):
        y_ref[cur, :, c * dc:(c + 1) * dc] += _dot(a, w2_ref[0, 0, :, c * dc:(c + 1) * dc].astype(BF16))

    slot = q % RING
    wait_rows(chunk, asem.at[q % 2])
    mine = lax.broadcasted_iota(jnp.int32, (C, LANE), 1) == jnp.clip(step - 1, 0, ns - 1) // 2
    gate = jnp.sum(jnp.where(mine, g_ref[pl.ds(f0, C), :], 0.0), axis=-1, keepdims=True)
    ring_ref[slot] = ring_ref[slot] + y_ref[1 - cur, pl.ds(f0, C), :] * gate

    @pl.when(q == (ns + 1) * nf - 1)
    def _():
        scatter(ring_ref.at[slot], chunk_tokens(step, f), C, ssem.at[q % 3])
        for k in range(3):
            wait_rows(chunk, ssem.at[k])
        wait_rows(chunk, asem.at[(q + 1) % 2])
        wait_rows(xf_ref, xsem)


def _ffn(hf, ids, g, w1, w3, w2, l):
    T, D = hf.shape
    E = w1.shape[1]
    R = g.shape[0]
    FF = w1.shape[-1]
    rm = R // 2
    ns = 2 * E
    tf = _pick(FF, (256, 128))
    nf = FF // tf
    assert nf >= 3 and rm % nf == 0 and (rm // nf) % SUB == 0
    ex = lambda s: jnp.minimum(s, ns - 1) // 2
    grid_spec = pltpu.PrefetchScalarGridSpec(
        num_scalar_prefetch=1, grid=(ns + 1, nf),
        in_specs=[pl.BlockSpec(memory_space=pl.ANY),
                  pl.BlockSpec((1, 1, D, tf), lambda s, f, ids: (l, ex(s), 0, f)),
                  pl.BlockSpec((1, 1, D, tf), lambda s, f, ids: (l, ex(s), 0, f)),
                  pl.BlockSpec((1, 1, tf, D), lambda s, f, ids: (l, ex(s), f, 0)),
                  pl.BlockSpec((rm, LANE), lambda s, f, ids: (jnp.maximum(s - 1, 0) % 2, 0)),
                  pl.BlockSpec(memory_space=pl.ANY)],
        out_specs=pl.BlockSpec(memory_space=pl.ANY),
        scratch_shapes=[pltpu.VMEM((rm, D), F32), pltpu.VMEM((rm, D), BF16),
                        pltpu.VMEM((2, rm, D), F32), pltpu.VMEM((RING, rm // nf, D), F32),
                        pltpu.SemaphoreType.DMA, pltpu.SemaphoreType.DMA((2,)), pltpu.SemaphoreType.DMA((3,))])
    return pl.pallas_call(
        functools.partial(_ffn_kernel, rm=rm, ns=ns, nf=nf), grid_spec=grid_spec,
        out_shape=jax.ShapeDtypeStruct((T, D), F32),
        input_output_aliases={6: 0},
        compiler_params=_cp("arbitrary", "arbitrary"), name="ffn",
    )(ids, hf, w1, w3, w2, g, jnp.zeros((T, D), F32))


def _moe(h, logits, w1, w3, w2, l, Lc):
    B, S, D = h.shape
    E = N_EXPERTS
    assert B % 2 == 0
    segs = tuple((lo, n, EC_FACTOR * n // E) for lo, n in ((Lc, S - Lc), (0, Lc)) if n)
    idx, g = _route(jnp.swapaxes(logits[:, :, :E], 1, 2), segs)
    ids = jnp.transpose(idx[:, :, :E], (2, 0, 1)).astype(jnp.int32).reshape(-1)
    return _ffn(h.reshape(B * S, D), ids, g.reshape(-1, LANE), w1, w3, w2, l).reshape(B, S, D)


def kernel(x, c, ctx, c_ctx, ada_w, ada_b, norm_mix_pre, norm_mix_post, norm_ffn_pre, norm_ffn_post,
           ab_w_in, ab_w_out, mlstm_gate_b, mlstm_norm, na_rpb,
           cd_w_in, cd_w_out, gqa_sink, lru_conv_w, lru_conv_b, lru_gate_w, lru_gate_b, lru_lambda,
           moe_router, moe_w1, moe_w3, moe_w2):
    B, N, D = x.shape
    Lc = ctx.shape[1]
    S = Lc + N
    assert Lc == TS and N % TS == 0 and B < SUB
    ctx_row = B
    cond = jnp.concatenate([c, c_ctx[None], jnp.zeros((SUB - B - 1, D), F32)], axis=0)
    mod = _ada(cond, ada_w, ada_b).reshape(-1, 1, D)
    router = jnp.pad(moe_router, ((0, 0), (0, 0), (0, LANE - N_EXPERTS))).astype(BF16)

    h = _prenorm(ctx, x, norm_mix_pre[0], mod, 0, ctx_row)
    hf = h.reshape(B * S, D)
    w_in = ab_w_in[0]
    o0 = 3 * MLSTM_QK
    n0 = o0 + MLSTM_V + MLSTM_GATES
    w_og = jnp.pad(w_in[:, o0:n0], ((0, 0), (0, LANE - MLSTM_GATES))).astype(BF16)
    P = _matmul(hf, w_in[:, :o0].astype(BF16), BF16).reshape(B, S, -1)
    PN = _matmul(hf, w_in[:, n0:].astype(BF16), BF16).reshape(B, S, -1)
    OG = _matmul(hf, w_og, F32).reshape(B, S, -1)
    gcol = OG[:, :, MLSTM_V:MLSTM_V + MLSTM_GATES]
    Hd = _mlstm(P, gcol, jnp.swapaxes(gcol, 1, 2), mlstm_gate_b[0])
    ML = _mlstm_out(Hd, OG, mlstm_norm[0])
    NL = _na(PN, _na_bias(na_rpb[0]), Lc, 0)
    Y = _matmul2(ML.reshape(B * S, -1), NL.reshape(B * S, -1), ab_w_out[0].astype(BF16)).reshape(B, S, D)
    xs, hb, lg = _resid((ctx, x), Y, norm_mix_post[0], mod, 0, 2, ctx_row, 0, (norm_ffn_pre[0], 0, 3, 4, router[0]))
    Y = _moe(hb, lg, moe_w1, moe_w3, moe_w2, 0, Lc)
    xs, h = _resid(xs, Y, norm_ffn_post[0], mod, 0, 5, ctx_row, 0, (norm_mix_pre[1], 1, 0, 1, None))

    P1 = _matmul(h.reshape(B * S, D), cd_w_in[0].astype(BF16), F32).reshape(B, S, -1)
    cos, sin = _rope_tables(Lc, N)
    AL = _gqa(P1, cos, sin, gqa_sink[0], Lc)
    RL = _lru(P1, GQA_Q + 2 * GQA_KV, GQA_Q + 2 * GQA_KV + LRU_WIDTH,
              lru_conv_w[0], lru_conv_b[0], lru_gate_w[0], lru_gate_b[0], lru_lambda[0], Lc)
    Y = _matmul2(AL.reshape(B * N, -1), RL.reshape(B * N, -1), cd_w_out[0].astype(BF16)).reshape(B, N, D)
    xl, hb, lg = _resid(xs, Y, norm_mix_post[1], mod, 1, 2, None, Lc // TS, (norm_ffn_pre[1], 1, 3, 4, router[1]))
    Y = _moe(hb, lg, moe_w1, moe_w3, moe_w2, 1, 0)
    (xl,) = _resid(xl, Y, norm_ffn_post[1], mod, 1, 5, None, 0, None)
    return xl
```

```python
import functools

import numpy as np
import jax
import jax.numpy as jnp
from jax import lax
from jax.experimental import pallas as pl
from jax.experimental.pallas import tpu as pltpu

F32 = jnp.float32
BF16 = jnp.bfloat16
EPS = 1e-6
NEG = -1e30

GRID_W = 64
MLSTM_HEADS = 4
MLSTM_DK = 256
MLSTM_DV = 256
MLSTM_QK = MLSTM_HEADS * MLSTM_DK
MLSTM_V = MLSTM_HEADS * MLSTM_DV
MLSTM_GATES = 2 * 2 * MLSTM_HEADS
NA_HEADS = 8
NA_DIM = 128
NA_WIDTH = NA_HEADS * NA_DIM
NA_KH = 8
NA_KW = 16
GQA_HEADS = 16
GQA_KV_HEADS = 4
GQA_DIM = 64
GQA_Q = GQA_HEADS * GQA_DIM
GQA_KV = GQA_KV_HEADS * GQA_DIM
GQA_WINDOW = 128
ROPE_THETA = 10000.0
LRU_WIDTH = 1024
LRU_BLOCKS = 8
LRU_BS = LRU_WIDTH // LRU_BLOCKS
LRU_C = 8.0
N_EXPERTS = 16
EC_FACTOR = 2

TS = 256
LANE = 128
SUB = 8
VMEM_LIMIT = 56 * 1024 * 1024


def _cp(*sem):
    return pltpu.CompilerParams(dimension_semantics=sem, vmem_limit_bytes=VMEM_LIMIT)


def _pick(n, prefs):
    for p in prefs:
        if n % p == 0:
            return p
    return n


def _sigmoid(x):
    return 1.0 / (1.0 + jnp.exp(-x))


def _dot(a, b):
    return jnp.dot(a, b, preferred_element_type=F32)


def _dot_nt(a, b):
    return lax.dot_general(a, b, (((1,), (1,)), ((), ())), preferred_element_type=F32)


def _dot_tn(a, b):
    return lax.dot_general(a, b, (((0,), (0,)), ((), ())), preferred_element_type=F32)


def _rms(x, w):
    return x * lax.rsqrt(jnp.mean(x * x, axis=-1, keepdims=True) + EPS) * w


def _ada_kernel(c_ref, w_ref, b_ref, o_ref):
    c = c_ref[...]
    a = (c * _sigmoid(c)).astype(BF16)
    o_ref[0] = _dot(a, w_ref[0].astype(BF16)) + b_ref[0]


def _ada(cond, ada_w, ada_b):
    L, D, D6 = ada_w.shape
    tn = _pick(D6, (1024, 512, 256, 128))
    return pl.pallas_call(
        _ada_kernel, grid=(L, D6 // tn),
        in_specs=[pl.BlockSpec((SUB, D), lambda l, j: (0, 0)),
                  pl.BlockSpec((1, D, tn), lambda l, j: (l, 0, j)),
                  pl.BlockSpec((1, 1, tn), lambda l, j: (l, 0, j))],
        out_specs=pl.BlockSpec((1, SUB, tn), lambda l, j: (l, 0, j)),
        out_shape=jax.ShapeDtypeStruct((L, SUB, D6), F32),
        compiler_params=_cp("parallel", "parallel"), name="ada",
    )(cond, ada_w, ada_b.reshape(L, 1, D6))


def _mod_map(l, k, ctx_row, has_ctx):
    if has_ctx:
        return lambda b, s: ((l * SUB + jnp.where(s == 0, ctx_row, b)) * 6 + k, 0, 0)
    return lambda b, s: ((l * SUB + b) * 6 + k, 0, 0)


def _ctx_lat_specs(D):
    return [pl.BlockSpec((1, TS, D), lambda b, s: (b, 0, 0)),
            pl.BlockSpec((1, TS, D), lambda b, s: (b, jnp.maximum(s - 1, 0), 0))]


def _ctx_lat_tile(ctx_ref, lat_ref):
    return jnp.where(pl.program_id(1) == 0, ctx_ref[0], lat_ref[0])


def _prenorm_kernel(ctx_ref, lat_ref, w_ref, sh_ref, sc_ref, o_ref):
    y = _rms(_ctx_lat_tile(ctx_ref, lat_ref), w_ref[...])
    o_ref[0] = (y * (1.0 + sc_ref[0]) + sh_ref[0]).astype(o_ref.dtype)


def _prenorm(ctx, lat, w, mod, l, ctx_row):
    B, N, D = lat.shape
    S = ctx.shape[1] + N
    vec = lambda k: pl.BlockSpec((1, 1, D), _mod_map(l, k, ctx_row, True))
    return pl.pallas_call(
        _prenorm_kernel, grid=(B, S // TS),
        in_specs=_ctx_lat_specs(D) + [pl.BlockSpec((1, D), lambda b, s: (0, 0)), vec(0), vec(1)],
        out_specs=pl.BlockSpec((1, TS, D), lambda b, s: (b, s, 0)),
        out_shape=jax.ShapeDtypeStruct((B, S, D), BF16),
        compiler_params=_cp("parallel", "parallel"), name="prenorm",
    )(ctx, lat, w.reshape(1, D), mod, mod)


def _resid_kernel(nxt_mode, split_x, x_ref, *rest):
    if split_x:
        x = _ctx_lat_tile(x_ref, rest[0])
        rest = rest[1:]
    else:
        x = x_ref[0]
    y_ref, wpost_ref, g_ref, *rest = rest
    xn = x + g_ref[0] * _rms(y_ref[0], wpost_ref[...])
    if nxt_mode is None:
        (xo_ref,) = rest
    elif nxt_mode == "mixer":
        wpre_ref, sh_ref, sc_ref, xo_ref, ho_ref = rest
    else:
        wpre_ref, sh_ref, sc_ref, router_ref, xo_ref, ho_ref, lo_ref = rest
    xo_ref[0] = xn
    if nxt_mode is not None:
        h = _rms(xn, wpre_ref[...]) * (1.0 + sc_ref[0]) + sh_ref[0]
        ho_ref[0] = h.astype(ho_ref.dtype)
        if nxt_mode == "experts":
            lo_ref[0] = _dot(h.astype(BF16), router_ref[...])


def _resid(x, y, wpost, mod, l, kg, ctx_row, x_off, nxt):
    B, Sy, D = y.shape
    split_x = isinstance(x, tuple)
    has_ctx = ctx_row is not None and (split_x or (x_off == 0 and x.shape[1] == Sy))
    vec = lambda ll, k: pl.BlockSpec((1, 1, D), _mod_map(ll, k, ctx_row, has_ctx))
    row = pl.BlockSpec((1, D), lambda b, s: (0, 0))
    tile = pl.BlockSpec((1, TS, D), lambda b, s: (b, s, 0))
    if split_x:
        in_specs = _ctx_lat_specs(D)
        args = list(x)
    else:
        in_specs = [pl.BlockSpec((1, TS, D), lambda b, s: (b, s + x_off, 0))]
        args = [x]
    in_specs += [tile, row, vec(l, kg)]
    args += [y, wpost.reshape(1, D), mod]
    out_specs = [tile]
    out_shape = [jax.ShapeDtypeStruct((B, Sy, D), F32)]
    mode = None
    if nxt is not None:
        wpre, ln, ksh, ksc, router = nxt
        in_specs += [row, vec(ln, ksh), vec(ln, ksc)]
        args += [wpre.reshape(1, D), mod, mod]
        out_specs.append(tile)
        if router is None:
            mode = "mixer"
            out_shape.append(jax.ShapeDtypeStruct((B, Sy, D), BF16))
        else:
            mode = "experts"
            in_specs.append(pl.BlockSpec((D, LANE), lambda b, s: (0, 0)))
            args.append(router)
            out_specs.append(pl.BlockSpec((1, TS, LANE), lambda b, s: (b, s, 0)))
            out_shape += [jax.ShapeDtypeStruct((B, Sy, D), F32), jax.ShapeDtypeStruct((B, Sy, LANE), F32)]
    out = pl.pallas_call(
        functools.partial(_resid_kernel, mode, split_x), grid=(B, Sy // TS),
        in_specs=in_specs, out_specs=out_specs, out_shape=out_shape,
        compiler_params=_cp("parallel", "parallel"), name="resid",
    )(*args)
    return out


def _mm_kernel(x_ref, w_ref, o_ref):
    o_ref[...] = _dot(x_ref[...].astype(BF16), w_ref[...]).astype(o_ref.dtype)


def _matmul(x, w, out_dtype):
    M, K = x.shape
    N = w.shape[1]
    tm = _pick(M, (1024, 512, 256))
    tn = _pick(N, (1024, 512, 384, 256, 128))
    return pl.pallas_call(
        _mm_kernel, grid=(M // tm, N // tn),
        in_specs=[pl.BlockSpec((tm, K), lambda i, j: (i, 0)),
                  pl.BlockSpec((K, tn), lambda i, j: (0, j))],
        out_specs=pl.BlockSpec((tm, tn), lambda i, j: (i, j)),
        out_shape=jax.ShapeDtypeStruct((M, N), out_dtype),
        compiler_params=_cp("parallel", "parallel"), name="matmul",
    )(x, w)


def _mm2_kernel(x1_ref, x2_ref, w1_ref, w2_ref, o_ref):
    o_ref[...] = _dot(x1_ref[...], w1_ref[...]) + _dot(x2_ref[...], w2_ref[...])


def _matmul2(x1, x2, w):
    M, K1 = x1.shape
    K2 = x2.shape[1]
    assert K1 == K2 and w.shape[0] == K1 + K2
    N = w.shape[1]
    tm = _pick(M, (1024, 512, 256))
    tn = _pick(N, (512, 256, 128))
    return pl.pallas_call(
        _mm2_kernel, grid=(M // tm, N // tn),
        in_specs=[pl.BlockSpec((tm, K1), lambda i, j: (i, 0)),
                  pl.BlockSpec((tm, K2), lambda i, j: (i, 0)),
                  pl.BlockSpec((K1, tn), lambda i, j: (0, j)),
                  pl.BlockSpec((K2, tn), lambda i, j: (1, j))],
        out_specs=pl.BlockSpec((tm, tn), lambda i, j: (i, j)),
        out_shape=jax.ShapeDtypeStruct((M, N), F32),
        compiler_params=_cp("parallel", "parallel"), name="matmul2",
    )(x1, x2, w, w)


def _log_sigmoid(x):
    return jnp.minimum(x, 0.0) - jnp.log1p(jnp.exp(-jnp.abs(x)))


def _mlstm_kernel(q_ref, k_ref, v_ref, gc_ref, gr_ref, bc_ref, br_ref, o_ref, C_ref, n_ref, m_ref):
    H, dk, dv, L = MLSTM_HEADS, MLSTM_DK, MLSTM_DV, TS
    d = pl.program_id(1)
    t = pl.program_id(2)

    @pl.when(t == 0)
    def _():
        C_ref[...] = jnp.zeros_like(C_ref)
        n_ref[...] = jnp.zeros_like(n_ref)
        m_ref[...] = jnp.zeros_like(m_ref)

    fwd = d == 0
    ri = lax.broadcasted_iota(jnp.int32, (L, L), 0)
    ci = lax.broadcasted_iota(jnp.int32, (L, L), 1)
    diff = (ci - ri) * (1 - 2 * d)
    causal = diff <= 0
    causal_f = jnp.where(causal, 1.0, 0.0)
    causal_t = jnp.where(diff >= 0, 1.0, 0.0)
    gc = gc_ref[0] + bc_ref[...]
    gr = gr_ref[0] + br_ref[...]
    lfc = _log_sigmoid(gc)
    lfr = _log_sigmoid(gr)
    hi = lax.Precision.HIGHEST
    bcol_all = jnp.dot(causal_f, lfc, precision=hi, preferred_element_type=F32)
    brow_all = jnp.dot(lfr, causal_t, precision=hi, preferred_element_type=F32)
    tot_all = jnp.sum(lfr, axis=-1, keepdims=True)
    sel = lambda a, b: jnp.where(fwd, a, b)
    scale = dk ** -0.5
    heads = range(H)
    qs = [q_ref[0, :, h * dk:(h + 1) * dk] for h in heads]
    ks = [k_ref[0, :, h * dk:(h + 1) * dk] * scale for h in heads]
    vs = [v_ref[0, :, h * dv:(h + 1) * dv] for h in heads]
    qk = [_dot_nt(qs[h], ks[h]) for h in heads]
    qc = [_dot(qs[h], C_ref[h].astype(BF16)) for h in heads]
    ms = [m_ref[h] for h in heads]
    bcs, brs, irs, decays, m_news, wks, kvs = [], [], [], [], [], [], []
    for h in heads:
        ic = sel(gc[:, h:h + 1], gc[:, 2 * H + h:2 * H + h + 1])
        ir = sel(gr[h:h + 1], gr[2 * H + h:2 * H + h + 1])
        bc = sel(bcol_all[:, H + h:H + h + 1], bcol_all[:, 3 * H + h:3 * H + h + 1])
        br = sel(brow_all[H + h:H + h + 1], brow_all[3 * H + h:3 * H + h + 1])
        tot = sel(tot_all[H + h:H + h + 1], tot_all[3 * H + h:3 * H + h + 1])
        g_c = tot - bc + ic
        g_r = tot - br + ir
        m_new = jnp.maximum(tot + ms[h], jnp.max(g_r, axis=-1, keepdims=True))
        wk = jnp.exp(g_c - m_new) * ks[h].astype(F32)
        bcs.append(bc); brs.append(br); irs.append(ir); m_news.append(m_new); wks.append(wk)
        decays.append(jnp.exp(tot + ms[h] - m_new))
        kvs.append(_dot_tn(wk.astype(BF16), vs[h]))
    ss, mjs, w_inters = [], [], []
    for h in heads:
        dmat = jnp.where(causal, bcs[h] - brs[h] + irs[h], NEG)
        inter = bcs[h] + ms[h]
        mj = jnp.maximum(inter, jnp.max(dmat, axis=-1, keepdims=True))
        ss.append(qk[h] * jnp.exp(dmat - mj))
        mjs.append(mj)
        w_inters.append(jnp.exp(inter - mj))
    sv = [_dot(ss[h].astype(BF16), vs[h]) for h in heads]
    for h in heads:
        num = w_inters[h] * qc[h] + sv[h]
        qn = jnp.sum(qs[h].astype(F32) * n_ref[h], axis=-1, keepdims=True)
        den = w_inters[h] * qn + jnp.sum(ss[h], axis=-1, keepdims=True)
        o_ref[0, 0, :, h * dv:(h + 1) * dv] = num / jnp.maximum(jnp.abs(den), jnp.exp(-mjs[h]))
        C_ref[h] = decays[h] * C_ref[h] + kvs[h]
        n_ref[h] = decays[h] * n_ref[h] + jnp.sum(wks[h], axis=0, keepdims=True)
        m_ref[h] = m_news[h]


def _mlstm(P, gcol, grow, gate_b):
    B, S, _ = P.shape
    nc = S // TS
    chunk = lambda d, t: jnp.where(d == 0, t, jnp.where(t == 0, 0, nc - t))
    blk = lambda c: pl.BlockSpec((1, TS, MLSTM_QK), lambda b, d, t: (b, chunk(d, t), c))
    G = MLSTM_GATES
    return pl.pallas_call(
        _mlstm_kernel, grid=(B, 2, nc),
        in_specs=[blk(0), blk(1), blk(2),
                  pl.BlockSpec((1, TS, G), lambda b, d, t: (b, chunk(d, t), 0)),
                  pl.BlockSpec((1, G, TS), lambda b, d, t: (b, 0, chunk(d, t))),
                  pl.BlockSpec((1, G), lambda b, d, t: (0, 0)),
                  pl.BlockSpec((G, 1), lambda b, d, t: (0, 0))],
        out_specs=pl.BlockSpec((1, 1, TS, MLSTM_V), lambda b, d, t: (d, b, chunk(d, t), 0)),
        out_shape=jax.ShapeDtypeStruct((2, B, S, MLSTM_V), F32),
        scratch_shapes=[pltpu.VMEM((MLSTM_HEADS, MLSTM_DK, MLSTM_DV), F32),
                        pltpu.VMEM((MLSTM_HEADS, 1, MLSTM_DK), F32),
                        pltpu.VMEM((MLSTM_HEADS, 1, 1), F32)],
        compiler_params=_cp("parallel", "arbitrary", "arbitrary"), name="mlstm",
    )(P, P, P, gcol, grow, gate_b.reshape(1, G), gate_b.reshape(G, 1))


def _mlstm_out_kernel(hf_ref, hr_ref, o_ref, gain_ref, out_ref):
    dv = MLSTM_DV
    for h in range(MLSTM_HEADS):
        sl = slice(h * dv, (h + 1) * dv)
        x = hf_ref[0, 0, :, sl] + hr_ref[0, 0, :, sl]
        out_ref[0, :, sl] = (_rms(x, gain_ref[:, sl]) * _sigmoid(o_ref[0, :, sl])).astype(out_ref.dtype)


def _mlstm_out(Hd, OG, gain):
    _, B, S, V = Hd.shape
    return pl.pallas_call(
        _mlstm_out_kernel, grid=(B, S // TS),
        in_specs=[pl.BlockSpec((1, 1, TS, V), lambda b, s: (0, b, s, 0)),
                  pl.BlockSpec((1, 1, TS, V), lambda b, s: (1, b, s, 0)),
                  pl.BlockSpec((1, TS, V), lambda b, s: (b, s, 0)),
                  pl.BlockSpec((1, V), lambda b, s: (0, 0))],
        out_specs=pl.BlockSpec((1, TS, V), lambda b, s: (b, s, 0)),
        out_shape=jax.ShapeDtypeStruct((B, S, V), BF16),
        compiler_params=_cp("parallel", "parallel"), name="mlstm_out",
    )(Hd, Hd, OG, gain.reshape(1, V))


NA_QR = 4
NA_KR = NA_KH + 2 * NA_QR - 4
NA_GROUPS = 4


def _na_bias(rpb):
    a = np.arange(NA_QR)
    kr = np.arange(NA_KR)
    qrel = NA_QR * np.arange(3)[:, None] + a[None, :]
    r0rel = np.stack([0 * a, a, NA_KR - NA_KH + 0 * a])
    valid_r = (kr[None, None] >= r0rel[..., None]) & (kr[None, None] < r0rel[..., None] + NA_KH)
    drow = np.clip(kr[None, None] - qrel[..., None] + (NA_KH - 1), 0, 2 * NA_KH - 2)
    w = np.arange(GRID_W)[:, None]
    cc = np.arange(GRID_W)[None, :]
    col_start = np.clip(w - NA_KW // 2, 0, GRID_W - NA_KW)
    valid_c = (cc >= col_start) & (cc < col_start + NA_KW)
    dcol = np.clip(cc - w + (NA_KW - 1), 0, 2 * NA_KW - 2)
    pick_r = jnp.asarray(np.eye(2 * NA_KH - 1, dtype=np.float32)[drow])
    pick_c = jnp.asarray(np.eye(2 * NA_KW - 1, dtype=np.float32)[dcol])
    t = jnp.einsum("cakr,hrs,wzs->chawkz", pick_r, rpb.astype(F32), pick_c,
                   precision=lax.Precision.HIGHEST)
    valid = valid_r[:, None, :, None, :, None] & valid_c[None, None, None, :, None, :]
    t = jnp.where(valid, t, NEG)
    return t.reshape(3, NA_HEADS, NA_QR * GRID_W, NA_KR * GRID_W)


def _na_kernel(q_ref, k_ref, v_ref, bias_ref, o_ref, *, Lc, rows_n):
    W, d = GRID_W, NA_DIM
    scale = d ** -0.5
    kc = k_ref[0, 0:Lc, :]
    vc = v_ref[0, 0:Lc, :]
    sc = _dot_nt(q_ref[0, 0:Lc, :], kc) * scale
    pc = jnp.exp(sc - jnp.max(sc, axis=-1, keepdims=True))
    oc = _dot(pc.astype(BF16), vc) / jnp.sum(pc, axis=-1, keepdims=True)
    o_ref[0, 0:Lc, :] = oc.astype(o_ref.dtype)

    def groups(it, carry):
        geo, raw = [], []
        for j in range(NA_GROUPS):
            r = (it * NA_GROUPS + j) * NA_QR
            kr0 = jnp.clip(r - NA_KH // 2, 0, rows_n - NA_KR)
            qoff = pl.multiple_of(Lc + r * W, NA_QR * W)
            koff = pl.multiple_of(Lc + kr0 * W, W)
            qg = q_ref[0, pl.ds(qoff, NA_QR * W), :]
            geo.append((qoff, koff, (r - kr0) // NA_QR))
            raw.append((_dot_nt(qg, k_ref[0, pl.ds(koff, NA_KR * W), :]), _dot_nt(qg, kc)))
        probs = []
        for (qoff, koff, cls), (qk_w, qk_x) in zip(geo, raw):
            sw = qk_w * scale + bias_ref[cls, 0]
            sx = qk_x * scale
            m = jnp.maximum(jnp.max(sw, axis=-1, keepdims=True), jnp.max(sx, axis=-1, keepdims=True))
            pw = jnp.exp(sw - m)
            px = jnp.exp(sx - m)
            probs.append((pw, px, jnp.sum(pw, axis=-1, keepdims=True) + jnp.sum(px, axis=-1, keepdims=True)))
        for (qoff, koff, cls), (pw, px, l) in zip(geo, probs):
            o = (_dot(pw.astype(BF16), v_ref[0, pl.ds(koff, NA_KR * W), :]) + _dot(px.astype(BF16), vc)) / l
            o_ref[0, pl.ds(qoff, NA_QR * W), :] = o.astype(o_ref.dtype)
        return carry

    lax.fori_loop(0, rows_n // (NA_QR * NA_GROUPS), groups, 0)


def _na(P, bias, Lc, col0):
    B, S, _ = P.shape
    rows_n = (S - Lc) // GRID_W
    assert rows_n % (NA_QR * NA_GROUPS) == 0 and rows_n >= NA_KR
    c0 = col0 // NA_DIM
    blk = lambda c: pl.BlockSpec((1, S, NA_DIM), lambda b, h: (b, 0, c0 + c * NA_HEADS + h))
    return pl.pallas_call(
        functools.partial(_na_kernel, Lc=Lc, rows_n=rows_n), grid=(B, NA_HEADS),
        in_specs=[blk(0), blk(1), blk(2),
                  pl.BlockSpec((3, 1, NA_QR * GRID_W, NA_KR * GRID_W), lambda b, h: (0, h, 0, 0))],
        out_specs=pl.BlockSpec((1, S, NA_DIM), lambda b, h: (b, 0, h)),
        out_shape=jax.ShapeDtypeStruct((B, S, NA_WIDTH), BF16),
        compiler_params=_cp("parallel", "parallel"), name="na",
    )(P, P, P, bias)


def _rope_tables(Lc, N):
    quarter = GQA_DIM // 4
    t = jnp.arange(N)
    inv = ROPE_THETA ** (-(jnp.arange(quarter, dtype=F32) / quarter))
    ang_r = (t // GRID_W).astype(F32)[:, None] * inv[None, :]
    ang_c = (t % GRID_W).astype(F32)[:, None] * inv[None, :]
    cos = jnp.concatenate([jnp.cos(ang_r)] * 2 + [jnp.cos(ang_c)] * 2, axis=-1)
    sin = jnp.concatenate([-jnp.sin(ang_r), jnp.sin(ang_r), -jnp.sin(ang_c), jnp.sin(ang_c)], axis=-1)
    cos = jnp.concatenate([jnp.ones((Lc, GQA_DIM), F32), cos], axis=0)
    sin = jnp.concatenate([jnp.zeros((Lc, GQA_DIM), F32), sin], axis=0)
    return jnp.tile(cos, (1, LANE // GQA_DIM)), jnp.tile(sin, (1, LANE // GQA_DIM))


def _rope_rot(x, cos, sin):
    quarter = GQA_DIM // 4
    lane = lax.broadcasted_iota(jnp.int32, x.shape, 1)
    first = (lane % (2 * quarter)) < quarter
    rot = jnp.where(first, pltpu.roll(x, LANE - quarter, 1), pltpu.roll(x, quarter, 1))
    return x * cos + rot * sin


def _gqa_kernel(sink_ref, q_ref, k_ref, v_ref, cos_ref, sin_ref, o_ref, kd_ref, vd_ref, *, Lc, N):
    W, dh, KV = GQA_WINDOW, GQA_DIM, GQA_KV_HEADS
    G = GQA_HEADS // KV
    S = Lc + N
    i = pl.program_id(1)
    scale = dh ** -0.5

    @pl.when(i == 0)
    def _():
        low_t = lax.broadcasted_iota(jnp.int32, (TS, LANE), 1) < dh
        zero = jnp.zeros((W, LANE), BF16)
        for kh in range(KV):
            for ref in (kd_ref, vd_ref):
                ref[kh, Lc:Lc + W, :] = zero
                ref[kh, Lc + W + N:Lc + 2 * W + N, :] = zero

        def build(c, carry):
            r0 = pl.multiple_of(c * TS, TS)
            dst = pl.multiple_of(r0 + jnp.where(c >= Lc // TS, W, 0), W)
            cs = cos_ref[pl.ds(r0, TS), :]
            sn = sin_ref[pl.ds(r0, TS), :]
            for hh in range(KV // 2):
                kr = _rope_rot(k_ref[0, pl.ds(r0, TS), hh * LANE:(hh + 1) * LANE], cs, sn)
                vx = v_ref[0, pl.ds(r0, TS), hh * LANE:(hh + 1) * LANE]
                for ref, a in ((kd_ref, kr), (vd_ref, vx)):
                    sw = pltpu.roll(a, dh, 1)
                    ref[2 * hh, pl.ds(dst, TS), :] = jnp.where(low_t, a, sw).astype(BF16)
                    ref[2 * hh + 1, pl.ds(dst, TS), :] = jnp.where(low_t, sw, a).astype(BF16)
            return carry

        lax.fori_loop(0, S // TS, build, 0)

    span = 3 * W
    off = pl.multiple_of(Lc + i * W, W)
    cq = cos_ref[pl.ds(off, W), :]
    sq = sin_ref[pl.ds(off, W), :]
    ii = lax.broadcasted_iota(jnp.int32, (G * W, span), 0) % W
    jj = lax.broadcasted_iota(jnp.int32, (G * W, span), 1)
    kpos = i * W - W + jj
    valid = (kpos >= 0) & (kpos < N) & (jnp.abs(ii - (jj - W)) <= W)
    rg = lax.broadcasted_iota(jnp.int32, (G * W, 1), 0) // W
    low = lax.broadcasted_iota(jnp.int32, (W, LANE), 1) < dh
    raw = []
    for kh in range(KV):
        parts = []
        for slab in range(G // 2):
            j = kh * (G // 2) + slab
            qr = _rope_rot(q_ref[0, :, j * LANE:(j + 1) * LANE] * scale, cq, sq)
            parts += [jnp.where(low, qr, 0.0).astype(BF16), jnp.where(low, 0.0, qr).astype(BF16)]
        q4 = jnp.concatenate(parts, axis=0)
        raw.append((_dot_nt(q4, kd_ref[kh, pl.ds(off, span), :]), _dot_nt(q4, kd_ref[kh, 0:Lc, :])))
    for kh in range(KV):
        vw = vd_ref[kh, pl.ds(off, span), :]
        vc = vd_ref[kh, 0:Lc, :]
        sw = jnp.where(valid, raw[kh][0], NEG)
        sx = raw[kh][1]
        sk = sink_ref[G * kh + G - 1]
        for g in range(G - 2, -1, -1):
            sk = jnp.where(rg == g, sink_ref[G * kh + g], sk)
        m = jnp.maximum(jnp.maximum(jnp.max(sw, axis=-1, keepdims=True),
                                    jnp.max(sx, axis=-1, keepdims=True)), sk)
        pw = jnp.exp(sw - m)
        px = jnp.exp(sx - m)
        l = jnp.sum(pw, axis=-1, keepdims=True) + jnp.sum(px, axis=-1, keepdims=True) + jnp.exp(sk - m)
        o = (_dot(pw.astype(BF16), vw) + _dot(px.astype(BF16), vc)) / l
        for slab in range(G // 2):
            j = kh * (G // 2) + slab
            o_ref[0, :, j * LANE:(j + 1) * LANE] = jnp.where(
                low, o[2 * slab * W:(2 * slab + 1) * W], o[(2 * slab + 1) * W:(2 * slab + 2) * W]
            ).astype(o_ref.dtype)


def _gqa(P1, cos, sin, sink, Lc):
    B, S, _ = P1.shape
    N = S - Lc
    W = GQA_WINDOW
    rows = Lc + 2 * W + N
    table = pl.BlockSpec((S, LANE), lambda b, i: (0, 0))
    return pl.pallas_call(
        functools.partial(_gqa_kernel, Lc=Lc, N=N), grid=(B, N // W),
        in_specs=[pl.BlockSpec(memory_space=pltpu.SMEM),
                  pl.BlockSpec((1, W, GQA_Q), lambda b, i: (b, i + Lc // W, 0)),
                  pl.BlockSpec((1, S, GQA_KV), lambda b, i: (b, 0, GQA_Q // GQA_KV)),
                  pl.BlockSpec((1, S, GQA_KV), lambda b, i: (b, 0, GQA_Q // GQA_KV + 1)),
                  table, table],
        out_specs=pl.BlockSpec((1, W, GQA_Q), lambda b, i: (b, i, 0)),
        out_shape=jax.ShapeDtypeStruct((B, N, GQA_Q), BF16),
        scratch_shapes=[pltpu.VMEM((GQA_KV_HEADS, rows, LANE), BF16),
                        pltpu.VMEM((GQA_KV_HEADS, rows, LANE), BF16)],
        compiler_params=_cp("parallel", "arbitrary"), name="gqa",
    )(sink, P1, P1, P1, cos, sin)


def _gelu_tanh(x):
    return 0.5 * x * (1.0 + jnp.tanh(np.sqrt(2.0 / np.pi) * (x + 0.044715 * (x * x * x))))


def _lru_kernel(x_ref, g_ref, cw_ref, cb_ref, gw_ref, gb_ref, lam_ref, o_ref,
                af_ref, bf_ref, ar_ref, br_ref, *, Lc, N):
    S = Lc + N
    CH = 128
    LW = x_ref.shape[-1]
    nb = LW // LRU_BS
    nt = S // SUB
    ntc = Lc // SUB
    row = lax.broadcasted_iota(jnp.int32, (CH, LW), 0)
    crow = row % SUB
    first_row = row == 0
    last_row = row == CH - 1
    last_row2 = row == CH - 2
    sps = []
    for d in range(2):
        lam = lam_ref[d:d + 1, :]
        sps.append(jnp.maximum(-lam, 0.0) + jnp.log1p(jnp.exp(-jnp.abs(lam))))

    def gates(c, carry):
        off = pl.multiple_of(c * CH, CH)
        xc = x_ref[0, pl.ds(off, CH), :]
        prev = x_ref[0, pl.ds(pl.multiple_of(jnp.maximum(off - SUB, 0), SUB), SUB), :]
        nxt = x_ref[0, pl.ds(pl.multiple_of(jnp.minimum(off + CH, S - SUB), SUB), SUB), :]
        pz = jnp.where((c == 0) | (c == Lc // CH), 0.0, 1.0)
        nz = jnp.where((c == Lc // CH - 1) | (c == S // CH - 1), 0.0, 1.0)
        p7 = prev[SUB - 1:SUB, :] * pz
        n0 = nxt[0:1, :] * nz
        n1 = nxt[1:2, :] * nz
        xm1 = jnp.where(first_row, p7, pltpu.roll(xc, 1, 0))
        xp1 = jnp.where(last_row, n0, pltpu.roll(xc, CH - 1, 0))
        xp2 = jnp.where(last_row2, n0, jnp.where(last_row, n1, pltpu.roll(xc, CH - 2, 0)))
        u = (xm1 * cw_ref[0:1, :] + cb_ref[...] + xc * cw_ref[1:2, :] + xp1 * cw_ref[2:3, :]
             + xp2 * cw_ref[3:4, :])
        ub = u.astype(BF16)
        for d, (a_ref, b_ref) in enumerate(((af_ref, bf_ref), (ar_ref, br_ref))):
            pre = []
            for gi in range(2):
                pre.append(jnp.concatenate(
                    [_dot(ub[:, k * LRU_BS:(k + 1) * LRU_BS], gw_ref[d, gi, k].astype(BF16))
                     for k in range(nb)], axis=-1) + gb_ref[d, gi:gi + 1, :])
            r = _sigmoid(pre[0])
            ig = _sigmoid(pre[1])
            a = jnp.exp(-LRU_C * r * sps[d])
            b = jnp.sqrt(1.0 - a * a) * (ig * u)
            for s in (1, 2, 4):
                if d == 0:
                    keep = crow >= s
                    a_sh = pltpu.roll(a, s, 0)
                    b_sh = pltpu.roll(b, s, 0)
                else:
                    keep = crow < SUB - s
                    a_sh = pltpu.roll(a, CH - s, 0)
                    b_sh = pltpu.roll(b, CH - s, 0)
                b = jnp.where(keep, a * b_sh + b, b)
                a = jnp.where(keep, a * a_sh, a)
            a_ref[pl.ds(off, CH), :] = a
            b_ref[pl.ds(off, CH), :] = b
        return carry

    lax.fori_loop(0, S // CH, gates, 0)

    def scan(ft, rt, carry):
        hf, hr = carry
        of = pl.multiple_of(ft * SUB, SUB)
        orv = pl.multiple_of(rt * SUB, SUB)
        tf = af_ref[pl.ds(of, SUB), :] * hf + bf_ref[pl.ds(of, SUB), :]
        tr = ar_ref[pl.ds(orv, SUB), :] * hr + br_ref[pl.ds(orv, SUB), :]
        bf_ref[pl.ds(of, SUB), :] = tf
        br_ref[pl.ds(orv, SUB), :] = tr
        return (jnp.broadcast_to(tf[SUB - 1:SUB, :], (SUB, LW)), jnp.broadcast_to(tr[0:1, :], (SUB, LW)))

    zero = jnp.zeros((SUB, LW), F32)
    carry = lax.fori_loop(0, ntc, lambda i, c: scan(i, ntc - 1 - i, c), (zero, zero), unroll=8)
    lax.fori_loop(ntc, nt, lambda i, c: scan(i, nt - 1 - (i - ntc), c), carry, unroll=8)

    def outp(c, carry):
        off = pl.multiple_of(c * CH, CH)
        src = pl.multiple_of(Lc + off, CH)
        hs = bf_ref[pl.ds(src, CH), :] + br_ref[pl.ds(src, CH), :]
        o_ref[0, pl.ds(off, CH), :] = (hs * _gelu_tanh(g_ref[0, pl.ds(src, CH), :])).astype(o_ref.dtype)
        return carry

    lax.fori_loop(0, N // CH, outp, 0)


def _lru(P1, col_x, col_g, conv_w, conv_b, gate_w, gate_b, lam, Lc):
    B, S, _ = P1.shape
    N = S - Lc
    nb = 2
    lw = nb * LRU_BS
    return pl.pallas_call(
        functools.partial(_lru_kernel, Lc=Lc, N=N), grid=(B, LRU_BLOCKS // nb),
        in_specs=[pl.BlockSpec((1, S, lw), lambda b, j: (b, 0, col_x // lw + j)),
                  pl.BlockSpec((1, S, lw), lambda b, j: (b, 0, col_g // lw + j)),
                  pl.BlockSpec((4, lw), lambda b, j: (0, j)),
                  pl.BlockSpec((1, lw), lambda b, j: (0, j)),
                  pl.BlockSpec((2, 2, nb, LRU_BS, LRU_BS), lambda b, j: (0, 0, j, 0, 0)),
                  pl.BlockSpec((2, 2, lw), lambda b, j: (0, 0, j)),
                  pl.BlockSpec((2, lw), lambda b, j: (0, j))],
        out_specs=pl.BlockSpec((1, N, lw), lambda b, j: (b, 0, j)),
        out_shape=jax.ShapeDtypeStruct((B, N, LRU_WIDTH), BF16),
        scratch_shapes=[pltpu.VMEM((S, lw), F32) for _ in range(4)],
        compiler_params=_cp("parallel", "parallel"), name="lru",
    )(P1, P1, conv_w, conv_b.reshape(1, LRU_WIDTH), gate_w, gate_b, lam)


ROUTE_BISECT = 2048


def _cumsum_lanes(x):
    n = x.shape[-1]
    lane = lax.broadcasted_iota(jnp.int32, x.shape, 1)
    s = 1
    while s < n:
        x = x + jnp.where(lane >= s, pltpu.roll(x, s, 1), 0.0)
        s *= 2
    return x


def _route_kernel(lt_ref, idx_ref, g_ref, cm_ref, af_ref, *, segs, S):
    E = lt_ref.shape[1]
    b = pl.program_id(0)
    lt = lt_ref[0]
    ex = jnp.exp(lt - jnp.max(lt, axis=0, keepdims=True))
    aff = ex / jnp.sum(ex, axis=0, keepdims=True)
    lane_e = lax.broadcasted_iota(jnp.int32, (1, LANE), 1)
    slot0 = 0
    for lo, n, cap in segs:
        a = aff[:, lo:lo + n]

        def bisect(carry):
            lo_v, hi_v, c_lo, it, _ = carry
            mid = lo_v + 0.5 * (hi_v - lo_v)
            c = jnp.sum(jnp.where(a >= mid, 1.0, 0.0), axis=1, keepdims=True)
            ge = c >= cap
            lo_v = jnp.where(ge, mid, lo_v)
            hi_v = jnp.where(ge, hi_v, mid)
            c_lo = jnp.where(ge, c, c_lo)
            mid = lo_v + 0.5 * (hi_v - lo_v)
            done = (c_lo == cap) | (mid <= lo_v) | (mid >= hi_v)
            return lo_v, hi_v, c_lo, it + 1, jnp.min(jnp.where(done, 1, 0))

        lo_v, hi_v, _, _, _ = lax.while_loop(
            lambda carry: (carry[4] == 0) & (carry[3] < ROUTE_BISECT), bisect,
            (jnp.zeros((E, 1), F32), jnp.full((E, 1), 2.0, F32), jnp.full((E, 1), float(n), F32),
             jnp.int32(0), jnp.int32(0)))
        top = jnp.where(a >= hi_v, 1.0, 0.0)
        tie = jnp.where((a >= lo_v) & (a < hi_v), 1.0, 0.0)
        need = cap - jnp.sum(top, axis=1, keepdims=True)
        tie_rank = _cumsum_lanes(tie) - tie
        sel = top + tie * jnp.where(tie_rank < need, 1.0, 0.0)
        cm_ref[:, 0:n] = _cumsum_lanes(sel) * sel
        af_ref[:, 0:n] = a
        tok = (lax.broadcasted_iota(jnp.int32, (1, n), 1) + (lo + b * S)).astype(F32)
        jcol = (lax.broadcasted_iota(jnp.int32, (cap, 1), 0) + 1).astype(F32)

        def per_expert(e, carry):
            oi, og = carry
            ce = cm_ref[pl.ds(e, 1), 0:n]
            ae = af_ref[pl.ds(e, 1), 0:n]
            acc_i = jnp.zeros((cap, LANE), F32)
            acc_g = jnp.zeros((cap, LANE), F32)
            for k in range(n // LANE):
                sl = slice(k * LANE, (k + 1) * LANE)
                hit = ce[:, sl] == jcol
                acc_i = acc_i + jnp.where(hit, tok[:, sl], 0.0)
                acc_g = acc_g + jnp.where(hit, ae[:, sl], 0.0)
            mine = lane_e == e
            return (jnp.where(mine, jnp.sum(acc_i, axis=1, keepdims=True), oi),
                    jnp.where(mine, jnp.sum(acc_g, axis=1, keepdims=True), og))

        zero = jnp.zeros((cap, LANE), F32)
        oi, og = lax.fori_loop(0, E, per_expert, (zero, zero))
        idx_ref[0, slot0:slot0 + cap, :] = oi
        g_ref[0, slot0:slot0 + cap, :] = og
        slot0 += cap


def _route(logits_t, segs):
    B, E, S = logits_t.shape
    slots = sum(cap for _, _, cap in segs)
    nmax = max(n for _, n, _ in segs)
    out = jax.ShapeDtypeStruct((B, slots, LANE), F32)
    blk = pl.BlockSpec((1, slots, LANE), lambda b: (b, 0, 0))
    return pl.pallas_call(
        functools.partial(_route_kernel, segs=segs, S=S), grid=(B,),
        in_specs=[pl.BlockSpec((1, E, S), lambda b: (b, 0, 0))],
        out_specs=[blk, blk], out_shape=[out, out],
        scratch_shapes=[pltpu.VMEM((E, nmax), F32), pltpu.VMEM((E, nmax), F32)],
        compiler_params=_cp("parallel"), name="route",
    )(logits_t)


RING = 4


def _ffn_kernel(ids_ref, h_hbm, w1_ref, w3_ref, w2_ref, g_ref, zero_hbm, out_hbm,
                xf_ref, xb_ref, y_ref, ring_ref, xsem, asem, ssem, *, rm, ns, nf):
    del zero_hbm
    C = rm // nf
    D = xf_ref.shape[-1]
    step = pl.program_id(0)
    f = pl.program_id(1)
    q = step * nf + f
    cur = step % 2
    f0 = pl.multiple_of(f * C, SUB)

    def chunk_tokens(c_step, c_f):
        p = c_step - 1
        return jnp.where(p < 0, rm, jnp.minimum(p, ns - 1) * rm) + c_f * C

    def gather(src, dst, sem, tb, row0, n):
        for i in range(n):
            pltpu.make_async_copy(src.at[pl.ds(ids_ref[tb + i], 1), :], dst.at[pl.ds(row0 + i, 1), :], sem).start()

    def scatter(src, tb, n, sem):
        for i in range(n):
            pltpu.make_async_copy(src.at[pl.ds(i, 1), :], out_hbm.at[pl.ds(ids_ref[tb + i], 1), :], sem).start()

    def wait_rows(buf, sem):
        pltpu.make_async_copy(buf, buf, sem).wait()

    chunk = ring_ref.at[0]
    tb_prev = jnp.where(f == 0, chunk_tokens(step - 1, nf - 1), chunk_tokens(step, f - 1))
    tb_next = jnp.where(f == nf - 1, chunk_tokens(step + 1, 0), chunk_tokens(step, f + 1))

    @pl.when(q == 0)
    def _():
        ring_ref[...] = jnp.zeros_like(ring_ref)
        y_ref[1] = jnp.zeros((rm, D), F32)

        def first(r, carry):
            pltpu.make_async_copy(h_hbm.at[pl.ds(ids_ref[r], 1), :], xf_ref.at[pl.ds(r, 1), :], xsem).start()
            return carry
        lax.fori_loop(0, rm, first, 0, unroll=8)
        scatter(y_ref.at[1, pl.ds(0, C)], chunk_tokens(0, nf - 2), C, ssem.at[0])
        scatter(y_ref.at[1, pl.ds(0, C)], chunk_tokens(0, nf - 3), C, ssem.at[1])
        gather(out_hbm, ring_ref.at[0], asem.at[0], chunk_tokens(0, 0), 0, C)

    @pl.when(f == 0)
    def _():
        wait_rows(xf_ref, xsem)
        xb_ref[...] = xf_ref[...].astype(BF16)
        y_ref[cur] = jnp.zeros((rm, D), F32)

    wait_rows(chunk, ssem.at[q % 3])
    scatter(ring_ref.at[(q + RING - 1) % RING], tb_prev, C, ssem.at[(q + 2) % 3])
    gather(out_hbm, ring_ref.at[(q + 1) % RING], asem.at[(q + 1) % 2], tb_next, 0, C)
    gather(h_hbm, xf_ref, xsem, jnp.minimum(step + 1, ns - 1) * rm + f * C, f0, C)

    x = xb_ref[...]
    u = _dot(x, w1_ref[0, 0].astype(BF16))
    v = _dot(x, w3_ref[0, 0].astype(BF16))
    a = (u * _sigmoid(u) * v).astype(BF16)
    dc = _pick(D, (512, 256, 128))
    for c in range(D // dc):
        y_ref[cur, :, c * dc:(c + 1) * dc] += _dot(a, w2_ref[0, 0, :, c * dc:(c + 1) * dc].astype(BF16))

    slot = q % RING
    wait_rows(chunk, asem.at[q % 2])
    mine = lax.broadcasted_iota(jnp.int32, (C, LANE), 1) == jnp.clip(step - 1, 0, ns - 1) // 2
    gate = jnp.sum(jnp.where(mine, g_ref[pl.ds(f0, C), :], 0.0), axis=-1, keepdims=True)
    ring_ref[slot] = ring_ref[slot] + y_ref[1 - cur, pl.ds(f0, C), :] * gate

    @pl.when(q == (ns + 1) * nf - 1)
    def _():
        scatter(ring_ref.at[slot], chunk_tokens(step, f), C, ssem.at[q % 3])
        for k in range(3):
            wait_rows(chunk, ssem.at[k])
        wait_rows(chunk, asem.at[(q + 1) % 2])
        wait_rows(xf_ref, xsem)


def _ffn(hf, ids, g, w1, w3, w2, l):
    T, D = hf.shape
    E = w1.shape[1]
    R = g.shape[0]
    FF = w1.shape[-1]
    rm = R // 2
    ns = 2 * E
    tf = _pick(FF, (256, 128))
    nf = FF // tf
    assert nf >= 3 and rm % nf == 0 and (rm // nf) % SUB == 0
    ex = lambda s: jnp.minimum(s, ns - 1) // 2
    grid_spec = pltpu.PrefetchScalarGridSpec(
        num_scalar_prefetch=1, grid=(ns + 1, nf),
        in_specs=[pl.BlockSpec(memory_space=pl.ANY),
                  pl.BlockSpec((1, 1, D, tf), lambda s, f, ids: (l, ex(s), 0, f)),
                  pl.BlockSpec((1, 1, D, tf), lambda s, f, ids: (l, ex(s), 0, f)),
                  pl.BlockSpec((1, 1, tf, D), lambda s, f, ids: (l, ex(s), f, 0)),
                  pl.BlockSpec((rm, LANE), lambda s, f, ids: (jnp.maximum(s - 1, 0) % 2, 0)),
                  pl.BlockSpec(memory_space=pl.ANY)],
        out_specs=pl.BlockSpec(memory_space=pl.ANY),
        scratch_shapes=[pltpu.VMEM((rm, D), F32), pltpu.VMEM((rm, D), BF16),
                        pltpu.VMEM((2, rm, D), F32), pltpu.VMEM((RING, rm // nf, D), F32),
                        pltpu.SemaphoreType.DMA, pltpu.SemaphoreType.DMA((2,)), pltpu.SemaphoreType.DMA((3,))])
    return pl.pallas_call(
        functools.partial(_ffn_kernel, rm=rm, ns=ns, nf=nf), grid_spec=grid_spec,
        out_shape=jax.ShapeDtypeStruct((T, D), F32),
        input_output_aliases={6: 0},
        compiler_params=_cp("arbitrary", "arbitrary"), name="ffn",
    )(ids, hf, w1, w3, w2, g, jnp.zeros((T, D), F32))


def _moe(h, logits, w1, w3, w2, l, Lc):
    B, S, D = h.shape
    E = N_EXPERTS
    assert B % 2 == 0
    segs = tuple((lo, n, EC_FACTOR * n // E) for lo, n in ((Lc, S - Lc), (0, Lc)) if n)
    idx, g = _route(jnp.swapaxes(logits[:, :, :E], 1, 2), segs)
    ids = jnp.transpose(idx[:, :, :E], (2, 0, 1)).astype(jnp.int32).reshape(-1)
    return _ffn(h.reshape(B * S, D), ids, g.reshape(-1, LANE), w1, w3, w2, l).reshape(B, S, D)


def kernel(x, c, ctx, c_ctx, ada_w, ada_b, norm_mix_pre, norm_mix_post, norm_ffn_pre, norm_ffn_post,
           ab_w_in, ab_w_out, mlstm_gate_b, mlstm_norm, na_rpb,
           cd_w_in, cd_w_out, gqa_sink, lru_conv_w, lru_conv_b, lru_gate_w, lru_gate_b, lru_lambda,
           moe_router, moe_w1, moe_w3, moe_w2):
    B, N, D = x.shape
    Lc = ctx.shape[1]
    S = Lc + N
    assert Lc == TS and N % TS == 0 and B < SUB
    ctx_row = B
    cond = jnp.concatenate([c, c_ctx[None], jnp.zeros((SUB - B - 1, D), F32)], axis=0)
    mod = _ada(cond, ada_w, ada_b).reshape(-1, 1, D)
    router = jnp.pad(moe_router, ((0, 0), (0, 0), (0, LANE - N_EXPERTS))).astype(BF16)

    h = _prenorm(ctx, x, norm_mix_pre[0], mod, 0, ctx_row)
    hf = h.reshape(B * S, D)
    w_in = ab_w_in[0]
    o0 = 3 * MLSTM_QK
    n0 = o0 + MLSTM_V + MLSTM_GATES
    w_og = jnp.pad(w_in[:, o0:n0], ((0, 0), (0, LANE - MLSTM_GATES))).astype(BF16)
    P = _matmul(hf, w_in[:, :o0].astype(BF16), BF16).reshape(B, S, -1)
    PN = _matmul(hf, w_in[:, n0:].astype(BF16), BF16).reshape(B, S, -1)
    OG = _matmul(hf, w_og, F32).reshape(B, S, -1)
    gcol = OG[:, :, MLSTM_V:MLSTM_V + MLSTM_GATES]
    Hd = _mlstm(P, gcol, jnp.swapaxes(gcol, 1, 2), mlstm_gate_b[0])
    ML = _mlstm_out(Hd, OG, mlstm_norm[0])
    NL = _na(PN, _na_bias(na_rpb[0]), Lc, 0)
    Y = _matmul2(ML.reshape(B * S, -1), NL.reshape(B * S, -1), ab_w_out[0].astype(BF16)).reshape(B, S, D)
    xs, hb, lg = _resid((ctx, x), Y, norm_mix_post[0], mod, 0, 2, ctx_row, 0, (norm_ffn_pre[0], 0, 3, 4, router[0]))
    Y = _moe(hb, lg, moe_w1, moe_w3, moe_w2, 0, Lc)
    xs, h = _resid(xs, Y, norm_ffn_post[0], mod, 0, 5, ctx_row, 0, (norm_mix_pre[1], 1, 0, 1, None))

    P1 = _matmul(h.reshape(B * S, D), cd_w_in[0].astype(BF16), F32).reshape(B, S, -1)
    cos, sin = _rope_tables(Lc, N)
    AL = _gqa(P1, cos, sin, gqa_sink[0], Lc)
    RL = _lru(P1, GQA_Q + 2 * GQA_KV, GQA_Q + 2 * GQA_KV + LRU_WIDTH,
              lru_conv_w[0], lru_conv_b[0], lru_gate_w[0], lru_gate_b[0], lru_lambda[0], Lc)
    Y = _matmul2(AL.reshape(B * N, -1), RL.reshape(B * N, -1), cd_w_out[0].astype(BF16)).reshape(B, N, D)
    xl, hb, lg = _resid(xs, Y, norm_mix_post[1], mod, 1, 2, None, Lc // TS, (norm_ffn_pre[1], 1, 3, 4, router[1]))
    Y = _moe(hb, lg, moe_w1, moe_w3, moe_w2, 1, 0)
    (xl,) = _resid(xl, Y, norm_ffn_post[1], mod, 1, 5, None, 0, None)
    return xl
```

```python
import functools

import numpy as np
import jax
import jax.numpy as jnp
from jax import lax
from jax.experimental import pallas as pl
from jax.experimental.pallas import tpu as pltpu

F32 = jnp.float32
BF16 = jnp.bfloat16
EPS = 1e-6
NEG = -1e30

GRID_W = 64
MLSTM_HEADS = 4
MLSTM_DK = 256
MLSTM_DV = 256
MLSTM_QK = MLSTM_HEADS * MLSTM_DK
MLSTM_V = MLSTM_HEADS * MLSTM_DV
MLSTM_GATES = 2 * 2 * MLSTM_HEADS
NA_HEADS = 8
NA_DIM = 128
NA_WIDTH = NA_HEADS * NA_DIM
NA_KH = 8
NA_KW = 16
GQA_HEADS = 16
GQA_KV_HEADS = 4
GQA_DIM = 64
GQA_Q = GQA_HEADS * GQA_DIM
GQA_KV = GQA_KV_HEADS * GQA_DIM
GQA_WINDOW = 128
ROPE_THETA = 10000.0
LRU_WIDTH = 1024
LRU_BLOCKS = 8
LRU_BS = LRU_WIDTH // LRU_BLOCKS
LRU_C = 8.0
N_EXPERTS = 16
EC_FACTOR = 2

TS = 256
LANE = 128
SUB = 8
VMEM_LIMIT = 56 * 1024 * 1024


def _cp(*sem):
    return pltpu.CompilerParams(dimension_semantics=sem, vmem_limit_bytes=VMEM_LIMIT)


def _pick(n, prefs):
    for p in prefs:
        if n % p == 0:
            return p
    return n


def _sigmoid(x):
    return 1.0 / (1.0 + jnp.exp(-x))


def _dot(a, b):
    return jnp.dot(a, b, preferred_element_type=F32)


def _dot_nt(a, b):
    return lax.dot_general(a, b, (((1,), (1,)), ((), ())), preferred_element_type=F32)


def _dot_tn(a, b):
    return lax.dot_general(a, b, (((0,), (0,)), ((), ())), preferred_element_type=F32)


def _rms(x, w):
    return x * lax.rsqrt(jnp.mean(x * x, axis=-1, keepdims=True) + EPS) * w


def _ada_kernel(c_ref, w_ref, b_ref, o_ref):
    c = c_ref[...]
    a = (c * _sigmoid(c)).astype(BF16)
    o_ref[0] = _dot(a, w_ref[0].astype(BF16)) + b_ref[0]


def _ada(cond, ada_w, ada_b):
    L, D, D6 = ada_w.shape
    tn = _pick(D6, (1024, 512, 256, 128))
    return pl.pallas_call(
        _ada_kernel, grid=(L, D6 // tn),
        in_specs=[pl.BlockSpec((SUB, D), lambda l, j: (0, 0)),
                  pl.BlockSpec((1, D, tn), lambda l, j: (l, 0, j)),
                  pl.BlockSpec((1, 1, tn), lambda l, j: (l, 0, j))],
        out_specs=pl.BlockSpec((1, SUB, tn), lambda l, j: (l, 0, j)),
        out_shape=jax.ShapeDtypeStruct((L, SUB, D6), F32),
        compiler_params=_cp("parallel", "parallel"), name="ada",
    )(cond, ada_w, ada_b.reshape(L, 1, D6))


def _mod_map(l, k, ctx_row, has_ctx):
    if has_ctx:
        return lambda b, s: ((l * SUB + jnp.where(s == 0, ctx_row, b)) * 6 + k, 0, 0)
    return lambda b, s: ((l * SUB + b) * 6 + k, 0, 0)


def _ctx_lat_specs(D):
    return [pl.BlockSpec((1, TS, D), lambda b, s: (b, 0, 0)),
            pl.BlockSpec((1, TS, D), lambda b, s: (b, jnp.maximum(s - 1, 0), 0))]


def _ctx_lat_tile(ctx_ref, lat_ref):
    return jnp.where(pl.program_id(1) == 0, ctx_ref[0], lat_ref[0])


def _prenorm_kernel(ctx_ref, lat_ref, w_ref, sh_ref, sc_ref, o_ref):
    y = _rms(_ctx_lat_tile(ctx_ref, lat_ref), w_ref[...])
    o_ref[0] = (y * (1.0 + sc_ref[0]) + sh_ref[0]).astype(o_ref.dtype)


def _prenorm(ctx, lat, w, mod, l, ctx_row):
    B, N, D = lat.shape
    S = ctx.shape[1] + N
    vec = lambda k: pl.BlockSpec((1, 1, D), _mod_map(l, k, ctx_row, True))
    return pl.pallas_call(
        _prenorm_kernel, grid=(B, S // TS),
        in_specs=_ctx_lat_specs(D) + [pl.BlockSpec((1, D), lambda b, s: (0, 0)), vec(0), vec(1)],
        out_specs=pl.BlockSpec((1, TS, D), lambda b, s: (b, s, 0)),
        out_shape=jax.ShapeDtypeStruct((B, S, D), BF16),
        compiler_params=_cp("parallel", "parallel"), name="prenorm",
    )(ctx, lat, w.reshape(1, D), mod, mod)


def _resid_kernel(nxt_mode, split_x, x_ref, *rest):
    if split_x:
        x = _ctx_lat_tile(x_ref, rest[0])
        rest = rest[1:]
    else:
        x = x_ref[0]
    y_ref, wpost_ref, g_ref, *rest = rest
    xn = x + g_ref[0] * _rms(y_ref[0], wpost_ref[...])
    if nxt_mode is None:
        (xo_ref,) = rest
    elif nxt_mode == "mixer":
        wpre_ref, sh_ref, sc_ref, xo_ref, ho_ref = rest
    else:
        wpre_ref, sh_ref, sc_ref, router_ref, xo_ref, ho_ref, lo_ref = rest
    xo_ref[0] = xn
    if nxt_mode is not None:
        h = _rms(xn, wpre_ref[...]) * (1.0 + sc_ref[0]) + sh_ref[0]
        ho_ref[0] = h.astype(ho_ref.dtype)
        if nxt_mode == "experts":
            lo_ref[0] = _dot(h.astype(BF16), router_ref[...])


def _resid(x, y, wpost, mod, l, kg, ctx_row, x_off, nxt):
    B, Sy, D = y.shape
    split_x = isinstance(x, tuple)
    has_ctx = ctx_row is not None and (split_x or (x_off == 0 and x.shape[1] == Sy))
    vec = lambda ll, k: pl.BlockSpec((1, 1, D), _mod_map(ll, k, ctx_row, has_ctx))
    row = pl.BlockSpec((1, D), lambda b, s: (0, 0))
    tile = pl.BlockSpec((1, TS, D), lambda b, s: (b, s, 0))
    if split_x:
        in_specs = _ctx_lat_specs(D)
        args = list(x)
    else:
        in_specs = [pl.BlockSpec((1, TS, D), lambda b, s: (b, s + x_off, 0))]
        args = [x]
    in_specs += [tile, row, vec(l, kg)]
    args += [y, wpost.reshape(1, D), mod]
    out_specs = [tile]
    out_shape = [jax.ShapeDtypeStruct((B, Sy, D), F32)]
    mode = None
    if nxt is not None:
        wpre, ln, ksh, ksc, router = nxt
        in_specs += [row, vec(ln, ksh), vec(ln, ksc)]
        args += [wpre.reshape(1, D), mod, mod]
        out_specs.append(tile)
        if router is None:
            mode = "mixer"
            out_shape.append(jax.ShapeDtypeStruct((B, Sy, D), BF16))
        else:
            mode = "experts"
            in_specs.append(pl.BlockSpec((D, LANE), lambda b, s: (0, 0)))
            args.append(router)
            out_specs.append(pl.BlockSpec((1, TS, LANE), lambda b, s: (b, s, 0)))
            out_shape += [jax.ShapeDtypeStruct((B, Sy, D), F32), jax.ShapeDtypeStruct((B, Sy, LANE), F32)]
    out = pl.pallas_call(
        functools.partial(_resid_kernel, mode, split_x), grid=(B, Sy // TS),
        in_specs=in_specs, out_specs=out_specs, out_shape=out_shape,
        compiler_params=_cp("parallel", "parallel"), name="resid",
    )(*args)
    return out


def _mm_kernel(x_ref, w_ref, o_ref):
    o_ref[...] = _dot(x_ref[...].astype(BF16), w_ref[...]).astype(o_ref.dtype)


def _matmul(x, w, out_dtype):
    M, K = x.shape
    N = w.shape[1]
    tm = _pick(M, (1024, 512, 256))
    tn = _pick(N, (1024, 896, 512, 384, 256, 128))
    return pl.pallas_call(
        _mm_kernel, grid=(M // tm, N // tn),
        in_specs=[pl.BlockSpec((tm, K), lambda i, j: (i, 0)),
                  pl.BlockSpec((K, tn), lambda i, j: (0, j))],
        out_specs=pl.BlockSpec((tm, tn), lambda i, j: (i, j)),
        out_shape=jax.ShapeDtypeStruct((M, N), out_dtype),
        compiler_params=_cp("parallel", "parallel"), name="matmul",
    )(x, w)


def _mm2_kernel(x1_ref, x2_ref, w1_ref, w2_ref, o_ref):
    o_ref[...] = _dot(x1_ref[...], w1_ref[...]) + _dot(x2_ref[...], w2_ref[...])


def _matmul2(x1, x2, w):
    M, K1 = x1.shape
    K2 = x2.shape[1]
    assert K1 == K2 and w.shape[0] == K1 + K2
    N = w.shape[1]
    tm = _pick(M, (1024, 512, 256))
    tn = _pick(N, (1024, 512, 256, 128))
    return pl.pallas_call(
        _mm2_kernel, grid=(M // tm, N // tn),
        in_specs=[pl.BlockSpec((tm, K1), lambda i, j: (i, 0)),
                  pl.BlockSpec((tm, K2), lambda i, j: (i, 0)),
                  pl.BlockSpec((K1, tn), lambda i, j: (0, j)),
                  pl.BlockSpec((K2, tn), lambda i, j: (1, j))],
        out_specs=pl.BlockSpec((tm, tn), lambda i, j: (i, j)),
        out_shape=jax.ShapeDtypeStruct((M, N), F32),
        compiler_params=_cp("parallel", "parallel"), name="matmul2",
    )(x1, x2, w, w)


def _log_sigmoid(x):
    return jnp.minimum(x, 0.0) - jnp.log1p(jnp.exp(-jnp.abs(x)))


def _mlstm_kernel(q_ref, k_ref, v_ref, gc_ref, gr_ref, bc_ref, br_ref, o_ref, C_ref, n_ref, m_ref):
    H, dk, dv, L = MLSTM_HEADS, MLSTM_DK, MLSTM_DV, TS
    d = pl.program_id(1)
    t = pl.program_id(2)

    @pl.when(t == 0)
    def _():
        C_ref[...] = jnp.zeros_like(C_ref)
        n_ref[...] = jnp.zeros_like(n_ref)
        m_ref[...] = jnp.zeros_like(m_ref)

    fwd = d == 0
    ri = lax.broadcasted_iota(jnp.int32, (L, L), 0)
    ci = lax.broadcasted_iota(jnp.int32, (L, L), 1)
    diff = (ci - ri) * (1 - 2 * d)
    causal = diff <= 0
    causal_f = jnp.where(causal, 1.0, 0.0)
    causal_t = jnp.where(diff >= 0, 1.0, 0.0)
    gc = gc_ref[0] + bc_ref[...]
    gr = gr_ref[0] + br_ref[...]
    lfc = _log_sigmoid(gc)
    lfr = _log_sigmoid(gr)
    hi = lax.Precision.HIGHEST
    bcol_all = jnp.dot(causal_f, lfc, precision=hi, preferred_element_type=F32)
    brow_all = jnp.dot(lfr, causal_t, precision=hi, preferred_element_type=F32)
    tot_all = jnp.sum(lfr, axis=-1, keepdims=True)
    sel = lambda a, b: jnp.where(fwd, a, b)
    scale = dk ** -0.5
    heads = range(H)
    qs = [q_ref[0, :, h * dk:(h + 1) * dk] for h in heads]
    ks = [k_ref[0, :, h * dk:(h + 1) * dk] * scale for h in heads]
    vs = [v_ref[0, :, h * dv:(h + 1) * dv] for h in heads]
    qk = [_dot_nt(qs[h], ks[h]) for h in heads]
    qc = [_dot(qs[h], C_ref[h].astype(BF16)) for h in heads]
    ms = [m_ref[h] for h in heads]
    bcs, brs, irs, decays, m_news, wks, kvs = [], [], [], [], [], [], []
    for h in heads:
        ic = sel(gc[:, h:h + 1], gc[:, 2 * H + h:2 * H + h + 1])
        ir = sel(gr[h:h + 1], gr[2 * H + h:2 * H + h + 1])
        bc = sel(bcol_all[:, H + h:H + h + 1], bcol_all[:, 3 * H + h:3 * H + h + 1])
        br = sel(brow_all[H + h:H + h + 1], brow_all[3 * H + h:3 * H + h + 1])
        tot = sel(tot_all[H + h:H + h + 1], tot_all[3 * H + h:3 * H + h + 1])
        g_c = tot - bc + ic
        g_r = tot - br + ir
        m_new = jnp.maximum(tot + ms[h], jnp.max(g_r, axis=-1, keepdims=True))
        wk = jnp.exp(g_c - m_new) * ks[h].astype(F32)
        bcs.append(bc); brs.append(br); irs.append(ir); m_news.append(m_new); wks.append(wk)
        decays.append(jnp.exp(tot + ms[h] - m_new))
        kvs.append(_dot_tn(wk.astype(BF16), vs[h]))
    ss, mjs, w_inters = [], [], []
    for h in heads:
        dmat = jnp.where(causal, bcs[h] - brs[h] + irs[h], NEG)
        inter = bcs[h] + ms[h]
        mj = jnp.maximum(inter, jnp.max(dmat, axis=-1, keepdims=True))
        ss.append(qk[h] * jnp.exp(dmat - mj))
        mjs.append(mj)
        w_inters.append(jnp.exp(inter - mj))
    sv = [_dot(ss[h].astype(BF16), vs[h]) for h in heads]
    for h in heads:
        num = w_inters[h] * qc[h] + sv[h]
        qn = jnp.sum(qs[h].astype(F32) * n_ref[h], axis=-1, keepdims=True)
        den = w_inters[h] * qn + jnp.sum(ss[h], axis=-1, keepdims=True)
        o_ref[0, 0, :, h * dv:(h + 1) * dv] = num / jnp.maximum(jnp.abs(den), jnp.exp(-mjs[h]))
        C_ref[h] = decays[h] * C_ref[h] + kvs[h]
        n_ref[h] = decays[h] * n_ref[h] + jnp.sum(wks[h], axis=0, keepdims=True)
        m_ref[h] = m_news[h]


def _mlstm(P, gcol, grow, gate_b):
    B, S, _ = P.shape
    nc = S // TS
    chunk = lambda d, t: jnp.where(d == 0, t, jnp.where(t == 0, 0, nc - t))
    blk = lambda c: pl.BlockSpec((1, TS, MLSTM_QK), lambda b, d, t: (b, chunk(d, t), c))
    G = MLSTM_GATES
    return pl.pallas_call(
        _mlstm_kernel, grid=(B, 2, nc),
        in_specs=[blk(0), blk(1), blk(2),
                  pl.BlockSpec((1, TS, G), lambda b, d, t: (b, chunk(d, t), 0)),
                  pl.BlockSpec((1, G, TS), lambda b, d, t: (b, 0, chunk(d, t))),
                  pl.BlockSpec((1, G), lambda b, d, t: (0, 0)),
                  pl.BlockSpec((G, 1), lambda b, d, t: (0, 0))],
        out_specs=pl.BlockSpec((1, 1, TS, MLSTM_V), lambda b, d, t: (d, b, chunk(d, t), 0)),
        out_shape=jax.ShapeDtypeStruct((2, B, S, MLSTM_V), F32),
        scratch_shapes=[pltpu.VMEM((MLSTM_HEADS, MLSTM_DK, MLSTM_DV), F32),
                        pltpu.VMEM((MLSTM_HEADS, 1, MLSTM_DK), F32),
                        pltpu.VMEM((MLSTM_HEADS, 1, 1), F32)],
        compiler_params=_cp("parallel", "arbitrary", "arbitrary"), name="mlstm",
    )(P, P, P, gcol, grow, gate_b.reshape(1, G), gate_b.reshape(G, 1))


def _mlstm_out_kernel(hf_ref, hr_ref, o_ref, gain_ref, out_ref):
    dv = MLSTM_DV
    for h in range(MLSTM_HEADS):
        sl = slice(h * dv, (h + 1) * dv)
        x = hf_ref[0, 0, :, sl] + hr_ref[0, 0, :, sl]
        out_ref[0, :, sl] = (_rms(x, gain_ref[:, sl]) * _sigmoid(o_ref[0, :, sl])).astype(out_ref.dtype)


def _mlstm_out(Hd, OG, gain):
    _, B, S, V = Hd.shape
    return pl.pallas_call(
        _mlstm_out_kernel, grid=(B, S // TS),
        in_specs=[pl.BlockSpec((1, 1, TS, V), lambda b, s: (0, b, s, 0)),
                  pl.BlockSpec((1, 1, TS, V), lambda b, s: (1, b, s, 0)),
                  pl.BlockSpec((1, TS, V), lambda b, s: (b, s, 0)),
                  pl.BlockSpec((1, V), lambda b, s: (0, 0))],
        out_specs=pl.BlockSpec((1, TS, V), lambda b, s: (b, s, 0)),
        out_shape=jax.ShapeDtypeStruct((B, S, V), BF16),
        compiler_params=_cp("parallel", "parallel"), name="mlstm_out",
    )(Hd, Hd, OG, gain.reshape(1, V))


NA_QR = 4
NA_KR = NA_KH + 2 * NA_QR - 4
NA_GROUPS = 4


def _na_bias(rpb):
    a = np.arange(NA_QR)
    kr = np.arange(NA_KR)
    qrel = NA_QR * np.arange(3)[:, None] + a[None, :]
    r0rel = np.stack([0 * a, a, NA_KR - NA_KH + 0 * a])
    valid_r = (kr[None, None] >= r0rel[..., None]) & (kr[None, None] < r0rel[..., None] + NA_KH)
    drow = np.clip(kr[None, None] - qrel[..., None] + (NA_KH - 1), 0, 2 * NA_KH - 2)
    w = np.arange(GRID_W)[:, None]
    cc = np.arange(GRID_W)[None, :]
    col_start = np.clip(w - NA_KW // 2, 0, GRID_W - NA_KW)
    valid_c = (cc >= col_start) & (cc < col_start + NA_KW)
    dcol = np.clip(cc - w + (NA_KW - 1), 0, 2 * NA_KW - 2)
    pick_r = jnp.asarray(np.eye(2 * NA_KH - 1, dtype=np.float32)[drow])
    pick_c = jnp.asarray(np.eye(2 * NA_KW - 1, dtype=np.float32)[dcol])
    t = jnp.einsum("cakr,hrs,wzs->chawkz", pick_r, rpb.astype(F32), pick_c,
                   precision=lax.Precision.HIGHEST)
    valid = valid_r[:, None, :, None, :, None] & valid_c[None, None, None, :, None, :]
    t = jnp.where(valid, t, NEG)
    return t.reshape(3, NA_HEADS, NA_QR * GRID_W, NA_KR * GRID_W)


def _na_kernel(q_ref, k_ref, v_ref, bias_ref, o_ref, *, Lc, rows_n):
    W, d = GRID_W, NA_DIM
    scale = d ** -0.5
    kc = k_ref[0, 0:Lc, :]
    vc = v_ref[0, 0:Lc, :]
    sc = _dot_nt(q_ref[0, 0:Lc, :], kc) * scale
    pc = jnp.exp(sc - jnp.max(sc, axis=-1, keepdims=True))
    oc = _dot(pc.astype(BF16), vc) / jnp.sum(pc, axis=-1, keepdims=True)
    o_ref[0, 0:Lc, :] = oc.astype(o_ref.dtype)

    def groups(it, carry):
        geo, raw = [], []
        for j in range(NA_GROUPS):
            r = (it * NA_GROUPS + j) * NA_QR
            kr0 = jnp.clip(r - NA_KH // 2, 0, rows_n - NA_KR)
            qoff = pl.multiple_of(Lc + r * W, NA_QR * W)
            koff = pl.multiple_of(Lc + kr0 * W, W)
            qg = q_ref[0, pl.ds(qoff, NA_QR * W), :]
            geo.append((qoff, koff, (r - kr0) // NA_QR))
            raw.append((_dot_nt(qg, k_ref[0, pl.ds(koff, NA_KR * W), :]), _dot_nt(qg, kc)))
        probs = []
        for (qoff, koff, cls), (qk_w, qk_x) in zip(geo, raw):
            sw = qk_w * scale + bias_ref[cls, 0]
            sx = qk_x * scale
            m = jnp.maximum(jnp.max(sw, axis=-1, keepdims=True), jnp.max(sx, axis=-1, keepdims=True))
            pw = jnp.exp(sw - m)
            px = jnp.exp(sx - m)
            probs.append((pw, px, jnp.sum(pw, axis=-1, keepdims=True) + jnp.sum(px, axis=-1, keepdims=True)))
        for (qoff, koff, cls), (pw, px, l) in zip(geo, probs):
            o = (_dot(pw.astype(BF16), v_ref[0, pl.ds(koff, NA_KR * W), :]) + _dot(px.astype(BF16), vc)) / l
            o_ref[0, pl.ds(qoff, NA_QR * W), :] = o.astype(o_ref.dtype)
        return carry

    lax.fori_loop(0, rows_n // (NA_QR * NA_GROUPS), groups, 0)


def _na(P, bias, Lc, col0):
    B, S, _ = P.shape
    rows_n = (S - Lc) // GRID_W
    assert rows_n % (NA_QR * NA_GROUPS) == 0 and rows_n >= NA_KR
    c0 = col0 // NA_DIM
    blk = lambda c: pl.BlockSpec((1, S, NA_DIM), lambda b, h: (b, 0, c0 + c * NA_HEADS + h))
    return pl.pallas_call(
        functools.partial(_na_kernel, Lc=Lc, rows_n=rows_n), grid=(B, NA_HEADS),
        in_specs=[blk(0), blk(1), blk(2),
                  pl.BlockSpec((3, 1, NA_QR * GRID_W, NA_KR * GRID_W), lambda b, h: (0, h, 0, 0))],
        out_specs=pl.BlockSpec((1, S, NA_DIM), lambda b, h: (b, 0, h)),
        out_shape=jax.ShapeDtypeStruct((B, S, NA_WIDTH), BF16),
        compiler_params=_cp("parallel", "parallel"), name="na",
    )(P, P, P, bias)


def _rope_tables(Lc, N):
    quarter = GQA_DIM // 4
    t = jnp.arange(N)
    inv = ROPE_THETA ** (-(jnp.arange(quarter, dtype=F32) / quarter))
    ang_r = (t // GRID_W).astype(F32)[:, None] * inv[None, :]
    ang_c = (t % GRID_W).astype(F32)[:, None] * inv[None, :]
    cos = jnp.concatenate([jnp.cos(ang_r)] * 2 + [jnp.cos(ang_c)] * 2, axis=-1)
    sin = jnp.concatenate([-jnp.sin(ang_r), jnp.sin(ang_r), -jnp.sin(ang_c), jnp.sin(ang_c)], axis=-1)
    cos = jnp.concatenate([jnp.ones((Lc, GQA_DIM), F32), cos], axis=0)
    sin = jnp.concatenate([jnp.zeros((Lc, GQA_DIM), F32), sin], axis=0)
    return jnp.tile(cos, (1, LANE // GQA_DIM)), jnp.tile(sin, (1, LANE // GQA_DIM))


def _rope_rot(x, cos, sin):
    quarter = GQA_DIM // 4
    lane = lax.broadcasted_iota(jnp.int32, x.shape, 1)
    first = (lane % (2 * quarter)) < quarter
    rot = jnp.where(first, pltpu.roll(x, LANE - quarter, 1), pltpu.roll(x, quarter, 1))
    return x * cos + rot * sin


def _gqa_kernel(sink_ref, q_ref, k_ref, v_ref, cos_ref, sin_ref, o_ref, kd_ref, vd_ref, *, Lc, N):
    W, dh, KV = GQA_WINDOW, GQA_DIM, GQA_KV_HEADS
    G = GQA_HEADS // KV
    S = Lc + N
    i = pl.program_id(1)
    scale = dh ** -0.5

    @pl.when(i == 0)
    def _():
        low_t = lax.broadcasted_iota(jnp.int32, (TS, LANE), 1) < dh
        zero = jnp.zeros((W, LANE), BF16)
        for kh in range(KV):
            for ref in (kd_ref, vd_ref):
                ref[kh, Lc:Lc + W, :] = zero
                ref[kh, Lc + W + N:Lc + 2 * W + N, :] = zero

        def build(c, carry):
            r0 = pl.multiple_of(c * TS, TS)
            dst = pl.multiple_of(r0 + jnp.where(c >= Lc // TS, W, 0), W)
            cs = cos_ref[pl.ds(r0, TS), :]
            sn = sin_ref[pl.ds(r0, TS), :]
            for hh in range(KV // 2):
                kr = _rope_rot(k_ref[0, pl.ds(r0, TS), hh * LANE:(hh + 1) * LANE], cs, sn)
                vx = v_ref[0, pl.ds(r0, TS), hh * LANE:(hh + 1) * LANE]
                for ref, a in ((kd_ref, kr), (vd_ref, vx)):
                    sw = pltpu.roll(a, dh, 1)
                    ref[2 * hh, pl.ds(dst, TS), :] = jnp.where(low_t, a, sw).astype(BF16)
                    ref[2 * hh + 1, pl.ds(dst, TS), :] = jnp.where(low_t, sw, a).astype(BF16)
            return carry

        lax.fori_loop(0, S // TS, build, 0)

    span = 3 * W
    off = pl.multiple_of(Lc + i * W, W)
    cq = cos_ref[pl.ds(off, W), :]
    sq = sin_ref[pl.ds(off, W), :]
    ii = lax.broadcasted_iota(jnp.int32, (G * W, span), 0) % W
    jj = lax.broadcasted_iota(jnp.int32, (G * W, span), 1)
    kpos = i * W - W + jj
    valid = (kpos >= 0) & (kpos < N) & (jnp.abs(ii - (jj - W)) <= W)
    rg = lax.broadcasted_iota(jnp.int32, (G * W, 1), 0) // W
    low = lax.broadcasted_iota(jnp.int32, (W, LANE), 1) < dh
    raw = []
    for kh in range(KV):
        parts = []
        for slab in range(G // 2):
            j = kh * (G // 2) + slab
            qr = _rope_rot(q_ref[0, :, j * LANE:(j + 1) * LANE] * scale, cq, sq)
            parts += [jnp.where(low, qr, 0.0).astype(BF16), jnp.where(low, 0.0, qr).astype(BF16)]
        q4 = jnp.concatenate(parts, axis=0)
        raw.append((_dot_nt(q4, kd_ref[kh, pl.ds(off, span), :]), _dot_nt(q4, kd_ref[kh, 0:Lc, :])))
    for kh in range(KV):
        vw = vd_ref[kh, pl.ds(off, span), :]
        vc = vd_ref[kh, 0:Lc, :]
        sw = jnp.where(valid, raw[kh][0], NEG)
        sx = raw[kh][1]
        sk = sink_ref[G * kh + G - 1]
        for g in range(G - 2, -1, -1):
            sk = jnp.where(rg == g, sink_ref[G * kh + g], sk)
        m = jnp.maximum(jnp.maximum(jnp.max(sw, axis=-1, keepdims=True),
                                    jnp.max(sx, axis=-1, keepdims=True)), sk)
        pw = jnp.exp(sw - m)
        px = jnp.exp(sx - m)
        l = jnp.sum(pw, axis=-1, keepdims=True) + jnp.sum(px, axis=-1, keepdims=True) + jnp.exp(sk - m)
        o = (_dot(pw.astype(BF16), vw) + _dot(px.astype(BF16), vc)) / l
        for slab in range(G // 2):
            j = kh * (G // 2) + slab
            o_ref[0, :, j * LANE:(j + 1) * LANE] = jnp.where(
                low, o[2 * slab * W:(2 * slab + 1) * W], o[(2 * slab + 1) * W:(2 * slab + 2) * W]
            ).astype(o_ref.dtype)


def _gqa(P1, cos, sin, sink, Lc):
    B, S, _ = P1.shape
    N = S - Lc
    W = GQA_WINDOW
    rows = Lc + 2 * W + N
    table = pl.BlockSpec((S, LANE), lambda b, i: (0, 0))
    return pl.pallas_call(
        functools.partial(_gqa_kernel, Lc=Lc, N=N), grid=(B, N // W),
        in_specs=[pl.BlockSpec(memory_space=pltpu.SMEM),
                  pl.BlockSpec((1, W, GQA_Q), lambda b, i: (b, i + Lc // W, 0)),
                  pl.BlockSpec((1, S, GQA_KV), lambda b, i: (b, 0, GQA_Q // GQA_KV)),
                  pl.BlockSpec((1, S, GQA_KV), lambda b, i: (b, 0, GQA_Q // GQA_KV + 1)),
                  table, table],
        out_specs=pl.BlockSpec((1, W, GQA_Q), lambda b, i: (b, i, 0)),
        out_shape=jax.ShapeDtypeStruct((B, N, GQA_Q), BF16),
        scratch_shapes=[pltpu.VMEM((GQA_KV_HEADS, rows, LANE), BF16),
                        pltpu.VMEM((GQA_KV_HEADS, rows, LANE), BF16)],
        compiler_params=_cp("parallel", "arbitrary"), name="gqa",
    )(sink, P1, P1, P1, cos, sin)


def _gelu_tanh(x):
    return 0.5 * x * (1.0 + jnp.tanh(np.sqrt(2.0 / np.pi) * (x + 0.044715 * (x * x * x))))


def _lru_kernel(x_ref, g_ref, cw_ref, cb_ref, gw_ref, gb_ref, lam_ref, o_ref,
                af_ref, bf_ref, ar_ref, br_ref, *, Lc, N):
    S = Lc + N
    CH = 128
    LW = x_ref.shape[-1]
    nb = LW // LRU_BS
    nt = S // SUB
    ntc = Lc // SUB
    row = lax.broadcasted_iota(jnp.int32, (CH, LW), 0)
    crow = row % SUB
    first_row = row == 0
    last_row = row == CH - 1
    last_row2 = row == CH - 2
    sps = []
    for d in range(2):
        lam = lam_ref[d:d + 1, :]
        sps.append(jnp.maximum(-lam, 0.0) + jnp.log1p(jnp.exp(-jnp.abs(lam))))

    def gates(c, carry):
        off = pl.multiple_of(c * CH, CH)
        xc = x_ref[0, pl.ds(off, CH), :]
        prev = x_ref[0, pl.ds(pl.multiple_of(jnp.maximum(off - SUB, 0), SUB), SUB), :]
        nxt = x_ref[0, pl.ds(pl.multiple_of(jnp.minimum(off + CH, S - SUB), SUB), SUB), :]
        pz = jnp.where((c == 0) | (c == Lc // CH), 0.0, 1.0)
        nz = jnp.where((c == Lc // CH - 1) | (c == S // CH - 1), 0.0, 1.0)
        p7 = prev[SUB - 1:SUB, :] * pz
        n0 = nxt[0:1, :] * nz
        n1 = nxt[1:2, :] * nz
        xm1 = jnp.where(first_row, p7, pltpu.roll(xc, 1, 0))
        xp1 = jnp.where(last_row, n0, pltpu.roll(xc, CH - 1, 0))
        xp2 = jnp.where(last_row2, n0, jnp.where(last_row, n1, pltpu.roll(xc, CH - 2, 0)))
        u = (xm1 * cw_ref[0:1, :] + cb_ref[...] + xc * cw_ref[1:2, :] + xp1 * cw_ref[2:3, :]
             + xp2 * cw_ref[3:4, :])
        ub = u.astype(BF16)
        for d, (a_ref, b_ref) in enumerate(((af_ref, bf_ref), (ar_ref, br_ref))):
            pre = []
            for gi in range(2):
                pre.append(jnp.concatenate(
                    [_dot(ub[:, k * LRU_BS:(k + 1) * LRU_BS], gw_ref[d, gi, k].astype(BF16))
                     for k in range(nb)], axis=-1) + gb_ref[d, gi:gi + 1, :])
            r = _sigmoid(pre[0])
            ig = _sigmoid(pre[1])
            a = jnp.exp(-LRU_C * r * sps[d])
            b = jnp.sqrt(1.0 - a * a) * (ig * u)
            for s in (1, 2, 4):
                if d == 0:
                    keep = crow >= s
                    a_sh = pltpu.roll(a, s, 0)
                    b_sh = pltpu.roll(b, s, 0)
                else:
                    keep = crow < SUB - s
                    a_sh = pltpu.roll(a, CH - s, 0)
                    b_sh = pltpu.roll(b, CH - s, 0)
                b = jnp.where(keep, a * b_sh + b, b)
                a = jnp.where(keep, a * a_sh, a)
            a_ref[pl.ds(off, CH), :] = a
            b_ref[pl.ds(off, CH), :] = b
        return carry

    lax.fori_loop(0, S // CH, gates, 0)

    def scan(ft, rt, carry):
        hf, hr = carry
        of = pl.multiple_of(ft * SUB, SUB)
        orv = pl.multiple_of(rt * SUB, SUB)
        tf = af_ref[pl.ds(of, SUB), :] * hf + bf_ref[pl.ds(of, SUB), :]
        tr = ar_ref[pl.ds(orv, SUB), :] * hr + br_ref[pl.ds(orv, SUB), :]
        bf_ref[pl.ds(of, SUB), :] = tf
        br_ref[pl.ds(orv, SUB), :] = tr
        return (jnp.broadcast_to(tf[SUB - 1:SUB, :], (SUB, LW)), jnp.broadcast_to(tr[0:1, :], (SUB, LW)))

    zero = jnp.zeros((SUB, LW), F32)
    carry = lax.fori_loop(0, ntc, lambda i, c: scan(i, ntc - 1 - i, c), (zero, zero), unroll=8)
    lax.fori_loop(ntc, nt, lambda i, c: scan(i, nt - 1 - (i - ntc), c), carry, unroll=8)

    def outp(c, carry):
        off = pl.multiple_of(c * CH, CH)
        src = pl.multiple_of(Lc + off, CH)
        hs = bf_ref[pl.ds(src, CH), :] + br_ref[pl.ds(src, CH), :]
        o_ref[0, pl.ds(off, CH), :] = (hs * _gelu_tanh(g_ref[0, pl.ds(src, CH), :])).astype(o_ref.dtype)
        return carry

    lax.fori_loop(0, N // CH, outp, 0)


def _lru(P1, col_x, col_g, conv_w, conv_b, gate_w, gate_b, lam, Lc):
    B, S, _ = P1.shape
    N = S - Lc
    nb = 2
    lw = nb * LRU_BS
    return pl.pallas_call(
        functools.partial(_lru_kernel, Lc=Lc, N=N), grid=(B, LRU_BLOCKS // nb),
        in_specs=[pl.BlockSpec((1, S, lw), lambda b, j: (b, 0, col_x // lw + j)),
                  pl.BlockSpec((1, S, lw), lambda b, j: (b, 0, col_g // lw + j)),
                  pl.BlockSpec((4, lw), lambda b, j: (0, j)),
                  pl.BlockSpec((1, lw), lambda b, j: (0, j)),
                  pl.BlockSpec((2, 2, nb, LRU_BS, LRU_BS), lambda b, j: (0, 0, j, 0, 0)),
                  pl.BlockSpec((2, 2, lw), lambda b, j: (0, 0, j)),
                  pl.BlockSpec((2, lw), lambda b, j: (0, j))],
        out_specs=pl.BlockSpec((1, N, lw), lambda b, j: (b, 0, j)),
        out_shape=jax.ShapeDtypeStruct((B, N, LRU_WIDTH), BF16),
        scratch_shapes=[pltpu.VMEM((S, lw), F32) for _ in range(4)],
        compiler_params=_cp("parallel", "parallel"), name="lru",
    )(P1, P1, conv_w, conv_b.reshape(1, LRU_WIDTH), gate_w, gate_b, lam)


ROUTE_BISECT = 2048


def _cumsum_lanes(x):
    n = x.shape[-1]
    lane = lax.broadcasted_iota(jnp.int32, x.shape, 1)
    s = 1
    while s < n:
        x = x + jnp.where(lane >= s, pltpu.roll(x, s, 1), 0.0)
        s *= 2
    return x


def _route_kernel(lt_ref, idx_ref, g_ref, cm_ref, af_ref, *, segs, S):
    E = lt_ref.shape[1]
    b = pl.program_id(0)
    lt = lt_ref[0]
    ex = jnp.exp(lt - jnp.max(lt, axis=0, keepdims=True))
    aff = ex / jnp.sum(ex, axis=0, keepdims=True)
    lane_e = lax.broadcasted_iota(jnp.int32, (1, LANE), 1)
    slot0 = 0
    for lo, n, cap in segs:
        a = aff[:, lo:lo + n]

        def bisect(carry):
            lo_v, hi_v, c_lo, it, _ = carry
            mid = lo_v + 0.5 * (hi_v - lo_v)
            c = jnp.sum(jnp.where(a >= mid, 1.0, 0.0), axis=1, keepdims=True)
            ge = c >= cap
            lo_v = jnp.where(ge, mid, lo_v)
            hi_v = jnp.where(ge, hi_v, mid)
            c_lo = jnp.where(ge, c, c_lo)
            mid = lo_v + 0.5 * (hi_v - lo_v)
            done = (c_lo == cap) | (mid <= lo_v) | (mid >= hi_v)
            return lo_v, hi_v, c_lo, it + 1, jnp.min(jnp.where(done, 1, 0))

        lo_v, hi_v, _, _, _ = lax.while_loop(
            lambda carry: (carry[4] == 0) & (carry[3] < ROUTE_BISECT), bisect,
            (jnp.zeros((E, 1), F32), jnp.full((E, 1), 2.0, F32), jnp.full((E, 1), float(n), F32),
             jnp.int32(0), jnp.int32(0)))
        top = jnp.where(a >= hi_v, 1.0, 0.0)
        tie = jnp.where((a >= lo_v) & (a < hi_v), 1.0, 0.0)
        need = cap - jnp.sum(top, axis=1, keepdims=True)
        tie_rank = _cumsum_lanes(tie) - tie
        sel = top + tie * jnp.where(tie_rank < need, 1.0, 0.0)
        cm_ref[:, 0:n] = _cumsum_lanes(sel) * sel
        af_ref[:, 0:n] = a
        tok = (lax.broadcasted_iota(jnp.int32, (1, n), 1) + (lo + b * S)).astype(F32)
        jcol = (lax.broadcasted_iota(jnp.int32, (cap, 1), 0) + 1).astype(F32)

        def per_expert(e, carry):
            oi, og = carry
            ce = cm_ref[pl.ds(e, 1), 0:n]
            ae = af_ref[pl.ds(e, 1), 0:n]
            acc_i = jnp.zeros((cap, LANE), F32)
            acc_g = jnp.zeros((cap, LANE), F32)
            for k in range(n // LANE):
                sl = slice(k * LANE, (k + 1) * LANE)
                hit = ce[:, sl] == jcol
                acc_i = acc_i + jnp.where(hit, tok[:, sl], 0.0)
                acc_g = acc_g + jnp.where(hit, ae[:, sl], 0.0)
            mine = lane_e == e
            return (jnp.where(mine, jnp.sum(acc_i, axis=1, keepdims=True), oi),
                    jnp.where(mine, jnp.sum(acc_g, axis=1, keepdims=True), og))

        zero = jnp.zeros((cap, LANE), F32)
        oi, og = lax.fori_loop(0, E, per_expert, (zero, zero))
        idx_ref[0, slot0:slot0 + cap, :] = oi
        g_ref[0, slot0:slot0 + cap, :] = og
        slot0 += cap


def _route(logits_t, segs):
    B, E, S = logits_t.shape
    slots = sum(cap for _, _, cap in segs)
    nmax = max(n for _, n, _ in segs)
    out = jax.ShapeDtypeStruct((B, slots, LANE), F32)
    blk = pl.BlockSpec((1, slots, LANE), lambda b: (b, 0, 0))
    return pl.pallas_call(
        functools.partial(_route_kernel, segs=segs, S=S), grid=(B,),
        in_specs=[pl.BlockSpec((1, E, S), lambda b: (b, 0, 0))],
        out_specs=[blk, blk], out_shape=[out, out],
        scratch_shapes=[pltpu.VMEM((E, nmax), F32), pltpu.VMEM((E, nmax), F32)],
        compiler_params=_cp("parallel"), name="route",
    )(logits_t)


RING = 4


def _ffn_kernel(ids_ref, h_hbm, w1_ref, w3_ref, w2_ref, g_ref, zero_hbm, out_hbm,
                xf_ref, xb_ref, y_ref, ring_ref, xsem, asem, ssem, *, rm, ns, nf):
    del zero_hbm
    C = rm // nf
    D = xf_ref.shape[-1]
    step = pl.program_id(0)
    f = pl.program_id(1)
    q = step * nf + f
    cur = step % 2
    f0 = pl.multiple_of(f * C, SUB)

    def chunk_tokens(c_step, c_f):
        p = c_step - 1
        return jnp.where(p < 0, rm, jnp.minimum(p, ns - 1) * rm) + c_f * C

    def gather(src, dst, sem, tb, row0, n):
        for i in range(n):
            pltpu.make_async_copy(src.at[pl.ds(ids_ref[tb + i], 1), :], dst.at[pl.ds(row0 + i, 1), :], sem).start()

    def scatter(src, tb, n, sem):
        for i in range(n):
            pltpu.make_async_copy(src.at[pl.ds(i, 1), :], out_hbm.at[pl.ds(ids_ref[tb + i], 1), :], sem).start()

    def wait_rows(buf, sem):
        pltpu.make_async_copy(buf, buf, sem).wait()

    chunk = ring_ref.at[0]
    tb_prev = jnp.where(f == 0, chunk_tokens(step - 1, nf - 1), chunk_tokens(step, f - 1))
    tb_next = jnp.where(f == nf - 1, chunk_tokens(step + 1, 0), chunk_tokens(step, f + 1))

    @pl.when(q == 0)
    def _():
        ring_ref[...] = jnp.zeros_like(ring_ref)
        y_ref[1] = jnp.zeros((rm, D), F32)

        def first(r, carry):
            pltpu.make_async_copy(h_hbm.at[pl.ds(ids_ref[r], 1), :], xf_ref.at[pl.ds(r, 1), :], xsem).start()
            return carry
        lax.fori_loop(0, rm, first, 0, unroll=8)
        scatter(y_ref.at[1, pl.ds(0, C)], chunk_tokens(0, nf - 2), C, ssem.at[0])
        scatter(y_ref.at[1, pl.ds(0, C)], chunk_tokens(0, nf - 3), C, ssem.at[1])
        gather(out_hbm, ring_ref.at[0], asem.at[0], chunk_tokens(0, 0), 0, C)

    @pl.when(f == 0)
    def _():
        wait_rows(xf_ref, xsem)
        xb_ref[...] = xf_ref[...].astype(BF16)
        y_ref[cur] = jnp.zeros((rm, D), F32)

    wait_rows(chunk, ssem.at[q % 3])
    scatter(ring_ref.at[(q + RING - 1) % RING], tb_prev, C, ssem.at[(q + 2) % 3])
    gather(out_hbm, ring_ref.at[(q + 1) % RING], asem.at[(q + 1) % 2], tb_next, 0, C)
    gather(h_hbm, xf_ref, xsem, jnp.minimum(step + 1, ns - 1) * rm + f * C, f0, C)

    x = xb_ref[...]
    u = _dot(x, w1_ref[0, 0].astype(BF16))
    v = _dot(x, w3_ref[0, 0].astype(BF16))
    a = (u * _sigmoid(u) * v).astype(BF16)
    dc = _pick(D, (512, 256, 128))
    for c in range(D // dc):
        y_ref[cur, :, c * dc:(c + 1) * dc] += _dot(a, w2_ref[0, 0, :, c * dc:(c + 1) * dc].astype(BF16))

    slot = q % RING
    wait_rows(chunk, asem.at[q % 2])
    mine = lax.broadcasted_iota(jnp.int32, (C, LANE), 1) == jnp.clip(step - 1, 0, ns - 1) // 2
    gate = jnp.sum(jnp.where(mine, g_ref[pl.ds(f0, C), :], 0.0), axis=-1, keepdims=True)
    ring_ref[slot] = ring_ref[slot] + y_ref[1 - cur, pl.ds(f0, C), :] * gate

    @pl.when(q == (ns + 1) * nf - 1)
    def _():
        scatter(ring_ref.at[slot], chunk_tokens(step, f), C, ssem.at[q % 3])
        for k in range(3):
            wait_rows(chunk, ssem.at[k])
        wait_rows(chunk, asem.at[(q + 1) % 2])
        wait_rows(xf_ref, xsem)


def _ffn(hf, ids, g, w1, w3, w2, l):
    T, D = hf.shape
    E = w1.shape[1]
    R = g.shape[0]
    FF = w1.shape[-1]
    rm = R // 2
    ns = 2 * E
    tf = _pick(FF, (256, 128))
    nf = FF // tf
    assert nf >= 3 and rm % nf == 0 and (rm // nf) % SUB == 0
    ex = lambda s: jnp.minimum(s, ns - 1) // 2
    grid_spec = pltpu.PrefetchScalarGridSpec(
        num_scalar_prefetch=1, grid=(ns + 1, nf),
        in_specs=[pl.BlockSpec(memory_space=pl.ANY),
                  pl.BlockSpec((1, 1, D, tf), lambda s, f, ids: (l, ex(s), 0, f)),
                  pl.BlockSpec((1, 1, D, tf), lambda s, f, ids: (l, ex(s), 0, f)),
                  pl.BlockSpec((1, 1, tf, D), lambda s, f, ids: (l, ex(s), f, 0)),
                  pl.BlockSpec((rm, LANE), lambda s, f, ids: (jnp.maximum(s - 1, 0) % 2, 0)),
                  pl.BlockSpec(memory_space=pl.ANY)],
        out_specs=pl.BlockSpec(memory_space=pl.ANY),
        scratch_shapes=[pltpu.VMEM((rm, D), F32), pltpu.VMEM((rm, D), BF16),
                        pltpu.VMEM((2, rm, D), F32), pltpu.VMEM((RING, rm // nf, D), F32),
                        pltpu.SemaphoreType.DMA, pltpu.SemaphoreType.DMA((2,)), pltpu.SemaphoreType.DMA((3,))])
    return pl.pallas_call(
        functools.partial(_ffn_kernel, rm=rm, ns=ns, nf=nf), grid_spec=grid_spec,
        out_shape=jax.ShapeDtypeStruct((T, D), F32),
        input_output_aliases={6: 0},
        compiler_params=_cp("arbitrary", "arbitrary"), name="ffn",
    )(ids, hf, w1, w3, w2, g, jnp.zeros((T, D), F32))


def _moe(h, logits, w1, w3, w2, l, Lc):
    B, S, D = h.shape
    E = N_EXPERTS
    assert B % 2 == 0
    segs = tuple((lo, n, EC_FACTOR * n // E) for lo, n in ((Lc, S - Lc), (0, Lc)) if n)
    idx, g = _route(jnp.swapaxes(logits[:, :, :E], 1, 2), segs)
    ids = jnp.transpose(idx[:, :, :E], (2, 0, 1)).astype(jnp.int32).reshape(-1)
    return _ffn(h.reshape(B * S, D), ids, g.reshape(-1, LANE), w1, w3, w2, l).reshape(B, S, D)


def kernel(x, c, ctx, c_ctx, ada_w, ada_b, norm_mix_pre, norm_mix_post, norm_ffn_pre, norm_ffn_post,
           ab_w_in, ab_w_out, mlstm_gate_b, mlstm_norm, na_rpb,
           cd_w_in, cd_w_out, gqa_sink, lru_conv_w, lru_conv_b, lru_gate_w, lru_gate_b, lru_lambda,
           moe_router, moe_w1, moe_w3, moe_w2):
    B, N, D = x.shape
    Lc = ctx.shape[1]
    S = Lc + N
    assert Lc == TS and N % TS == 0 and B < SUB
    ctx_row = B
    cond = jnp.concatenate([c, c_ctx[None], jnp.zeros((SUB - B - 1, D), F32)], axis=0)
    mod = _ada(cond, ada_w, ada_b).reshape(-1, 1, D)
    router = jnp.pad(moe_router, ((0, 0), (0, 0), (0, LANE - N_EXPERTS))).astype(BF16)

    h = _prenorm(ctx, x, norm_mix_pre[0], mod, 0, ctx_row)
    hf = h.reshape(B * S, D)
    w_in = ab_w_in[0]
    o0 = 3 * MLSTM_QK
    n0 = o0 + MLSTM_V + MLSTM_GATES
    w_og = jnp.pad(w_in[:, o0:n0], ((0, 0), (0, LANE - MLSTM_GATES))).astype(BF16)
    P = _matmul(hf, w_in[:, :o0].astype(BF16), BF16).reshape(B, S, -1)
    PN = _matmul(hf, w_in[:, n0:].astype(BF16), BF16).reshape(B, S, -1)
    OG = _matmul(hf, w_og, F32).reshape(B, S, -1)
    gcol = OG[:, :, MLSTM_V:MLSTM_V + MLSTM_GATES]
    Hd = _mlstm(P, gcol, jnp.swapaxes(gcol, 1, 2), mlstm_gate_b[0])
    ML = _mlstm_out(Hd, OG, mlstm_norm[0])
    NL = _na(PN, _na_bias(na_rpb[0]), Lc, 0)
    Y = _matmul2(ML.reshape(B * S, -1), NL.reshape(B * S, -1), ab_w_out[0].astype(BF16)).reshape(B, S, D)
    xs, hb, lg = _resid((ctx, x), Y, norm_mix_post[0], mod, 0, 2, ctx_row, 0, (norm_ffn_pre[0], 0, 3, 4, router[0]))
    Y = _moe(hb, lg, moe_w1, moe_w3, moe_w2, 0, Lc)
    xs, h = _resid(xs, Y, norm_ffn_post[0], mod, 0, 5, ctx_row, 0, (norm_mix_pre[1], 1, 0, 1, None))

    P1 = _matmul(h.reshape(B * S, D), cd_w_in[0].astype(BF16), F32).reshape(B, S, -1)
    cos, sin = _rope_tables(Lc, N)
    AL = _gqa(P1, cos, sin, gqa_sink[0], Lc)
    RL = _lru(P1, GQA_Q + 2 * GQA_KV, GQA_Q + 2 * GQA_KV + LRU_WIDTH,
              lru_conv_w[0], lru_conv_b[0], lru_gate_w[0], lru_gate_b[0], lru_lambda[0], Lc)
    Y = _matmul2(AL.reshape(B * N, -1), RL.reshape(B * N, -1), cd_w_out[0].astype(BF16)).reshape(B, N, D)
    xl, hb, lg = _resid(xs, Y, norm_mix_post[1], mod, 1, 2, None, Lc // TS, (norm_ffn_pre[1], 1, 3, 4, router[1]))
    Y = _moe(hb, lg, moe_w1, moe_w3, moe_w2, 1, 0)
    (xl,) = _resid(xl, Y, norm_ffn_post[1], mod, 1, 5, None, 0, None)
    return xl
```
